```python
import jax, jax.numpy as jnp
from jax import lax
import numpy as np

D_MODEL = 1024
BATCH = 2
SEQ = 8192
DEPTH = 1

GRID_W = 64
CTX_LEN = 256

HEAD_DIM = 64
RWKV_WIDTH = D_MODEL
RWKV_HEADS = RWKV_WIDTH // HEAD_DIM
DECAY_LORA = 64
ICLR_LORA = 64
GATE_LORA = 128
GN_EPS = 64e-5
NORM_EPS = 1e-12
CONV_WIDTH = D_MODEL
CONV_GROUPS = CONV_WIDTH // HEAD_DIM
CONV_K = 3
N_EXPERTS = 32
TOP_K = 4
D_FF = D_MODEL
SWIGLU_LIMIT = 7.0
SWIGLU_ALPHA = 1.702
MOE_BLOCK = 256
RMS_EPS = 1e-6

K_OFF = 0
V_OFF = K_OFF + RWKV_WIDTH
DECAY_OFF = V_OFF + RWKV_WIDTH
ICLR_OFF = DECAY_OFF + 2 * DECAY_LORA
STATE_COLS = ICLR_OFF + 2 * ICLR_LORA
R_OFF = STATE_COLS
GLORA_OFF = R_OFF + RWKV_WIDTH
SHIFT_COLS = GLORA_OFF + GATE_LORA
CONV_OFF = SHIFT_COLS
GATE_OFF = CONV_OFF + 3 * CONV_WIDTH
N_IN = GATE_OFF + 2 * D_MODEL

kernel_name = 'hybrid_rwkv7_shortconv_moe_diffusion_block'


def _rmsnorm(x, g):
    xf = x.astype(jnp.float32)
    y = xf * lax.rsqrt(jnp.mean(xf * xf, axis=-1, keepdims=True) + RMS_EPS)
    return (y * g.astype(jnp.float32)).astype(x.dtype)


def _shift(z, axis, step):
    n = z.shape[axis]
    pad = [(0, 0)] * z.ndim
    if step > 0:
        body = lax.slice_in_dim(z, 0, n - step, axis=axis)
        pad[axis] = (step, 0)
    else:
        body = lax.slice_in_dim(z, -step, n, axis=axis)
        pad[axis] = (0, -step)
    return jnp.pad(body, pad)


def _qshift_grid(z, mu):
    b, t, ch = z.shape
    rows = t // GRID_W
    zg = z.reshape(b, rows, GRID_W, ch // 4, 4)
    shifted = jnp.stack([
        _shift(zg[..., 0], 2, 1),
        _shift(zg[..., 1], 2, -1),
        _shift(zg[..., 2], 1, 1),
        _shift(zg[..., 3], 1, -1),
    ], axis=-1).reshape(b, t, ch)
    return z + mu * (shifted - z)


def _qshift_seq(z, mu):
    b, t, ch = z.shape
    zg = z.reshape(b, t, ch // 4, 4)
    shifted = jnp.stack([
        _shift(zg[..., 0], 1, 1),
        _shift(zg[..., 1], 1, -1),
        _shift(zg[..., 2], 1, 1),
        _shift(zg[..., 3], 1, -1),
    ], axis=-1).reshape(b, t, ch)
    return z + mu * (shifted - z)


def _heads(u):
    return u.reshape(u.shape[0], u.shape[1], RWKV_HEADS, HEAD_DIM).astype(jnp.float32)


def _rwkv_state_terms(zs, w0, w_decay_up, a0, w_iclr_up, k_k, k_a):
    f32 = jnp.float32
    k = zs[..., K_OFF:K_OFF + RWKV_WIDTH]
    v = zs[..., V_OFF:V_OFF + RWKV_WIDTH]
    kk = _heads(k * k_k)
    kk = kk / jnp.maximum(jnp.sqrt(jnp.sum(kk * kk, axis=-1, keepdims=True)), NORM_EPS)
    dirs = []
    for d in range(2):
        lw = zs[..., DECAY_OFF + d * DECAY_LORA:DECAY_OFF + (d + 1) * DECAY_LORA]
        la = zs[..., ICLR_OFF + d * ICLR_LORA:ICLR_OFF + (d + 1) * ICLR_LORA]
        w_log = -jax.nn.softplus(-(w0[d] + jnp.tanh(lw) @ w_decay_up[d]).astype(f32)) - 0.5
        decay = jnp.exp(-jnp.exp(w_log))
        a = jax.nn.sigmoid((a0[d] + la @ w_iclr_up[d]).astype(f32))
        k_dir = k.astype(f32) * (1.0 + (a - 1.0) * k_a.astype(f32))
        dirs.append((_heads(decay), _heads(k_dir), kk * _heads(a)))
    return _heads(v), kk, dirs


def _wkv_scan(s0, decay, k, v, kk, kka, r, reverse):
    seq = [decay, k, v, kk, kka] + ([] if r is None else [r])
    xs = tuple(jnp.moveaxis(u, 1, 0) for u in seq)

    def step(s, inp):
        w_t, k_t, v_t, kk_t, kka_t = inp[:5]
        sa = jnp.einsum('bhvk,bhk->bhv', s, kk_t)
        s = (s * w_t[:, :, None, :] - sa[..., None] * kka_t[:, :, None, :]
             + v_t[..., None] * k_t[:, :, None, :])
        y = jnp.einsum('bhvk,bhk->bhv', s, inp[5]) if len(inp) == 6 else None
        return s, y

    s_fin, ys = lax.scan(step, s0, xs, reverse=reverse)
    return s_fin, (None if r is None else jnp.moveaxis(ys, 0, 1))


def _conv3(u, w):
    return lax.conv_general_dilated(
        u, w[:, None, :].astype(u.dtype), window_strides=(1,), padding=((1, 1),),
        dimension_numbers=('NWC', 'WIO', 'NWC'), feature_group_count=u.shape[-1])


def _merge_branches(z, r, lg, y, k_bonus, v, r_k, ln_x_w, ln_x_b, w_gate_up, conv_w, w_out):
    b, t, _ = z.shape
    mean = jnp.mean(y, axis=-1, keepdims=True)
    var = jnp.mean(jnp.square(y - mean), axis=-1, keepdims=True)
    yn = ((y - mean) * lax.rsqrt(var + GN_EPS)).reshape(b, t, RWKV_WIDTH) * ln_x_w + ln_x_b
    bonus = (jnp.sum(r * k_bonus * r_k, axis=-1, keepdims=True) * v).reshape(b, t, RWKV_WIDTH)
    g = jax.nn.sigmoid(lg) @ w_gate_up
    y_rwkv = ((yn + bonus) * g).astype(z.dtype)
    gb = z[..., CONV_OFF:CONV_OFF + CONV_WIDTH]
    gc = z[..., CONV_OFF + CONV_WIDTH:CONV_OFF + 2 * CONV_WIDTH]
    xin = z[..., CONV_OFF + 2 * CONV_WIDTH:CONV_OFF + 3 * CONV_WIDTH]
    y_conv = gb * _conv3(gc * xin, conv_w)
    ga = z[..., GATE_OFF:GATE_OFF + D_MODEL]
    gbr = z[..., GATE_OFF + D_MODEL:GATE_OFF + 2 * D_MODEL]
    merged = jax.nn.sigmoid(ga) * y_rwkv + jax.nn.sigmoid(gbr) * y_conv
    return merged @ w_out


def _moe(h, w_router, b_router, w_exp_in, b_exp_in, w_exp_out, b_exp_out):
    n, d = h.shape
    logits = (h @ w_router + b_router).astype(jnp.float32)
    top_val, top_idx = lax.top_k(logits, TOP_K)
    gates = jax.nn.softmax(top_val, axis=-1)
    nk = n * TOP_K
    flat_e = top_idx.reshape(-1).astype(jnp.int32)
    flat_tok = jnp.arange(nk, dtype=jnp.int32) // TOP_K
    flat_g = gates.reshape(-1)
    order = jnp.argsort(flat_e)
    e_sorted, tok_sorted, g_sorted = flat_e[order], flat_tok[order], flat_g[order]
    counts = jnp.bincount(flat_e, length=N_EXPERTS).astype(jnp.int32)
    starts = jnp.cumsum(counts) - counts
    padded = (counts + MOE_BLOCK - 1) // MOE_BLOCK * MOE_BLOCK
    pad_ends = jnp.cumsum(padded)
    pad_starts = pad_ends - padded
    dest = pad_starts[e_sorted] + jnp.arange(nk, dtype=jnp.int32) - starts[e_sorted]
    n_blocks = -(-(nk + N_EXPERTS * (MOE_BLOCK - 1)) // MOE_BLOCK)
    n_rows = n_blocks * MOE_BLOCK
    row_tok = jnp.zeros((n_rows,), jnp.int32).at[dest].set(tok_sorted)
    row_g = jnp.zeros((n_rows,), jnp.float32).at[dest].set(g_sorted)
    block_e = jnp.minimum(
        jnp.searchsorted(pad_ends, jnp.arange(n_blocks, dtype=jnp.int32) * MOE_BLOCK, side='right'),
        N_EXPERTS - 1)
    xb = h[row_tok].reshape(n_blocks, MOE_BLOCK, d)

    def expert_block(args):
        xblk, e = args
        gu = xblk @ w_exp_in[e] + b_exp_in[e]
        gate = jnp.minimum(gu[:, :D_FF], SWIGLU_LIMIT)
        up = jnp.clip(gu[:, D_FF:], -SWIGLU_LIMIT, SWIGLU_LIMIT)
        act = (up + 1.0) * gate * jax.nn.sigmoid(SWIGLU_ALPHA * gate)
        return act @ w_exp_out[e] + b_exp_out[e]

    yb = lax.map(expert_block, (xb, block_e)).reshape(n_rows, d)
    return jax.ops.segment_sum(yb * row_g[:, None].astype(yb.dtype), row_tok, num_segments=n)


def setup_inputs(seed: int = 0) -> dict:
    key = jax.random.key(seed)
    ks = jax.random.split(key, 30)
    f32 = jnp.float32
    L, D = DEPTH, D_MODEL

    def nrm(i, shape, scale):
        return jax.random.normal(ks[i], shape, f32) * scale

    def unif(i, shape, lo, hi):
        return jax.random.uniform(ks[i], shape, f32, lo, hi)

    return {
        'x': nrm(0, (BATCH, SEQ, D), 1.0),
        'c': nrm(1, (BATCH, D), 1.0),
        'ctx': nrm(2, (BATCH, CTX_LEN, D), 1.0),
        'c_ctx': nrm(3, (D,), 1.0),
        'w_ada': nrm(4, (L, D, 6 * D), 0.5 * D ** -0.5),
        'b_ada': nrm(5, (L, 6 * D), 0.01),
        'norm_pre_mix': 1.0 + nrm(6, (L, D), 0.02),
        'norm_post_mix': 1.0 + nrm(7, (L, D), 0.02),
        'norm_pre_ffn': 1.0 + nrm(8, (L, D), 0.02),
        'norm_post_ffn': 1.0 + nrm(9, (L, D), 0.02),
        'w_in': nrm(10, (L, D, N_IN), D ** -0.5),
        'mu_shift': unif(11, (L, SHIFT_COLS), 0.0, 0.5),
        'w0': unif(12, (L, 2, RWKV_WIDTH), -6.0, -1.0),
        'w_decay_up': nrm(13, (L, 2, DECAY_LORA, RWKV_WIDTH), 0.1),
        'a0': nrm(14, (L, 2, RWKV_WIDTH), 0.1),
        'w_iclr_up': nrm(15, (L, 2, ICLR_LORA, RWKV_WIDTH), ICLR_LORA ** -0.5),
        'k_k': 0.85 + nrm(16, (L, RWKV_WIDTH), 0.05),
        'k_a': 1.0 + nrm(17, (L, RWKV_WIDTH), 0.05),
        'r_k': nrm(18, (L, RWKV_HEADS, HEAD_DIM), 0.1),
        'w_gate_up': nrm(19, (L, GATE_LORA, RWKV_WIDTH), GATE_LORA ** -0.5),
        'ln_x_w': 1.0 + nrm(20, (L, RWKV_WIDTH), 0.02),
        'ln_x_b': nrm(21, (L, RWKV_WIDTH), 0.01),
        'conv_w': nrm(22, (L, CONV_K, CONV_WIDTH), CONV_K ** -0.5),
        'w_out': nrm(23, (L, D, D), D ** -0.5),
        'w_router': nrm(24, (L, D, N_EXPERTS), D ** -0.5),
        'b_router': nrm(25, (L, N_EXPERTS), 0.01),
        'w_exp_in': nrm(26, (L, N_EXPERTS, D, 2 * D_FF), D ** -0.5),
        'b_exp_in': nrm(27, (L, N_EXPERTS, 2 * D_FF), 0.01),
        'w_exp_out': nrm(28, (L, N_EXPERTS, D_FF, D), D_FF ** -0.5),
        'b_exp_out': nrm(29, (L, N_EXPERTS, D), 0.01),
    }


def reference(x, c, ctx, c_ctx, w_ada, b_ada, norm_pre_mix, norm_post_mix, norm_pre_ffn, norm_post_ffn,
              w_in, mu_shift, w0, w_decay_up, a0, w_iclr_up, k_k, k_a, r_k, w_gate_up, ln_x_w, ln_x_b,
              conv_w, w_out, w_router, b_router, w_exp_in, b_exp_in, w_exp_out, b_exp_out):
    xc = ctx
    b = x.shape[0]
    for l in range(DEPTH):
        last = l == DEPTH - 1
        mod = jax.nn.silu(c) @ w_ada[l] + b_ada[l]
        mod_c = jax.nn.silu(c_ctx) @ w_ada[l] + b_ada[l]
        sh1, sc1, g1, sh2, sc2, g2 = jnp.split(mod[:, None, :], 6, axis=-1)
        csh1, csc1, cg1, csh2, csc2, cg2 = jnp.split(mod_c, 6)
        rw_state = (w0[l], w_decay_up[l], a0[l], w_iclr_up[l], k_k[l], k_a[l])
        rw_read = (r_k[l], ln_x_w[l], ln_x_b[l], w_gate_up[l], conv_w[l], w_out[l])

        h = _rmsnorm(x, norm_pre_mix[l]) * (1.0 + sc1) + sh1
        hc = _rmsnorm(xc, norm_pre_mix[l]) * (1.0 + csc1) + csh1
        z = h @ w_in[l]
        zs = _qshift_grid(z[..., :SHIFT_COLS], mu_shift[l])
        n_ctx = STATE_COLS if last else N_IN
        n_ctx_shift = STATE_COLS if last else SHIFT_COLS
        zc = hc @ w_in[l][:, :n_ctx]
        zcs = _qshift_seq(zc[..., :n_ctx_shift], mu_shift[l][:n_ctx_shift])

        v, kk, dirs = _rwkv_state_terms(zs[..., :STATE_COLS], *rw_state)
        vc, kkc, dirs_c = _rwkv_state_terms(zcs[..., :STATE_COLS], *rw_state)
        r = _heads(zs[..., R_OFF:R_OFF + RWKV_WIDTH])
        rc = None if last else _heads(zcs[..., R_OFF:R_OFF + RWKV_WIDTH])
        s0 = jnp.zeros((b, RWKV_HEADS, HEAD_DIM, HEAD_DIM), jnp.float32)
        y_dirs, yc_dirs = [], []
        for d in range(2):
            dec_c, k_c, kka_c = dirs_c[d]
            s_ctx, yc_d = _wkv_scan(s0, dec_c, k_c, vc, kkc, kka_c, rc, d == 1)
            dec, k_d, kka = dirs[d]
            _, y_d = _wkv_scan(s_ctx, dec, k_d, v, kk, kka, r, d == 1)
            y_dirs.append(y_d)
            yc_dirs.append(yc_d)
        y = y_dirs[0] + y_dirs[1]
        k_bonus = 0.5 * (dirs[0][1] + dirs[1][1])
        mix = _merge_branches(z, r, zs[..., GLORA_OFF:SHIFT_COLS], y, k_bonus, v, *rw_read)
        x = x + g1 * _rmsnorm(mix, norm_post_mix[l])
        if not last:
            yc = yc_dirs[0] + yc_dirs[1]
            kc_bonus = 0.5 * (dirs_c[0][1] + dirs_c[1][1])
            mix_c = _merge_branches(zc, rc, zcs[..., GLORA_OFF:SHIFT_COLS], yc, kc_bonus, vc, *rw_read)
            xc = xc + cg1 * _rmsnorm(mix_c, norm_post_mix[l])

        moe_p = (w_router[l], b_router[l], w_exp_in[l], b_exp_in[l], w_exp_out[l], b_exp_out[l])
        h2 = (_rmsnorm(x, norm_pre_ffn[l]) * (1.0 + sc2) + sh2).reshape(-1, D_MODEL)
        if last:
            f = _moe(h2, *moe_p)
        else:
            h2c = (_rmsnorm(xc, norm_pre_ffn[l]) * (1.0 + csc2) + csh2).reshape(-1, D_MODEL)
            f_all = _moe(jnp.concatenate([h2, h2c], axis=0), *moe_p)
            f = f_all[:h2.shape[0]]
            xc = xc + cg2 * _rmsnorm(f_all[h2.shape[0]:].reshape(xc.shape), norm_post_ffn[l])
        x = x + g2 * _rmsnorm(f.reshape(x.shape), norm_post_ffn[l])
    return x
```

```python
import functools

import jax
import jax.numpy as jnp
from jax import lax
from jax.experimental import pallas as pl
from jax.experimental.pallas import tpu as pltpu

F32 = jnp.float32
BF16 = jnp.bfloat16

HEAD_DIM = 64
GRID_W = 64
DECAY_LORA = 64
ICLR_LORA = 64
GATE_LORA = 128
GN_EPS = 64e-5
NORM_EPS = 1e-12
CONV_K = 3
N_EXPERTS = 32
TOP_K = 4
SWIGLU_LIMIT = 7.0
SWIGLU_ALPHA = 1.702
MOE_BLOCK = 256
RMS_EPS = 1e-6

MXU_WIDTH = 256
HEADS_PER_GROUP = MXU_WIDTH // HEAD_DIM
CHUNK = 64
VMEM_LIMIT = 56 * 1024 * 1024


def _cparams(sem):
    return pltpu.CompilerParams(dimension_semantics=sem, vmem_limit_bytes=VMEM_LIMIT)


def _dot(a, b):
    return jnp.dot(a, b, preferred_element_type=F32)


def _dot_nt(a, b):
    return lax.dot_general(a, b, (((1,), (1,)), ((), ())), preferred_element_type=F32)


def _dot_tn(a, b):
    return lax.dot_general(a, b, (((0,), (0,)), ((), ())), preferred_element_type=F32)


def _split3(x):
    h = x.astype(BF16)
    r = x - h.astype(F32)
    m = r.astype(BF16)
    l = (r - m.astype(F32)).astype(BF16)
    return h, m, l


def _block_diag_mask(n, blk):
    r = lax.broadcasted_iota(jnp.int32, (n, n), 0) // blk
    c = lax.broadcasted_iota(jnp.int32, (n, n), 1) // blk
    return r == c


def _head_sum(x, ones_bd):
    h, m, l = _split3(x)
    return _dot(h, ones_bd) + _dot(m, ones_bd) + _dot(l, ones_bd)


def _rms(x, gain):
    return x * lax.rsqrt(jnp.mean(x * x, axis=-1, keepdims=True) + RMS_EPS) * gain


def _sigmoid(x):
    return 1.0 / (1.0 + jnp.exp(-x))


def _softplus(x):
    return jnp.maximum(x, 0.0) + jnp.log(1.0 + jnp.exp(-jnp.abs(x)))


def _ada_kernel(c_ref, w_ref, b_ref, o_ref):
    c = c_ref[...]
    s = c * _sigmoid(c)
    o_ref[...] = _dot(s, w_ref[...]) + b_ref[...]


def _ada(c_rows, w, b):
    rows, d = c_rows.shape
    n = w.shape[1]
    tn = 1024
    return pl.pallas_call(
        _ada_kernel,
        grid=(n // tn,),
        in_specs=[pl.BlockSpec((rows, d), lambda j: (0, 0)),
                  pl.BlockSpec((d, tn), lambda j: (0, j)),
                  pl.BlockSpec((1, tn), lambda j: (0, j))],
        out_specs=pl.BlockSpec((rows, tn), lambda j: (0, j)),
        out_shape=jax.ShapeDtypeStruct((rows, n), F32),
        compiler_params=_cparams(("arbitrary",)),
        name="ada_mod",
    )(c_rows, w, b.reshape(1, n))


def _inproj_kernel(x_ref, g_ref, sc_ref, sh_ref, *refs, n_plain, with_conv):
    n_w = n_plain + (3 if with_conv else 0)
    w_refs, o_refs = refs[:n_w], refs[n_w:]
    x = x_ref[0]
    h = _rms(x, g_ref[...]) * (1.0 + sc_ref[0]) + sh_ref[0]
    hb = h.astype(BF16)
    col = 512
    for w_ref, o_ref in zip(w_refs[:n_plain], o_refs[:n_plain]):
        n = w_ref.shape[1]
        for j in range(0, n, col):
            e = min(j + col, n)
            o_ref[0, :, j:e] = _dot(hb, w_ref[:, j:e]).astype(o_ref.dtype)
    if with_conv:
        wb_ref, wc_ref, wx_ref = w_refs[n_plain:]
        ob_ref, op_ref = o_refs[n_plain:]
        n = wb_ref.shape[1]
        for j in range(0, n, col):
            e = min(j + col, n)
            ob_ref[0, :, j:e] = _dot(hb, wb_ref[:, j:e]).astype(ob_ref.dtype)
            op_ref[0, :, j:e] = (_dot(hb, wc_ref[:, j:e]) * _dot(hb, wx_ref[:, j:e])).astype(op_ref.dtype)


def _inproj(x, gain, sc, sh, plain_ws, conv_ws, tm):
    b, t, d = x.shape
    with_conv = conv_ws is not None
    ws = list(plain_ws) + (list(conv_ws) if with_conv else [])
    out_w = [w.shape[1] for w in plain_ws] + ([conv_ws[0].shape[1]] * 2 if with_conv else [])
    row = lambda bi, i: (bi, i, 0)
    const = lambda bi, i: (0, 0)
    in_specs = [pl.BlockSpec((1, tm, d), row),
                pl.BlockSpec((1, d), const),
                pl.BlockSpec((1, 1, d), lambda bi, i: (bi, 0, 0)),
                pl.BlockSpec((1, 1, d), lambda bi, i: (bi, 0, 0))]
    in_specs += [pl.BlockSpec(w.shape, const, pipeline_mode=pl.Buffered(1)) for w in ws]
    return pl.pallas_call(
        functools.partial(_inproj_kernel, n_plain=len(plain_ws), with_conv=with_conv),
        grid=(b, t // tm),
        in_specs=in_specs,
        out_specs=[pl.BlockSpec((1, tm, n), row) for n in out_w],
        out_shape=[jax.ShapeDtypeStruct((b, t, n), BF16) for n in out_w],
        compiler_params=_cparams(("arbitrary", "arbitrary")),
        name="in_proj",
    )(x, gain.reshape(1, d), sc, sh, *ws)


def _shift_mix(cur, prev, nxt, mu, grid_mode, is_first, is_last):
    tm, w = cur.shape
    row = lax.broadcasted_iota(jnp.int32, (tm, w), 0)
    grp = lax.broadcasted_iota(jnp.int32, (tm, w), 1) % 4
    back = pltpu.roll(cur, 1, axis=0)
    fwd = pltpu.roll(cur, tm - 1, axis=0)
    if grid_mode:
        colpos = row % GRID_W
        left = jnp.where(colpos == 0, 0.0, back)
        right = jnp.where(colpos == GRID_W - 1, 0.0, fwd)
        prev = jnp.where(is_first, 0.0, prev)
        nxt = jnp.where(is_last, 0.0, nxt)
        up = jnp.concatenate([prev, cur[:tm - GRID_W]], axis=0)
        down = jnp.concatenate([cur[GRID_W:], nxt], axis=0)
        shifted = jnp.where(grp == 0, left, jnp.where(grp == 1, right, jnp.where(grp == 2, up, down)))
    else:
        before = jnp.where(row == 0, 0.0, back)
        after = jnp.where(row == tm - 1, 0.0, fwd)
        shifted = jnp.where(grp % 2 == 0, before, after)
    return cur + mu * (shifted - cur)


def _state_kernel(*refs, grid_mode, with_read, width):
    if grid_mode:
        (zm_ref, zmp_ref, zmn_ref, zl_ref, zlp_ref, zln_ref), refs = refs[:6], refs[6:]
    else:
        (zm_ref, zl_ref), refs = refs[:2], refs[2:]
        zmp_ref = zmn_ref = zlp_ref = zln_ref = None
    (mum_ref, mul_ref, w0_ref, wdec_ref, a0_ref, wic_ref, kk_ref, ka_ref), refs = refs[:8], refs[8:]
    if with_read:
        (rk_ref, wg_ref), refs = refs[:2], refs[2:]
    (kap_ref, v_ref, ld0_ref, ld1_ref, kd0_ref, kd1_ref, be0_ref, be1_ref), refs = refs[:8], refs[8:]
    if with_read:
        r_ref, bonus_ref, g_ref = refs

    i = pl.program_id(1)
    is_first = i == 0
    is_last = i == pl.num_programs(1) - 1
    ones_bd = _block_diag_mask(MXU_WIDTH, HEAD_DIM).astype(BF16)

    def shifted(z_ref, zp_ref, zn_ref, mu_ref, lo, hi):
        cur = z_ref[0, :, lo:hi].astype(F32)
        prev = zp_ref[0, :, lo:hi].astype(F32) if grid_mode else None
        nxt = zn_ref[0, :, lo:hi].astype(F32) if grid_mode else None
        return _shift_mix(cur, prev, nxt, mu_ref[:, lo:hi], grid_mode, is_first, is_last)

    lora = shifted(zl_ref, zlp_ref, zln_ref, mul_ref, 0, zl_ref.shape[2])
    lw = jnp.tanh(lora[:, :2 * DECAY_LORA]).astype(BF16)
    la = lora[:, 2 * DECAY_LORA:2 * DECAY_LORA + 2 * ICLR_LORA].astype(BF16)
    if with_read:
        lg = _sigmoid(lora[:, 2 * DECAY_LORA + 2 * ICLR_LORA:]).astype(BF16)

    for lo in range(0, width, MXU_WIDTH):
        hi = lo + MXU_WIDTH
        k = shifted(zm_ref, zmp_ref, zmn_ref, mum_ref, lo, hi)
        v = shifted(zm_ref, zmp_ref, zmn_ref, mum_ref, width + lo, width + hi)
        kk = k * kk_ref[:, lo:hi]
        n2 = _head_sum(kk * kk, ones_bd)
        kk = kk / jnp.maximum(jnp.sqrt(n2), NORM_EPS)
        kap_ref[0, :, lo:hi] = kk.astype(kap_ref.dtype)
        v_ref[0, :, lo:hi] = v.astype(v_ref.dtype)
        a_sum = None
        for d, (ld_ref, kd_ref, be_ref) in enumerate(((ld0_ref, kd0_ref, be0_ref), (ld1_ref, kd1_ref, be1_ref))):
            lo_d, hi_d = d * width + lo, d * width + hi
            pre_w = w0_ref[:, lo_d:hi_d] + _dot(lw, wdec_ref[:, lo_d:hi_d])
            w_log = -_softplus(-pre_w) - 0.5
            ld_ref[0, :, lo:hi] = -jnp.exp(w_log)
            a = _sigmoid(a0_ref[:, lo_d:hi_d] + _dot(la, wic_ref[:, lo_d:hi_d]))
            kd_ref[0, :, lo:hi] = (k * (1.0 + (a - 1.0) * ka_ref[:, lo:hi])).astype(kd_ref.dtype)
            be_ref[0, :, lo:hi] = (kk * a).astype(be_ref.dtype)
            a_sum = a if a_sum is None else a_sum + a
        if with_read:
            r = shifted(zm_ref, zmp_ref, zmn_ref, mum_ref, 2 * width + lo, 2 * width + hi)
            r_ref[0, :, lo:hi] = r.astype(r_ref.dtype)
            k_bonus = k * (1.0 + (0.5 * a_sum - 1.0) * ka_ref[:, lo:hi])
            s = _head_sum(r * k_bonus * rk_ref[:, lo:hi], ones_bd)
            bonus_ref[0, :, lo:hi] = (s * v).astype(bonus_ref.dtype)
            g_ref[0, :, lo:hi] = _dot(lg, wg_ref[:, lo:hi]).astype(g_ref.dtype)


def _state_terms(zm, zl, p, grid_mode, with_read, tm):
    b, t, _ = zm.shape
    width = p["k_k"].shape[1]
    nt = t // tm
    row = lambda bi, i: (bi, i, 0)
    const = lambda bi, i: (0, 0)
    ins, in_specs = [], []
    if grid_mode:
        hb = tm // GRID_W
        last = t // GRID_W - 1
        prev = lambda bi, i: (bi, jnp.maximum(i * hb - 1, 0), 0)
        nxt = lambda bi, i: (bi, jnp.minimum((i + 1) * hb, last), 0)
        for z in (zm, zl):
            ins += [z, z, z]
            in_specs += [pl.BlockSpec((1, tm, z.shape[2]), row),
                         pl.BlockSpec((1, GRID_W, z.shape[2]), prev),
                         pl.BlockSpec((1, GRID_W, z.shape[2]), nxt)]
    else:
        for z in (zm, zl):
            ins.append(z)
            in_specs.append(pl.BlockSpec((1, tm, z.shape[2]), row))
    names = ["mu_main", "mu_lora", "w0", "w_dec", "a0", "w_iclr", "k_k", "k_a"]
    if with_read:
        names += ["r_k", "w_gate"]
    for n in names:
        ins.append(p[n])
        in_specs.append(pl.BlockSpec(p[n].shape, const))
    out_dt = [BF16, BF16, F32, F32, BF16, BF16, BF16, BF16] + ([BF16] * 3 if with_read else [])
    return pl.pallas_call(
        functools.partial(_state_kernel, grid_mode=grid_mode, with_read=with_read, width=width),
        grid=(b, nt),
        in_specs=in_specs,
        out_specs=[pl.BlockSpec((1, tm, width), row) for _ in out_dt],
        out_shape=[jax.ShapeDtypeStruct((b, t, width), dt) for dt in out_dt],
        compiler_params=_cparams(("arbitrary", "arbitrary")),
        name="state_terms_grid" if grid_mode else "state_terms_seq",
    )(*ins)


def _pack_bd(y, bd_mask):
    reps = MXU_WIDTH // y.shape[0]
    return jnp.where(bd_mask, jnp.concatenate([y] * reps, axis=0), jnp.zeros((), y.dtype))


def _scan_group(reverse, s_bd, ld, kap, kd, be, v, r, consts):
    bd_mask, tri_incl, strict4, incl4, eye4 = consts[reverse]
    c = ld.shape[0]
    h3 = _split3(ld)
    cl = sum(_dot(tri_incl, t) for t in h3)
    ones = jnp.ones((c, 128), BF16)
    cl_end = sum(_dot_tn(t, ones) for t in h3)
    w_col = jnp.exp(jnp.concatenate([cl_end, cl_end], axis=1))
    e_neg = jnp.exp(-cl)
    kq = (kap * jnp.exp(cl - ld)).astype(BF16)
    kt = (kd * e_neg).astype(BF16)
    bt = (be * e_neg).astype(BF16)
    vb = v.astype(BF16)
    lhs = kq if r is None else jnp.concatenate([kq, (r * jnp.exp(cl)).astype(BF16)], axis=0)

    s_b = s_bd.astype(BF16)
    st = _dot(lhs, s_b)
    a_b = _dot_nt(lhs, _pack_bd(bt, bd_mask))
    a_k = _dot_nt(lhs, _pack_bd(kt, bd_mask))
    a_kb = jnp.where(strict4, a_b[:c], 0.0)
    a_kk = jnp.where(strict4, a_k[:c], 0.0)

    t_inv = eye4 - a_kb
    pw = a_kb
    n = 2
    while n < c:
        pw = _dot(pw.astype(BF16), _pack_bd(pw.astype(BF16), bd_mask))
        t_inv = t_inv + _dot(t_inv.astype(BF16), _pack_bd(pw.astype(BF16), bd_mask))
        n *= 2

    v_bd = _pack_bd(vb, bd_mask)
    if r is None:
        rhs = st + _dot(a_kk.astype(BF16), v_bd)
    else:
        a_rk = jnp.where(incl4, a_k[c:], 0.0)
        av = _dot(jnp.concatenate([a_kk, a_rk], axis=0).astype(BF16), v_bd)
        rhs = st[:c] + av[:c]
    u = _dot(t_inv.astype(BF16), _pack_bd(rhs.astype(BF16), bd_mask))
    ub = u.astype(BF16)
    y = None
    if r is not None:
        a_rb = jnp.where(incl4, a_b[c:], 0.0)
        y = st[c:] + av[c:] - _dot(a_rb.astype(BF16), _pack_bd(ub, bd_mask))
    delta = _dot_tn(jnp.concatenate([kt, -bt], axis=0), jnp.concatenate([vb, ub], axis=0))
    s_new = w_col * (s_bd + jnp.where(bd_mask, delta, 0.0))
    return s_new, y


def _scan_consts(c):
    bd_mask = _block_diag_mask(MXU_WIDTH, HEAD_DIM)
    ti = lax.broadcasted_iota(jnp.int32, (c, c), 0)
    tj = lax.broadcasted_iota(jnp.int32, (c, c), 1)
    t4 = lax.broadcasted_iota(jnp.int32, (c, MXU_WIDTH), 0)
    i4 = lax.broadcasted_iota(jnp.int32, (c, MXU_WIDTH), 1) % c
    eye4 = (t4 == i4).astype(F32)
    fwd = (bd_mask, (tj <= ti).astype(BF16), i4 < t4, i4 <= t4, eye4)
    bwd = (bd_mask, (tj >= ti).astype(BF16), i4 > t4, i4 >= t4, eye4)
    return (fwd, bwd)


def _scan_kernel(*refs, with_y, width):
    n_in = 12 if with_y else 10
    in_refs, refs = refs[:n_in], refs[n_in:]
    s0_ref, refs = refs[0], refs[1:]
    if with_y:
        y_refs, refs = refs[:2], refs[2:]
    sfin_ref, s_scr = refs
    c_idx = pl.program_id(1)
    n_groups = width // MXU_WIDTH

    @pl.when(c_idx == 0)
    def _():
        s_scr[...] = s0_ref[0]

    consts = _scan_consts(CHUNK)
    per_dir = 6 if with_y else 5
    for d in range(2):
        d_refs = in_refs[d * per_dir:(d + 1) * per_dir]
        for g in range(n_groups):
            lo, hi = g * MXU_WIDTH, (g + 1) * MXU_WIDTH
            arrs = [ref[0, :, lo:hi].astype(F32) for ref in d_refs]
            r = arrs[5] if with_y else None
            s_new, y = _scan_group(d, s_scr[d, g], *arrs[:5], r, consts)
            s_scr[d, g] = s_new
            if with_y:
                y_refs[d][0, :, lo:hi] = y

    @pl.when(c_idx == pl.num_programs(1) - 1)
    def _():
        sfin_ref[0] = s_scr[...]


def _wkv_scan(dirs, kap, v, r, s0):
    b, t, width = kap.shape
    nc = t // CHUNK
    with_y = r is not None
    fwd = lambda bi, ci: (bi, ci, 0)
    bwd = lambda bi, ci: (bi, nc - 1 - ci, 0)
    ins, in_specs = [], []
    for d, imap in enumerate((fwd, bwd)):
        ld, kd, be = dirs[d]
        for arr in (ld, kap, kd, be, v) + ((r,) if with_y else ()):
            ins.append(arr)
            in_specs.append(pl.BlockSpec((1, CHUNK, width), imap))
    n_groups = width // MXU_WIDTH
    s_shape = (2, n_groups, MXU_WIDTH, MXU_WIDTH)
    s_spec = pl.BlockSpec((1,) + s_shape, lambda bi, ci: (bi, 0, 0, 0, 0))
    ins.append(s0)
    in_specs.append(s_spec)
    out_specs, out_shape = [], []
    if with_y:
        out_specs += [pl.BlockSpec((1, CHUNK, width), fwd), pl.BlockSpec((1, CHUNK, width), bwd)]
        out_shape += [jax.ShapeDtypeStruct((b, t, width), F32)] * 2
    out_specs.append(s_spec)
    out_shape.append(jax.ShapeDtypeStruct((b,) + s_shape, F32))
    return pl.pallas_call(
        functools.partial(_scan_kernel, with_y=with_y, width=width),
        grid=(b, nc),
        in_specs=in_specs,
        out_specs=out_specs,
        out_shape=out_shape,
        scratch_shapes=[pltpu.VMEM(s_shape, F32)],
        compiler_params=_cparams(("arbitrary", "arbitrary")),
        name="wkv_scan" if with_y else "wkv_scan_ctx",
    )(*ins)


def _merge_kernel(yf_ref, yb_ref, bonus_ref, g_ref, zb_ref, p_ref, pp_ref, pn_ref, zg_ref, x_ref,
                  lnw_ref, lnb_ref, cw_ref, wout_ref, npost_ref, g1_ref, npre_ref, sc2_ref, sh2_ref,
                  wr_ref, br_ref,
                  x2_ref, h2_ref, gate_ref, idx_ref, rank_ref, cnt_ref, run_scr, *, width):
    first = jnp.logical_and(pl.program_id(0) == 0, pl.program_id(1) == 0)
    i = pl.program_id(1)
    tm = x_ref.shape[1]

    @pl.when(first)
    def _():
        run_scr[...] = jnp.zeros_like(run_scr)

    ones_bd = _block_diag_mask(MXU_WIDTH, HEAD_DIM).astype(BF16)
    row = lax.broadcasted_iota(jnp.int32, (tm, MXU_WIDTH), 0)
    halo = pp_ref.shape[1]
    parts = []
    for lo in range(0, width, MXU_WIDTH):
        hi = lo + MXU_WIDTH
        y = yf_ref[0, :, lo:hi] + yb_ref[0, :, lo:hi]
        mean = _head_sum(y, ones_bd) * (1.0 / HEAD_DIM)
        yc = y - mean
        var = _head_sum(yc * yc, ones_bd) * (1.0 / HEAD_DIM)
        yn = yc * lax.rsqrt(var + GN_EPS) * lnw_ref[:, lo:hi] + lnb_ref[:, lo:hi]
        y_rwkv = (yn + bonus_ref[0, :, lo:hi].astype(F32)) * g_ref[0, :, lo:hi].astype(F32)
        p = p_ref[0, :, lo:hi].astype(F32)
        p_prev = jnp.where(i == 0, 0.0, pp_ref[0, halo - 1:halo, lo:hi].astype(F32))
        p_next = jnp.where(i == pl.num_programs(1) - 1, 0.0, pn_ref[0, 0:1, lo:hi].astype(F32))
        before = jnp.where(row == 0, p_prev, pltpu.roll(p, 1, axis=0))
        after = jnp.where(row == tm - 1, p_next, pltpu.roll(p, tm - 1, axis=0))
        conv = cw_ref[0:1, lo:hi] * before + cw_ref[1:2, lo:hi] * p + cw_ref[2:3, lo:hi] * after
        y_conv = zb_ref[0, :, lo:hi].astype(F32) * conv
        ga = _sigmoid(zg_ref[0, :, lo:hi].astype(F32))
        gb = _sigmoid(zg_ref[0, :, width + lo:width + hi].astype(F32))
        parts.append((ga * y_rwkv + gb * y_conv).astype(BF16))
    merged = jnp.concatenate(parts, axis=1)
    mix = _dot(merged, wout_ref[...])
    x2 = x_ref[0] + g1_ref[0] * _rms(mix, npost_ref[...])
    x2_ref[0] = x2
    h2 = _rms(x2, npre_ref[...]) * (1.0 + sc2_ref[0]) + sh2_ref[0]
    h2_ref[0] = h2.astype(h2_ref.dtype)

    logits = _dot(h2, wr_ref[...]) + br_ref[...]
    lane = lax.broadcasted_iota(jnp.int32, logits.shape, 1)
    vals, idxs = [], []
    sel = jnp.zeros(logits.shape, F32)
    work = logits
    for _ in range(TOP_K):
        m = jnp.max(work, axis=-1, keepdims=True)
        idx = jnp.min(jnp.where(work == m, lane, logits.shape[1]), axis=-1, keepdims=True)
        hit = lane == idx
        vals.append(m)
        idxs.append(idx)
        sel = jnp.where(hit, 1.0, sel)
        work = jnp.where(hit, -jnp.inf, work)
    exps = [jnp.exp(vk - vals[0]) for vk in vals]
    denom = exps[0] + exps[1] + exps[2] + exps[3]
    r_i = lax.broadcasted_iota(jnp.int32, (tm, tm), 0)
    c_i = lax.broadcasted_iota(jnp.int32, (tm, tm), 1)
    before_cnt = _dot((c_i < r_i).astype(BF16), sel.astype(BF16)) + run_scr[...]
    gate_out = jnp.zeros(logits.shape, F32)
    idx_out = jnp.zeros(logits.shape, jnp.int32)
    rank_out = jnp.zeros(logits.shape, jnp.int32)
    for k in range(TOP_K):
        rank_k = jnp.sum(jnp.where(lane == idxs[k], before_cnt, 0.0), axis=-1, keepdims=True)
        gate_out = jnp.where(lane == k, exps[k] / denom, gate_out)
        idx_out = jnp.where(lane == k, idxs[k], idx_out)
        rank_out = jnp.where(lane == k, rank_k.astype(jnp.int32), rank_out)
    gate_ref[0] = gate_out
    idx_ref[0] = idx_out
    rank_ref[0] = rank_out
    run_scr[...] = run_scr[...] + jnp.sum(sel, axis=0, keepdims=True)
    cnt_ref[...] = run_scr[...]


def _merge_route(yf, yb, bonus, g, zb, p, zg, x, pr, tm):
    b, t, d = x.shape
    width = yf.shape[2]
    halo = 8
    hb = tm // halo
    last = t // halo - 1
    row = lambda bi, i: (bi, i, 0)
    const = lambda bi, i: (0, 0)
    per_b = lambda bi, i: (bi, 0, 0)
    prev = lambda bi, i: (bi, jnp.maximum(i * hb - 1, 0), 0)
    nxt = lambda bi, i: (bi, jnp.minimum((i + 1) * hb, last), 0)
    lanes = 128
    in_specs = [pl.BlockSpec((1, tm, width), row)] * 6
    in_specs += [pl.BlockSpec((1, halo, width), prev), pl.BlockSpec((1, halo, width), nxt),
                 pl.BlockSpec((1, tm, 2 * width), row), pl.BlockSpec((1, tm, d), row)]
    params = [pr["ln_x_w"], pr["ln_x_b"], pr["conv_w"], pr["w_out"], pr["norm_post_mix"]]
    in_specs += [pl.BlockSpec(a.shape, const) for a in params]
    in_specs.append(pl.BlockSpec((1, 1, d), per_b))
    in_specs.append(pl.BlockSpec(pr["norm_pre_ffn"].shape, const))
    in_specs += [pl.BlockSpec((1, 1, d), per_b)] * 2
    in_specs += [pl.BlockSpec(pr["w_router"].shape, const), pl.BlockSpec(pr["b_router"].shape, const)]
    out_specs = [pl.BlockSpec((1, tm, d), row), pl.BlockSpec((1, tm, d), row)]
    out_specs += [pl.BlockSpec((1, tm, lanes), row)] * 3
    out_specs.append(pl.BlockSpec((1, lanes), const))
    out_shape = [jax.ShapeDtypeStruct((b, t, d), F32), jax.ShapeDtypeStruct((b, t, d), BF16),
                 jax.ShapeDtypeStruct((b, t, lanes), F32), jax.ShapeDtypeStruct((b, t, lanes), jnp.int32),
                 jax.ShapeDtypeStruct((b, t, lanes), jnp.int32), jax.ShapeDtypeStruct((1, lanes), F32)]
    return pl.pallas_call(
        functools.partial(_merge_kernel, width=width),
        grid=(b, t // tm),
        in_specs=in_specs,
        out_specs=out_specs,
        out_shape=out_shape,
        scratch_shapes=[pltpu.VMEM((1, lanes), F32)],
        compiler_params=_cparams(("arbitrary", "arbitrary")),
        name="merge_route",
    )(yf, yb, bonus, g, zb, p, p, p, zg, x, *params, pr["g1"], pr["norm_pre_ffn"], pr["sc2"], pr["sh2"],
      pr["w_router"], pr["b_router"])


def _expert_kernel(be_ref, nvalid_ref, xb_ref, win_ref, bin_ref, wout_ref, bout_ref, o_ref, win_scr, wout_scr,
                   *, d_ff):
    i = pl.program_id(0)
    changed = jnp.logical_or(i == 0, be_ref[i] != be_ref[jnp.maximum(i - 1, 0)])

    @pl.when(changed)
    def _():
        win_scr[...] = win_ref[0].astype(BF16)
        wout_scr[...] = wout_ref[0].astype(BF16)

    @pl.when(i < nvalid_ref[0])
    def _():
        gu = _dot(xb_ref[...], win_scr[...]) + bin_ref[0]
        gate = jnp.minimum(gu[:, :d_ff], SWIGLU_LIMIT)
        up = jnp.clip(gu[:, d_ff:], -SWIGLU_LIMIT, SWIGLU_LIMIT)
        act = (up + 1.0) * gate * _sigmoid(SWIGLU_ALPHA * gate)
        o_ref[...] = _dot(act.astype(BF16), wout_scr[...]) + bout_ref[0]

    @pl.when(i >= nvalid_ref[0])
    def _():
        o_ref[...] = jnp.zeros_like(o_ref)


def _experts(xb, block_e, n_valid, w_in, b_in, w_out, b_out):
    n_rows, d = xb.shape
    n_e, _, d_ff2 = w_in.shape
    d_ff = d_ff2 // 2
    n_blocks = n_rows // MOE_BLOCK
    grid_spec = pltpu.PrefetchScalarGridSpec(
        num_scalar_prefetch=2,
        grid=(n_blocks,),
        in_specs=[pl.BlockSpec((MOE_BLOCK, d), lambda i, be, nv: (i, 0)),
                  pl.BlockSpec((1, d, d_ff2), lambda i, be, nv: (be[i], 0, 0)),
                  pl.BlockSpec((1, 1, d_ff2), lambda i, be, nv: (be[i], 0, 0)),
                  pl.BlockSpec((1, d_ff, d), lambda i, be, nv: (be[i], 0, 0)),
                  pl.BlockSpec((1, 1, d), lambda i, be, nv: (be[i], 0, 0))],
        out_specs=pl.BlockSpec((MOE_BLOCK, d), lambda i, be, nv: (i, 0)),
        scratch_shapes=[pltpu.VMEM((d, d_ff2), BF16), pltpu.VMEM((d_ff, d), BF16)],
    )
    return pl.pallas_call(
        functools.partial(_expert_kernel, d_ff=d_ff),
        grid_spec=grid_spec,
        out_shape=jax.ShapeDtypeStruct((n_rows, d), F32),
        compiler_params=_cparams(("arbitrary",)),
        name="moe_experts",
    )(block_e, n_valid, xb, w_in, b_in.reshape(n_e, 1, d_ff2), w_out, b_out.reshape(n_e, 1, d))


def _combine_kernel(y4_ref, gate_ref, x2_ref, npost_ref, g2_ref, o_ref):
    gates = gate_ref[0]
    d = o_ref.shape[2]
    f = None
    for k in range(TOP_K):
        term = gates[:, k:k + 1] * y4_ref[0, :, k * d:(k + 1) * d]
        f = term if f is None else f + term
    o_ref[0] = x2_ref[0] + g2_ref[0] * _rms(f, npost_ref[...])


def _combine(y4, gates, x2, npost, g2, tm):
    b, t, d = x2.shape
    row = lambda bi, i: (bi, i, 0)
    return pl.pallas_call(
        _combine_kernel,
        grid=(b, t // tm),
        in_specs=[pl.BlockSpec((1, tm, TOP_K * d), row),
                  pl.BlockSpec((1, tm, gates.shape[2]), row),
                  pl.BlockSpec((1, tm, d), row),
                  pl.BlockSpec((1, d), lambda bi, i: (0, 0)),
                  pl.BlockSpec((1, 1, d), lambda bi, i: (bi, 0, 0))],
        out_specs=pl.BlockSpec((1, tm, d), row),
        out_shape=jax.ShapeDtypeStruct((b, t, d), F32),
        compiler_params=_cparams(("arbitrary", "arbitrary")),
        name="moe_combine",
    )(y4, gates, x2, npost, g2)


def _lora_up(w_up):
    _, r, w = w_up.shape
    z = jnp.zeros((r, w), w_up.dtype)
    return jnp.concatenate([jnp.concatenate([w_up[0], z], axis=1),
                            jnp.concatenate([z, w_up[1]], axis=1)], axis=0).astype(BF16)


def _row_tile(t, pref):
    return pref if t % pref == 0 else t


def kernel(x, c, ctx, c_ctx, w_ada, b_ada, norm_pre_mix, norm_post_mix, norm_pre_ffn, norm_post_ffn,
           w_in, mu_shift, w0, w_decay_up, a0, w_iclr_up, k_k, k_a, r_k, w_gate_up, ln_x_w, ln_x_b,
           conv_w, w_out, w_router, b_router, w_exp_in, b_exp_in, w_exp_out, b_exp_out):
    b, t, d = x.shape
    t_ctx = ctx.shape[1]
    depth = w_ada.shape[0]
    width = k_k.shape[1]
    n_groups = width // MXU_WIDTH
    k_off, v_off = 0, width
    decay_off = 2 * width
    iclr_off = decay_off + 2 * DECAY_LORA
    r_off = iclr_off + 2 * ICLR_LORA
    glora_off = r_off + width
    conv_off = glora_off + GATE_LORA
    gate_off = conv_off + 3 * width

    xc = ctx
    for l in range(depth):
        last = l == depth - 1
        rows = jnp.concatenate([c, c_ctx[None, :], jnp.zeros((8 - b - 1, d), F32)], axis=0)
        mod = _ada(rows, w_ada[l], b_ada[l])
        sh1, sc1, g1, sh2, sc2, g2 = [mod[:b, None, j * d:(j + 1) * d] for j in range(6)]
        csh1, csc1, cg1, csh2, csc2, cg2 = [jnp.broadcast_to(mod[b:b + 1, None, j * d:(j + 1) * d], (b, 1, d))
                                            for j in range(6)]

        wl = w_in[l]
        cols = lambda lo, n: wl[:, lo:lo + n]
        w_main = jnp.concatenate([cols(k_off, width), cols(v_off, width), cols(r_off, width)], axis=1).astype(BF16)
        w_lora = jnp.concatenate([cols(decay_off, 2 * DECAY_LORA), cols(iclr_off, 2 * ICLR_LORA),
                                  cols(glora_off, GATE_LORA)], axis=1).astype(BF16)
        w_conv = tuple(cols(conv_off + j * width, width).astype(BF16) for j in range(3))
        w_gate = cols(gate_off, 2 * d).astype(BF16)
        mu = mu_shift[l]
        mu_main = jnp.concatenate([mu[k_off:k_off + width], mu[v_off:v_off + width], mu[r_off:r_off + width]])[None, :]
        mu_lora = jnp.concatenate([mu[decay_off:decay_off + 2 * DECAY_LORA], mu[iclr_off:iclr_off + 2 * ICLR_LORA],
                                   mu[glora_off:glora_off + GATE_LORA]])[None, :]
        sp = {
            "mu_main": mu_main, "mu_lora": mu_lora,
            "w0": w0[l].reshape(1, 2 * width), "w_dec": _lora_up(w_decay_up[l]),
            "a0": a0[l].reshape(1, 2 * width), "w_iclr": _lora_up(w_iclr_up[l]),
            "k_k": k_k[l][None, :], "k_a": k_a[l][None, :],
            "r_k": r_k[l].reshape(1, width), "w_gate": w_gate_up[l].astype(BF16),
        }
        if not last:
            raise NotImplementedError("context stream update for non-final layers")

        n_state_main = 2 * width
        zc_main, zc_lora = _inproj(xc, norm_pre_mix[l], csc1, csh1,
                                   [w_main[:, :n_state_main], w_lora[:, :2 * DECAY_LORA + 2 * ICLR_LORA]],
                                   None, _row_tile(t_ctx, 256))
        sp_ctx = dict(sp, mu_main=mu_main[:, :n_state_main], mu_lora=mu_lora[:, :2 * DECAY_LORA + 2 * ICLR_LORA])
        kap_c, v_c, ld0_c, ld1_c, kd0_c, kd1_c, be0_c, be1_c = _state_terms(
            zc_main, zc_lora, sp_ctx, grid_mode=False, with_read=False, tm=t_ctx)
        s0 = jnp.zeros((b, 2, n_groups, MXU_WIDTH, MXU_WIDTH), F32)
        (s_ctx,) = _wkv_scan([(ld0_c, kd0_c, be0_c), (ld1_c, kd1_c, be1_c)], kap_c, v_c, None, s0)

        zm, zl, zg, zb, p = _inproj(x, norm_pre_mix[l], sc1, sh1, [w_main, w_lora, w_gate], w_conv,
                                    _row_tile(t, 512))
        kap, v, ld0, ld1, kd0, kd1, be0, be1, r, bonus, g = _state_terms(
            zm, zl, sp, grid_mode=True, with_read=True, tm=_row_tile(t, 512))
        yf, yb, _ = _wkv_scan([(ld0, kd0, be0), (ld1, kd1, be1)], kap, v, r, s_ctx)

        lanes = 128
        pad_e = lanes - N_EXPERTS
        pr = {
            "ln_x_w": ln_x_w[l][None, :], "ln_x_b": ln_x_b[l][None, :], "conv_w": conv_w[l],
            "w_out": w_out[l].astype(BF16), "norm_post_mix": norm_post_mix[l][None, :], "g1": g1,
            "norm_pre_ffn": norm_pre_ffn[l][None, :], "sc2": sc2, "sh2": sh2,
            "w_router": jnp.pad(w_router[l], ((0, 0), (0, pad_e))),
            "b_router": jnp.pad(b_router[l], (0, pad_e), constant_values=-jnp.inf)[None, :],
        }
        x2, h2, gates, top_idx, rank, counts = _merge_route(yf, yb, bonus, g, zb, p, zg, x, pr, _row_tile(t, 256))

        n_tok = b * t
        nk = n_tok * TOP_K
        counts = counts[0, :N_EXPERTS].astype(jnp.int32)
        padded = (counts + MOE_BLOCK - 1) // MOE_BLOCK * MOE_BLOCK
        pad_ends = jnp.cumsum(padded)
        pad_starts = pad_ends - padded
        e_flat = top_idx[..., :TOP_K].reshape(nk)
        dest = pad_starts[e_flat] + rank[..., :TOP_K].reshape(nk)
        n_blocks = -(-(nk + N_EXPERTS * (MOE_BLOCK - 1)) // MOE_BLOCK)
        n_rows = n_blocks * MOE_BLOCK
        row_tok = jnp.zeros((n_rows,), jnp.int32).at[dest].set(jnp.arange(nk, dtype=jnp.int32) // TOP_K)
        block_e = jnp.minimum(
            jnp.searchsorted(pad_ends, jnp.arange(n_blocks, dtype=jnp.int32) * MOE_BLOCK, side='right'),
            N_EXPERTS - 1).astype(jnp.int32)
        n_valid = (pad_ends[-1:] // MOE_BLOCK).astype(jnp.int32)
        xb = h2.reshape(n_tok, d)[row_tok]
        yb_rows = _experts(xb, block_e, n_valid, w_exp_in[l], b_exp_in[l], w_exp_out[l], b_exp_out[l])
        y4 = yb_rows[dest].reshape(b, t, TOP_K * d)
        x = _combine(y4, gates, x2, norm_post_ffn[l][None, :], g2, _row_tile(t, 256))
    return x
```

```python
import functools

import jax
import jax.numpy as jnp
from jax import lax
from jax.experimental import pallas as pl
from jax.experimental.pallas import tpu as pltpu

F32 = jnp.float32
BF16 = jnp.bfloat16

HEAD_DIM = 64
GRID_W = 64
DECAY_LORA = 64
ICLR_LORA = 64
GATE_LORA = 128
GN_EPS = 64e-5
NORM_EPS = 1e-12
CONV_K = 3
N_EXPERTS = 32
TOP_K = 4
SWIGLU_LIMIT = 7.0
SWIGLU_ALPHA = 1.702
MOE_BLOCK = 256
RMS_EPS = 1e-6

MXU_WIDTH = 256
HEADS_PER_GROUP = MXU_WIDTH // HEAD_DIM
CHUNK = 64
VMEM_LIMIT = 56 * 1024 * 1024


def _cparams(sem):
    return pltpu.CompilerParams(dimension_semantics=sem, vmem_limit_bytes=VMEM_LIMIT)


def _dot(a, b):
    return jnp.dot(a, b, preferred_element_type=F32)


def _dot_nt(a, b):
    return lax.dot_general(a, b, (((1,), (1,)), ((), ())), preferred_element_type=F32)


def _dot_tn(a, b):
    return lax.dot_general(a, b, (((0,), (0,)), ((), ())), preferred_element_type=F32)


def _split3(x):
    h = x.astype(BF16)
    r = x - h.astype(F32)
    m = r.astype(BF16)
    l = (r - m.astype(F32)).astype(BF16)
    return h, m, l


def _block_diag_mask(n, blk):
    r = lax.broadcasted_iota(jnp.int32, (n, n), 0) // blk
    c = lax.broadcasted_iota(jnp.int32, (n, n), 1) // blk
    return r == c


def _head_sum(x, ones_bd):
    h, m, l = _split3(x)
    return _dot(h, ones_bd) + _dot(m, ones_bd) + _dot(l, ones_bd)


def _rms(x, gain):
    return x * lax.rsqrt(jnp.mean(x * x, axis=-1, keepdims=True) + RMS_EPS) * gain


def _sigmoid(x):
    return 1.0 / (1.0 + jnp.exp(-x))


def _softplus(x):
    return jnp.maximum(x, 0.0) + jnp.log(1.0 + jnp.exp(-jnp.abs(x)))


def _ada_kernel(c_ref, w_ref, b_ref, o_ref):
    c = c_ref[...]
    s = c * _sigmoid(c)
    o_ref[...] = _dot(s, w_ref[...]) + b_ref[...]


def _ada(c_rows, w, b):
    rows, d = c_rows.shape
    n = w.shape[1]
    tn = 1024
    return pl.pallas_call(
        _ada_kernel,
        grid=(n // tn,),
        in_specs=[pl.BlockSpec((rows, d), lambda j: (0, 0)),
                  pl.BlockSpec((d, tn), lambda j: (0, j)),
                  pl.BlockSpec((1, tn), lambda j: (0, j))],
        out_specs=pl.BlockSpec((rows, tn), lambda j: (0, j)),
        out_shape=jax.ShapeDtypeStruct((rows, n), F32),
        compiler_params=_cparams(("arbitrary",)),
        name="ada_mod",
    )(c_rows, w, b.reshape(1, n))


def _inproj_kernel(x_ref, g_ref, sc_ref, sh_ref, *refs, n_plain, with_conv):
    n_w = n_plain + (3 if with_conv else 0)
    w_refs, o_refs = refs[:n_w], refs[n_w:]
    x = x_ref[0]
    h = _rms(x, g_ref[...]) * (1.0 + sc_ref[0]) + sh_ref[0]
    hb = h.astype(BF16)
    col = 512
    for w_ref, o_ref in zip(w_refs[:n_plain], o_refs[:n_plain]):
        n = w_ref.shape[1]
        for j in range(0, n, col):
            e = min(j + col, n)
            o_ref[0, :, j:e] = _dot(hb, w_ref[:, j:e]).astype(o_ref.dtype)
    if with_conv:
        wb_ref, wc_ref, wx_ref = w_refs[n_plain:]
        ob_ref, op_ref = o_refs[n_plain:]
        n = wb_ref.shape[1]
        for j in range(0, n, col):
            e = min(j + col, n)
            ob_ref[0, :, j:e] = _dot(hb, wb_ref[:, j:e]).astype(ob_ref.dtype)
            op_ref[0, :, j:e] = (_dot(hb, wc_ref[:, j:e]) * _dot(hb, wx_ref[:, j:e])).astype(op_ref.dtype)


def _inproj(x, gain, sc, sh, plain_ws, conv_ws, tm):
    b, t, d = x.shape
    with_conv = conv_ws is not None
    ws = list(plain_ws) + (list(conv_ws) if with_conv else [])
    out_w = [w.shape[1] for w in plain_ws] + ([conv_ws[0].shape[1]] * 2 if with_conv else [])
    row = lambda bi, i: (bi, i, 0)
    const = lambda bi, i: (0, 0)
    in_specs = [pl.BlockSpec((1, tm, d), row),
                pl.BlockSpec((1, d), const),
                pl.BlockSpec((1, 1, d), lambda bi, i: (bi, 0, 0)),
                pl.BlockSpec((1, 1, d), lambda bi, i: (bi, 0, 0))]
    in_specs += [pl.BlockSpec(w.shape, const, pipeline_mode=pl.Buffered(1)) for w in ws]
    return pl.pallas_call(
        functools.partial(_inproj_kernel, n_plain=len(plain_ws), with_conv=with_conv),
        grid=(b, t // tm),
        in_specs=in_specs,
        out_specs=[pl.BlockSpec((1, tm, n), row) for n in out_w],
        out_shape=[jax.ShapeDtypeStruct((b, t, n), BF16) for n in out_w],
        compiler_params=_cparams(("arbitrary", "arbitrary")),
        name="in_proj",
    )(x, gain.reshape(1, d), sc, sh, *ws)


def _shift_mix(cur, prev, nxt, mu, grid_mode, is_first, is_last):
    tm, w = cur.shape
    row = lax.broadcasted_iota(jnp.int32, (tm, w), 0)
    grp = lax.broadcasted_iota(jnp.int32, (tm, w), 1) % 4
    back = pltpu.roll(cur, 1, axis=0)
    fwd = pltpu.roll(cur, tm - 1, axis=0)
    if grid_mode:
        colpos = row % GRID_W
        left = jnp.where(colpos == 0, 0.0, back)
        right = jnp.where(colpos == GRID_W - 1, 0.0, fwd)
        prev = jnp.where(is_first, 0.0, prev)
        nxt = jnp.where(is_last, 0.0, nxt)
        up = jnp.concatenate([prev, cur[:tm - GRID_W]], axis=0)
        down = jnp.concatenate([cur[GRID_W:], nxt], axis=0)
        shifted = jnp.where(grp == 0, left, jnp.where(grp == 1, right, jnp.where(grp == 2, up, down)))
    else:
        before = jnp.where(row == 0, 0.0, back)
        after = jnp.where(row == tm - 1, 0.0, fwd)
        shifted = jnp.where(grp % 2 == 0, before, after)
    return cur + mu * (shifted - cur)


def _state_kernel(*refs, grid_mode, with_read, width):
    if grid_mode:
        (zm_ref, zmp_ref, zmn_ref, zl_ref, zlp_ref, zln_ref), refs = refs[:6], refs[6:]
    else:
        (zm_ref, zl_ref), refs = refs[:2], refs[2:]
        zmp_ref = zmn_ref = zlp_ref = zln_ref = None
    (mum_ref, mul_ref, w0_ref, wdec_ref, a0_ref, wic_ref, kk_ref, ka_ref), refs = refs[:8], refs[8:]
    if with_read:
        (rk_ref, wg_ref), refs = refs[:2], refs[2:]
    (kap_ref, v_ref, ld0_ref, ld1_ref, kd0_ref, kd1_ref, be0_ref, be1_ref), refs = refs[:8], refs[8:]
    if with_read:
        r_ref, bonus_ref, g_ref = refs

    i = pl.program_id(1)
    is_first = i == 0
    is_last = i == pl.num_programs(1) - 1
    ones_bd = _block_diag_mask(MXU_WIDTH, HEAD_DIM).astype(BF16)

    def shifted(z_ref, zp_ref, zn_ref, mu_ref, lo, hi):
        cur = z_ref[0, :, lo:hi].astype(F32)
        prev = zp_ref[0, :, lo:hi].astype(F32) if grid_mode else None
        nxt = zn_ref[0, :, lo:hi].astype(F32) if grid_mode else None
        return _shift_mix(cur, prev, nxt, mu_ref[:, lo:hi], grid_mode, is_first, is_last)

    lora = shifted(zl_ref, zlp_ref, zln_ref, mul_ref, 0, zl_ref.shape[2])
    lw = jnp.tanh(lora[:, :2 * DECAY_LORA]).astype(BF16)
    la = lora[:, 2 * DECAY_LORA:2 * DECAY_LORA + 2 * ICLR_LORA].astype(BF16)
    if with_read:
        lg = _sigmoid(lora[:, 2 * DECAY_LORA + 2 * ICLR_LORA:]).astype(BF16)

    for lo in range(0, width, MXU_WIDTH):
        hi = lo + MXU_WIDTH
        k = shifted(zm_ref, zmp_ref, zmn_ref, mum_ref, lo, hi)
        v = shifted(zm_ref, zmp_ref, zmn_ref, mum_ref, width + lo, width + hi)
        kk = k * kk_ref[:, lo:hi]
        n2 = _head_sum(kk * kk, ones_bd)
        kk = kk / jnp.maximum(jnp.sqrt(n2), NORM_EPS)
        kap_ref[0, :, lo:hi] = kk.astype(kap_ref.dtype)
        v_ref[0, :, lo:hi] = v.astype(v_ref.dtype)
        a_sum = None
        for d, (ld_ref, kd_ref, be_ref) in enumerate(((ld0_ref, kd0_ref, be0_ref), (ld1_ref, kd1_ref, be1_ref))):
            lo_d, hi_d = d * width + lo, d * width + hi
            pre_w = w0_ref[:, lo_d:hi_d] + _dot(lw, wdec_ref[:, lo_d:hi_d])
            w_log = -_softplus(-pre_w) - 0.5
            ld_ref[0, :, lo:hi] = -jnp.exp(w_log)
            a = _sigmoid(a0_ref[:, lo_d:hi_d] + _dot(la, wic_ref[:, lo_d:hi_d]))
            kd_ref[0, :, lo:hi] = (k * (1.0 + (a - 1.0) * ka_ref[:, lo:hi])).astype(kd_ref.dtype)
            be_ref[0, :, lo:hi] = (kk * a).astype(be_ref.dtype)
            a_sum = a if a_sum is None else a_sum + a
        if with_read:
            r = shifted(zm_ref, zmp_ref, zmn_ref, mum_ref, 2 * width + lo, 2 * width + hi)
            r_ref[0, :, lo:hi] = r.astype(r_ref.dtype)
            k_bonus = k * (1.0 + (0.5 * a_sum - 1.0) * ka_ref[:, lo:hi])
            s = _head_sum(r * k_bonus * rk_ref[:, lo:hi], ones_bd)
            bonus_ref[0, :, lo:hi] = (s * v).astype(bonus_ref.dtype)
            g_ref[0, :, lo:hi] = _dot(lg, wg_ref[:, lo:hi]).astype(g_ref.dtype)


def _state_terms(zm, zl, p, grid_mode, with_read, tm):
    b, t, _ = zm.shape
    width = p["k_k"].shape[1]
    nt = t // tm
    row = lambda bi, i: (bi, i, 0)
    const = lambda bi, i: (0, 0)
    ins, in_specs = [], []
    if grid_mode:
        hb = tm // GRID_W
        last = t // GRID_W - 1
        prev = lambda bi, i: (bi, jnp.maximum(i * hb - 1, 0), 0)
        nxt = lambda bi, i: (bi, jnp.minimum((i + 1) * hb, last), 0)
        for z in (zm, zl):
            ins += [z, z, z]
            in_specs += [pl.BlockSpec((1, tm, z.shape[2]), row),
                         pl.BlockSpec((1, GRID_W, z.shape[2]), prev),
                         pl.BlockSpec((1, GRID_W, z.shape[2]), nxt)]
    else:
        for z in (zm, zl):
            ins.append(z)
            in_specs.append(pl.BlockSpec((1, tm, z.shape[2]), row))
    names = ["mu_main", "mu_lora", "w0", "w_dec", "a0", "w_iclr", "k_k", "k_a"]
    if with_read:
        names += ["r_k", "w_gate"]
    for n in names:
        ins.append(p[n])
        in_specs.append(pl.BlockSpec(p[n].shape, const))
    out_dt = [BF16, BF16, F32, F32, BF16, BF16, BF16, BF16] + ([BF16] * 3 if with_read else [])
    return pl.pallas_call(
        functools.partial(_state_kernel, grid_mode=grid_mode, with_read=with_read, width=width),
        grid=(b, nt),
        in_specs=in_specs,
        out_specs=[pl.BlockSpec((1, tm, width), row) for _ in out_dt],
        out_shape=[jax.ShapeDtypeStruct((b, t, width), dt) for dt in out_dt],
        compiler_params=_cparams(("arbitrary", "arbitrary")),
        name="state_terms_grid" if grid_mode else "state_terms_seq",
    )(*ins)


def _pack_bd(y, bd_mask):
    reps = MXU_WIDTH // y.shape[0]
    return jnp.where(bd_mask, jnp.concatenate([y] * reps, axis=0), jnp.zeros((), y.dtype))


def _scan_group(reverse, s_bd, ld, kap, kd, be, v, r, consts):
    bd_mask, tri_incl, strict4, incl4, eye4 = consts[reverse]
    c = ld.shape[0]
    h3 = _split3(ld)
    cl = sum(_dot(tri_incl, t) for t in h3)
    ones = jnp.ones((c, 128), BF16)
    cl_end = sum(_dot_tn(t, ones) for t in h3)
    w_col = jnp.exp(jnp.concatenate([cl_end, cl_end], axis=1))
    e_neg = jnp.exp(-cl)
    kq = (kap * jnp.exp(cl - ld)).astype(BF16)
    kt = (kd * e_neg).astype(BF16)
    bt = (be * e_neg).astype(BF16)
    vb = v.astype(BF16)
    lhs = kq if r is None else jnp.concatenate([kq, (r * jnp.exp(cl)).astype(BF16)], axis=0)

    s_b = s_bd.astype(BF16)
    st = _dot(lhs, s_b)
    a_b = _dot_nt(lhs, _pack_bd(bt, bd_mask))
    a_k = _dot_nt(lhs, _pack_bd(kt, bd_mask))
    a_kb = jnp.where(strict4, a_b[:c], 0.0)
    a_kk = jnp.where(strict4, a_k[:c], 0.0)

    t_inv = eye4 - a_kb
    pw = a_kb
    n = 2
    while n < c:
        pw = _dot(pw.astype(BF16), _pack_bd(pw.astype(BF16), bd_mask))
        t_inv = t_inv + _dot(t_inv.astype(BF16), _pack_bd(pw.astype(BF16), bd_mask))
        n *= 2

    v_bd = _pack_bd(vb, bd_mask)
    if r is None:
        rhs = st + _dot(a_kk.astype(BF16), v_bd)
    else:
        a_rk = jnp.where(incl4, a_k[c:], 0.0)
        av = _dot(jnp.concatenate([a_kk, a_rk], axis=0).astype(BF16), v_bd)
        rhs = st[:c] + av[:c]
    u = _dot(t_inv.astype(BF16), _pack_bd(rhs.astype(BF16), bd_mask))
    ub = u.astype(BF16)
    y = None
    if r is not None:
        a_rb = jnp.where(incl4, a_b[c:], 0.0)
        y = st[c:] + av[c:] - _dot(a_rb.astype(BF16), _pack_bd(ub, bd_mask))
    delta = _dot_tn(jnp.concatenate([kt, -bt], axis=0), jnp.concatenate([vb, ub], axis=0))
    s_new = w_col * (s_bd + jnp.where(bd_mask, delta, 0.0))
    return s_new, y


def _scan_consts(c):
    bd_mask = _block_diag_mask(MXU_WIDTH, HEAD_DIM)
    ti = lax.broadcasted_iota(jnp.int32, (c, c), 0)
    tj = lax.broadcasted_iota(jnp.int32, (c, c), 1)
    t4 = lax.broadcasted_iota(jnp.int32, (c, MXU_WIDTH), 0)
    i4 = lax.broadcasted_iota(jnp.int32, (c, MXU_WIDTH), 1) % c
    eye4 = (t4 == i4).astype(F32)
    fwd = (bd_mask, (tj <= ti).astype(BF16), i4 < t4, i4 <= t4, eye4)
    bwd = (bd_mask, (tj >= ti).astype(BF16), i4 > t4, i4 >= t4, eye4)
    return (fwd, bwd)


def _scan_kernel(*refs, with_y, width):
    n_in = 12 if with_y else 10
    in_refs, refs = refs[:n_in], refs[n_in:]
    s0_ref, refs = refs[0], refs[1:]
    if with_y:
        y_refs, refs = refs[:2], refs[2:]
    sfin_ref, s_scr = refs
    c_idx = pl.program_id(1)
    n_groups = width // MXU_WIDTH

    @pl.when(c_idx == 0)
    def _():
        s_scr[...] = s0_ref[0]

    consts = _scan_consts(CHUNK)
    per_dir = 6 if with_y else 5
    for d in range(2):
        d_refs = in_refs[d * per_dir:(d + 1) * per_dir]
        for g in range(n_groups):
            lo, hi = g * MXU_WIDTH, (g + 1) * MXU_WIDTH
            arrs = [ref[0, :, lo:hi].astype(F32) for ref in d_refs]
            r = arrs[5] if with_y else None
            s_new, y = _scan_group(d, s_scr[d, g], *arrs[:5], r, consts)
            s_scr[d, g] = s_new
            if with_y:
                y_refs[d][0, :, lo:hi] = y

    @pl.when(c_idx == pl.num_programs(1) - 1)
    def _():
        sfin_ref[0] = s_scr[...]


def _wkv_scan(dirs, kap, v, r, s0):
    b, t, width = kap.shape
    nc = t // CHUNK
    with_y = r is not None
    fwd = lambda bi, ci: (bi, ci, 0)
    bwd = lambda bi, ci: (bi, nc - 1 - ci, 0)
    ins, in_specs = [], []
    for d, imap in enumerate((fwd, bwd)):
        ld, kd, be = dirs[d]
        for arr in (ld, kap, kd, be, v) + ((r,) if with_y else ()):
            ins.append(arr)
            in_specs.append(pl.BlockSpec((1, CHUNK, width), imap))
    n_groups = width // MXU_WIDTH
    s_shape = (2, n_groups, MXU_WIDTH, MXU_WIDTH)
    s_spec = pl.BlockSpec((1,) + s_shape, lambda bi, ci: (bi, 0, 0, 0, 0))
    ins.append(s0)
    in_specs.append(s_spec)
    out_specs, out_shape = [], []
    if with_y:
        out_specs += [pl.BlockSpec((1, CHUNK, width), fwd), pl.BlockSpec((1, CHUNK, width), bwd)]
        out_shape += [jax.ShapeDtypeStruct((b, t, width), F32)] * 2
    out_specs.append(s_spec)
    out_shape.append(jax.ShapeDtypeStruct((b,) + s_shape, F32))
    return pl.pallas_call(
        functools.partial(_scan_kernel, with_y=with_y, width=width),
        grid=(b, nc),
        in_specs=in_specs,
        out_specs=out_specs,
        out_shape=out_shape,
        scratch_shapes=[pltpu.VMEM(s_shape, F32)],
        compiler_params=_cparams(("arbitrary", "arbitrary")),
        name="wkv_scan" if with_y else "wkv_scan_ctx",
    )(*ins)


def _merge_kernel(yf_ref, yb_ref, bonus_ref, g_ref, zb_ref, p_ref, pp_ref, pn_ref, zg_ref, x_ref,
                  lnw_ref, lnb_ref, cw_ref, wout_ref, npost_ref, g1_ref, npre_ref, sc2_ref, sh2_ref,
                  wr_ref, br_ref,
                  x2_ref, h2_ref, gate_ref, idx_ref, rank_ref, cnt_ref, run_scr, *, width):
    first = jnp.logical_and(pl.program_id(0) == 0, pl.program_id(1) == 0)
    i = pl.program_id(1)
    tm = x_ref.shape[1]

    @pl.when(first)
    def _():
        run_scr[...] = jnp.zeros_like(run_scr)

    ones_bd = _block_diag_mask(MXU_WIDTH, HEAD_DIM).astype(BF16)
    row = lax.broadcasted_iota(jnp.int32, (tm, MXU_WIDTH), 0)
    halo = pp_ref.shape[1]
    parts = []
    for lo in range(0, width, MXU_WIDTH):
        hi = lo + MXU_WIDTH
        y = yf_ref[0, :, lo:hi] + yb_ref[0, :, lo:hi]
        mean = _head_sum(y, ones_bd) * (1.0 / HEAD_DIM)
        yc = y - mean
        var = _head_sum(yc * yc, ones_bd) * (1.0 / HEAD_DIM)
        yn = yc * lax.rsqrt(var + GN_EPS) * lnw_ref[:, lo:hi] + lnb_ref[:, lo:hi]
        y_rwkv = (yn + bonus_ref[0, :, lo:hi].astype(F32)) * g_ref[0, :, lo:hi].astype(F32)
        p = p_ref[0, :, lo:hi].astype(F32)
        p_prev = jnp.where(i == 0, 0.0, pp_ref[0, halo - 1:halo, lo:hi].astype(F32))
        p_next = jnp.where(i == pl.num_programs(1) - 1, 0.0, pn_ref[0, 0:1, lo:hi].astype(F32))
        before = jnp.where(row == 0, p_prev, pltpu.roll(p, 1, axis=0))
        after = jnp.where(row == tm - 1, p_next, pltpu.roll(p, tm - 1, axis=0))
        conv = cw_ref[0:1, lo:hi] * before + cw_ref[1:2, lo:hi] * p + cw_ref[2:3, lo:hi] * after
        y_conv = zb_ref[0, :, lo:hi].astype(F32) * conv
        ga = _sigmoid(zg_ref[0, :, lo:hi].astype(F32))
        gb = _sigmoid(zg_ref[0, :, width + lo:width + hi].astype(F32))
        parts.append((ga * y_rwkv + gb * y_conv).astype(BF16))
    merged = jnp.concatenate(parts, axis=1)
    mix = _dot(merged, wout_ref[...])
    x2 = x_ref[0] + g1_ref[0] * _rms(mix, npost_ref[...])
    x2_ref[0] = x2
    h2 = _rms(x2, npre_ref[...]) * (1.0 + sc2_ref[0]) + sh2_ref[0]
    h2_ref[0] = h2.astype(h2_ref.dtype)

    logits = _dot(h2, wr_ref[...]) + br_ref[...]
    lane = lax.broadcasted_iota(jnp.int32, logits.shape, 1)
    vals, idxs = [], []
    sel = jnp.zeros(logits.shape, F32)
    work = logits
    for _ in range(TOP_K):
        m = jnp.max(work, axis=-1, keepdims=True)
        idx = jnp.min(jnp.where(work == m, lane, logits.shape[1]), axis=-1, keepdims=True)
        hit = lane == idx
        vals.append(m)
        idxs.append(idx)
        sel = jnp.where(hit, 1.0, sel)
        work = jnp.where(hit, -jnp.inf, work)
    exps = [jnp.exp(vk - vals[0]) for vk in vals]
    denom = exps[0] + exps[1] + exps[2] + exps[3]
    r_i = lax.broadcasted_iota(jnp.int32, (tm, tm), 0)
    c_i = lax.broadcasted_iota(jnp.int32, (tm, tm), 1)
    before_cnt = _dot((c_i < r_i).astype(BF16), sel.astype(BF16)) + run_scr[...]
    gate_out = jnp.zeros(logits.shape, F32)
    idx_out = jnp.zeros(logits.shape, jnp.int32)
    rank_out = jnp.zeros(logits.shape, jnp.int32)
    for k in range(TOP_K):
        rank_k = jnp.sum(jnp.where(lane == idxs[k], before_cnt, 0.0), axis=-1, keepdims=True)
        gate_out = jnp.where(lane == k, exps[k] / denom, gate_out)
        idx_out = jnp.where(lane == k, idxs[k], idx_out)
        rank_out = jnp.where(lane == k, rank_k.astype(jnp.int32), rank_out)
    gate_ref[0] = gate_out
    idx_ref[0] = idx_out
    rank_ref[0] = rank_out
    run_scr[...] = run_scr[...] + jnp.sum(sel, axis=0, keepdims=True)
    cnt_ref[...] = run_scr[...]


def _merge_route(yf, yb, bonus, g, zb, p, zg, x, pr, tm):
    b, t, d = x.shape
    width = yf.shape[2]
    halo = 8
    hb = tm // halo
    last = t // halo - 1
    row = lambda bi, i: (bi, i, 0)
    const = lambda bi, i: (0, 0)
    per_b = lambda bi, i: (bi, 0, 0)
    prev = lambda bi, i: (bi, jnp.maximum(i * hb - 1, 0), 0)
    nxt = lambda bi, i: (bi, jnp.minimum((i + 1) * hb, last), 0)
    lanes = 128
    in_specs = [pl.BlockSpec((1, tm, width), row)] * 6
    in_specs += [pl.BlockSpec((1, halo, width), prev), pl.BlockSpec((1, halo, width), nxt),
                 pl.BlockSpec((1, tm, 2 * width), row), pl.BlockSpec((1, tm, d), row)]
    params = [pr["ln_x_w"], pr["ln_x_b"], pr["conv_w"], pr["w_out"], pr["norm_post_mix"]]
    in_specs += [pl.BlockSpec(a.shape, const) for a in params]
    in_specs.append(pl.BlockSpec((1, 1, d), per_b))
    in_specs.append(pl.BlockSpec(pr["norm_pre_ffn"].shape, const))
    in_specs += [pl.BlockSpec((1, 1, d), per_b)] * 2
    in_specs += [pl.BlockSpec(pr["w_router"].shape, const), pl.BlockSpec(pr["b_router"].shape, const)]
    out_specs = [pl.BlockSpec((1, tm, d), row), pl.BlockSpec((1, tm, d), row)]
    out_specs += [pl.BlockSpec((1, tm, lanes), row)] * 3
    out_specs.append(pl.BlockSpec((1, lanes), const))
    out_shape = [jax.ShapeDtypeStruct((b, t, d), F32), jax.ShapeDtypeStruct((b, t, d), F32),
                 jax.ShapeDtypeStruct((b, t, lanes), F32), jax.ShapeDtypeStruct((b, t, lanes), jnp.int32),
                 jax.ShapeDtypeStruct((b, t, lanes), jnp.int32), jax.ShapeDtypeStruct((1, lanes), F32)]
    return pl.pallas_call(
        functools.partial(_merge_kernel, width=width),
        grid=(b, t // tm),
        in_specs=in_specs,
        out_specs=out_specs,
        out_shape=out_shape,
        scratch_shapes=[pltpu.VMEM((1, lanes), F32)],
        compiler_params=_cparams(("arbitrary", "arbitrary")),
        name="merge_route",
    )(yf, yb, bonus, g, zb, p, p, p, zg, x, *params, pr["g1"], pr["norm_pre_ffn"], pr["sc2"], pr["sh2"],
      pr["w_router"], pr["b_router"])


def _row_copy(src_ref, src_row, dst_ref, dst_row, sem):
    return pltpu.make_async_copy(src_ref.at[pl.ds(src_row, 1)], dst_ref.at[pl.ds(dst_row, 1)], sem)


def _dispatch_kernel(dest_ref, nrows_ref, h2_ref, xb_ref, zrow, sem, zsem, *, tm, n_pad):
    base = pl.program_id(0) * (tm * TOP_K)

    @pl.when(pl.program_id(0) == 0)
    def _():
        zrow[...] = jnp.zeros_like(zrow)

        def per_block(j, carry):
            def fill(r, c2):
                _row_copy(zrow, 0, xb_ref, j * MOE_BLOCK + r, zsem).start()
                return c2
            return lax.fori_loop(nrows_ref[j], MOE_BLOCK, fill, carry)

        lax.fori_loop(0, nrows_ref.shape[0], per_block, 0)
        pltpu.make_async_copy(xb_ref.at[pl.ds(0, n_pad)], xb_ref.at[pl.ds(0, n_pad)], zsem).wait()

    def issue(r, carry):
        for k in range(TOP_K):
            _row_copy(h2_ref, r, xb_ref, dest_ref[base + r * TOP_K + k], sem).start()
        return carry

    lax.fori_loop(0, tm, issue, 0, unroll=4)
    for _ in range(TOP_K):
        pltpu.make_async_copy(h2_ref, xb_ref.at[pl.ds(0, tm)], sem).wait()


def _dispatch(h2, dest, block_rows, tm):
    n_tok, d = h2.shape
    n_rows = block_rows.shape[0] * MOE_BLOCK
    grid_spec = pltpu.PrefetchScalarGridSpec(
        num_scalar_prefetch=2,
        grid=(n_tok // tm,),
        in_specs=[pl.BlockSpec((tm, d), lambda i, dest, nr: (i, 0))],
        out_specs=pl.BlockSpec(memory_space=pl.ANY),
        scratch_shapes=[pltpu.VMEM((8, d), h2.dtype), pltpu.SemaphoreType.DMA, pltpu.SemaphoreType.DMA],
    )
    return pl.pallas_call(
        functools.partial(_dispatch_kernel, tm=tm, n_pad=n_rows - dest.shape[0]),
        grid_spec=grid_spec,
        out_shape=jax.ShapeDtypeStruct((n_rows, d), h2.dtype),
        compiler_params=_cparams(("arbitrary",)),
        name="moe_dispatch",
    )(dest, block_rows, h2)


def _expert_kernel(be_ref, nrows_ref, xb_ref, win_ref, bin_ref, wout_ref, bout_ref, o_ref, win_scr, wout_scr,
                   *, d_ff):
    i = pl.program_id(0)
    changed = jnp.logical_or(i == 0, be_ref[i] != be_ref[jnp.maximum(i - 1, 0)])
    n_rows = nrows_ref[i]

    @pl.when(jnp.logical_and(changed, n_rows > 0))
    def _():
        win_scr[...] = win_ref[0].astype(BF16)
        wout_scr[...] = wout_ref[0].astype(BF16)

    @pl.when(n_rows > 0)
    def _():
        row = lax.broadcasted_iota(jnp.int32, xb_ref.shape, 0)
        xb = jnp.where(row < n_rows, xb_ref[...], 0.0).astype(BF16)
        gu = _dot(xb, win_scr[...]) + bin_ref[0]
        gate = jnp.minimum(gu[:, :d_ff], SWIGLU_LIMIT)
        up = jnp.clip(gu[:, d_ff:], -SWIGLU_LIMIT, SWIGLU_LIMIT)
        act = (up + 1.0) * gate * _sigmoid(SWIGLU_ALPHA * gate)
        o_ref[...] = _dot(act.astype(BF16), wout_scr[...]) + bout_ref[0]

    @pl.when(n_rows == 0)
    def _():
        o_ref[...] = jnp.zeros_like(o_ref)


def _experts(xb, block_e, block_rows, w_in, b_in, w_out, b_out):
    n_rows, d = xb.shape
    n_e, _, d_ff2 = w_in.shape
    d_ff = d_ff2 // 2
    n_blocks = n_rows // MOE_BLOCK
    grid_spec = pltpu.PrefetchScalarGridSpec(
        num_scalar_prefetch=2,
        grid=(n_blocks,),
        in_specs=[pl.BlockSpec((MOE_BLOCK, d), lambda i, be, nr: (i, 0)),
                  pl.BlockSpec((1, d, d_ff2), lambda i, be, nr: (be[i], 0, 0)),
                  pl.BlockSpec((1, 1, d_ff2), lambda i, be, nr: (be[i], 0, 0)),
                  pl.BlockSpec((1, d_ff, d), lambda i, be, nr: (be[i], 0, 0)),
                  pl.BlockSpec((1, 1, d), lambda i, be, nr: (be[i], 0, 0))],
        out_specs=pl.BlockSpec((MOE_BLOCK, d), lambda i, be, nr: (i, 0)),
        scratch_shapes=[pltpu.VMEM((d, d_ff2), BF16), pltpu.VMEM((d_ff, d), BF16)],
    )
    return pl.pallas_call(
        functools.partial(_expert_kernel, d_ff=d_ff),
        grid_spec=grid_spec,
        out_shape=jax.ShapeDtypeStruct((n_rows, d), F32),
        compiler_params=_cparams(("arbitrary",)),
        name="moe_experts",
    )(block_e, block_rows, xb, w_in, b_in.reshape(n_e, 1, d_ff2), w_out, b_out.reshape(n_e, 1, d))


def _combine_kernel(dest_ref, yb_ref, gate_ref, x2_ref, npost_ref, g2_ref, o_ref, ybuf, sem, *, tm):
    tile = pl.program_id(0) * pl.num_programs(1) + pl.program_id(1)
    base = tile * (tm * TOP_K)

    def issue(r, carry):
        for k in range(TOP_K):
            _row_copy(yb_ref, dest_ref[base + r * TOP_K + k], ybuf.at[k], r, sem).start()
        return carry

    lax.fori_loop(0, tm, issue, 0, unroll=4)
    for k in range(TOP_K):
        pltpu.make_async_copy(yb_ref.at[pl.ds(0, tm)], ybuf.at[k], sem).wait()
    gates = gate_ref[0]
    f = None
    for k in range(TOP_K):
        term = gates[:, k:k + 1] * ybuf[k]
        f = term if f is None else f + term
    o_ref[0] = x2_ref[0] + g2_ref[0] * _rms(f, npost_ref[...])


def _combine(yb_rows, dest, gates, x2, npost, g2, tm):
    b, t, d = x2.shape
    row = lambda bi, i, dest: (bi, i, 0)
    grid_spec = pltpu.PrefetchScalarGridSpec(
        num_scalar_prefetch=1,
        grid=(b, t // tm),
        in_specs=[pl.BlockSpec(memory_space=pl.ANY),
                  pl.BlockSpec((1, tm, gates.shape[2]), row),
                  pl.BlockSpec((1, tm, d), row),
                  pl.BlockSpec((1, d), lambda bi, i, dest: (0, 0)),
                  pl.BlockSpec((1, 1, d), lambda bi, i, dest: (bi, 0, 0))],
        out_specs=pl.BlockSpec((1, tm, d), row),
        scratch_shapes=[pltpu.VMEM((TOP_K, tm, d), F32), pltpu.SemaphoreType.DMA],
    )
    return pl.pallas_call(
        functools.partial(_combine_kernel, tm=tm),
        grid_spec=grid_spec,
        out_shape=jax.ShapeDtypeStruct((b, t, d), F32),
        compiler_params=_cparams(("arbitrary", "arbitrary")),
        name="moe_combine",
    )(dest, yb_rows, gates, x2, npost, g2)


def _lora_up(w_up):
    _, r, w = w_up.shape
    z = jnp.zeros((r, w), w_up.dtype)
    return jnp.concatenate([jnp.concatenate([w_up[0], z], axis=1),
                            jnp.concatenate([z, w_up[1]], axis=1)], axis=0).astype(BF16)


def _row_tile(t, pref):
    return pref if t % pref == 0 else t


def kernel(x, c, ctx, c_ctx, w_ada, b_ada, norm_pre_mix, norm_post_mix, norm_pre_ffn, norm_post_ffn,
           w_in, mu_shift, w0, w_decay_up, a0, w_iclr_up, k_k, k_a, r_k, w_gate_up, ln_x_w, ln_x_b,
           conv_w, w_out, w_router, b_router, w_exp_in, b_exp_in, w_exp_out, b_exp_out):
    b, t, d = x.shape
    t_ctx = ctx.shape[1]
    depth = w_ada.shape[0]
    width = k_k.shape[1]
    n_groups = width // MXU_WIDTH
    k_off, v_off = 0, width
    decay_off = 2 * width
    iclr_off = decay_off + 2 * DECAY_LORA
    r_off = iclr_off + 2 * ICLR_LORA
    glora_off = r_off + width
    conv_off = glora_off + GATE_LORA
    gate_off = conv_off + 3 * width

    xc = ctx
    for l in range(depth):
        last = l == depth - 1
        rows = jnp.concatenate([c, c_ctx[None, :], jnp.zeros((8 - b - 1, d), F32)], axis=0)
        mod = _ada(rows, w_ada[l], b_ada[l])
        sh1, sc1, g1, sh2, sc2, g2 = [mod[:b, None, j * d:(j + 1) * d] for j in range(6)]
        csh1, csc1, cg1, csh2, csc2, cg2 = [jnp.broadcast_to(mod[b:b + 1, None, j * d:(j + 1) * d], (b, 1, d))
                                            for j in range(6)]

        wl = w_in[l]
        cols = lambda lo, n: wl[:, lo:lo + n]
        w_main = jnp.concatenate([cols(k_off, width), cols(v_off, width), cols(r_off, width)], axis=1).astype(BF16)
        w_lora = jnp.concatenate([cols(decay_off, 2 * DECAY_LORA), cols(iclr_off, 2 * ICLR_LORA),
                                  cols(glora_off, GATE_LORA)], axis=1).astype(BF16)
        w_conv = tuple(cols(conv_off + j * width, width).astype(BF16) for j in range(3))
        w_gate = cols(gate_off, 2 * d).astype(BF16)
        mu = mu_shift[l]
        mu_main = jnp.concatenate([mu[k_off:k_off + width], mu[v_off:v_off + width], mu[r_off:r_off + width]])[None, :]
        mu_lora = jnp.concatenate([mu[decay_off:decay_off + 2 * DECAY_LORA], mu[iclr_off:iclr_off + 2 * ICLR_LORA],
                                   mu[glora_off:glora_off + GATE_LORA]])[None, :]
        sp = {
            "mu_main": mu_main, "mu_lora": mu_lora,
            "w0": w0[l].reshape(1, 2 * width), "w_dec": _lora_up(w_decay_up[l]),
            "a0": a0[l].reshape(1, 2 * width), "w_iclr": _lora_up(w_iclr_up[l]),
            "k_k": k_k[l][None, :], "k_a": k_a[l][None, :],
            "r_k": r_k[l].reshape(1, width), "w_gate": w_gate_up[l].astype(BF16),
        }
        if not last:
            raise NotImplementedError("context stream update for non-final layers")

        n_state_main = 2 * width
        zc_main, zc_lora = _inproj(xc, norm_pre_mix[l], csc1, csh1,
                                   [w_main[:, :n_state_main], w_lora[:, :2 * DECAY_LORA + 2 * ICLR_LORA]],
                                   None, _row_tile(t_ctx, 256))
        sp_ctx = dict(sp, mu_main=mu_main[:, :n_state_main], mu_lora=mu_lora[:, :2 * DECAY_LORA + 2 * ICLR_LORA])
        kap_c, v_c, ld0_c, ld1_c, kd0_c, kd1_c, be0_c, be1_c = _state_terms(
            zc_main, zc_lora, sp_ctx, grid_mode=False, with_read=False, tm=t_ctx)
        s0 = jnp.zeros((b, 2, n_groups, MXU_WIDTH, MXU_WIDTH), F32)
        (s_ctx,) = _wkv_scan([(ld0_c, kd0_c, be0_c), (ld1_c, kd1_c, be1_c)], kap_c, v_c, None, s0)

        zm, zl, zg, zb, p = _inproj(x, norm_pre_mix[l], sc1, sh1, [w_main, w_lora, w_gate], w_conv,
                                    _row_tile(t, 512))
        kap, v, ld0, ld1, kd0, kd1, be0, be1, r, bonus, g = _state_terms(
            zm, zl, sp, grid_mode=True, with_read=True, tm=_row_tile(t, 512))
        yf, yb, _ = _wkv_scan([(ld0, kd0, be0), (ld1, kd1, be1)], kap, v, r, s_ctx)

        lanes = 128
        pad_e = lanes - N_EXPERTS
        pr = {
            "ln_x_w": ln_x_w[l][None, :], "ln_x_b": ln_x_b[l][None, :], "conv_w": conv_w[l],
            "w_out": w_out[l].astype(BF16), "norm_post_mix": norm_post_mix[l][None, :], "g1": g1,
            "norm_pre_ffn": norm_pre_ffn[l][None, :], "sc2": sc2, "sh2": sh2,
            "w_router": jnp.pad(w_router[l], ((0, 0), (0, pad_e))),
            "b_router": jnp.pad(b_router[l], (0, pad_e), constant_values=-jnp.inf)[None, :],
        }
        x2, h2, gates, top_idx, rank, counts = _merge_route(yf, yb, bonus, g, zb, p, zg, x, pr, _row_tile(t, 256))

        n_tok = b * t
        nk = n_tok * TOP_K
        counts = counts[0, :N_EXPERTS].astype(jnp.int32)
        padded = (counts + MOE_BLOCK - 1) // MOE_BLOCK * MOE_BLOCK
        pad_ends = jnp.cumsum(padded)
        pad_starts = pad_ends - padded
        experts = jnp.arange(N_EXPERTS, dtype=jnp.int32)
        e_flat = top_idx[..., :TOP_K].reshape(nk)
        dest = jnp.sum(jnp.where(e_flat[:, None] == experts[None, :], pad_starts[None, :], 0), axis=1) \
            + rank[..., :TOP_K].reshape(nk)
        n_blocks = -(-(nk + N_EXPERTS * (MOE_BLOCK - 1)) // MOE_BLOCK)
        n_rows = n_blocks * MOE_BLOCK
        blk_start = jnp.arange(n_blocks, dtype=jnp.int32) * MOE_BLOCK
        block_e = jnp.minimum(jnp.sum((pad_ends[None, :] <= blk_start[:, None]).astype(jnp.int32), axis=1),
                              N_EXPERTS - 1)
        own = block_e[:, None] == experts[None, :]
        seg_end = jnp.sum(jnp.where(own, (pad_starts + counts)[None, :], 0), axis=1)
        block_rows = jnp.clip(seg_end - blk_start, 0, MOE_BLOCK).astype(jnp.int32)
        xb = _dispatch(h2.reshape(n_tok, d), dest, block_rows, _row_tile(n_tok, 256))
        yb_rows = _experts(xb, block_e, block_rows, w_exp_in[l], b_exp_in[l], w_exp_out[l], b_exp_out[l])
        x = _combine(yb_rows, dest, gates, x2, norm_post_ffn[l][None, :], g2, _row_tile(t, 256))
    return x
```

```python
import functools

import jax
import jax.numpy as jnp
from jax import lax
from jax.experimental import pallas as pl
from jax.experimental.pallas import tpu as pltpu

F32 = jnp.float32
BF16 = jnp.bfloat16

HEAD_DIM = 64
GRID_W = 64
DECAY_LORA = 64
ICLR_LORA = 64
GATE_LORA = 128
GN_EPS = 64e-5
NORM_EPS = 1e-12
CONV_K = 3
N_EXPERTS = 32
TOP_K = 4
SWIGLU_LIMIT = 7.0
SWIGLU_ALPHA = 1.702
MOE_BLOCK = 256
RMS_EPS = 1e-6

MXU_WIDTH = 256
HEADS_PER_GROUP = MXU_WIDTH // HEAD_DIM
CHUNK = 64
VMEM_LIMIT = 56 * 1024 * 1024


def _cparams(sem):
    return pltpu.CompilerParams(dimension_semantics=sem, vmem_limit_bytes=VMEM_LIMIT)


def _dot(a, b):
    return jnp.dot(a, b, preferred_element_type=F32)


def _dot_nt(a, b):
    return lax.dot_general(a, b, (((1,), (1,)), ((), ())), preferred_element_type=F32)


def _dot_tn(a, b):
    return lax.dot_general(a, b, (((0,), (0,)), ((), ())), preferred_element_type=F32)


def _split3(x):
    h = x.astype(BF16)
    r = x - h.astype(F32)
    m = r.astype(BF16)
    l = (r - m.astype(F32)).astype(BF16)
    return h, m, l


def _block_diag_mask(n, blk):
    r = lax.broadcasted_iota(jnp.int32, (n, n), 0) // blk
    c = lax.broadcasted_iota(jnp.int32, (n, n), 1) // blk
    return r == c


def _head_sum(x, ones_bd):
    h, m, l = _split3(x)
    return _dot(h, ones_bd) + _dot(m, ones_bd) + _dot(l, ones_bd)


def _rms(x, gain):
    return x * lax.rsqrt(jnp.mean(x * x, axis=-1, keepdims=True) + RMS_EPS) * gain


def _sigmoid(x):
    return 1.0 / (1.0 + jnp.exp(-x))


def _softplus(x):
    return jnp.maximum(x, 0.0) + jnp.log(1.0 + jnp.exp(-jnp.abs(x)))


def _ada_kernel(c_ref, w_ref, b_ref, o_ref):
    c = c_ref[...]
    s = c * _sigmoid(c)
    o_ref[...] = _dot(s, w_ref[...]) + b_ref[...]


def _ada(c_rows, w, b):
    rows, d = c_rows.shape
    n = w.shape[1]
    tn = 1024
    return pl.pallas_call(
        _ada_kernel,
        grid=(n // tn,),
        in_specs=[pl.BlockSpec((rows, d), lambda j: (0, 0)),
                  pl.BlockSpec((d, tn), lambda j: (0, j)),
                  pl.BlockSpec((1, tn), lambda j: (0, j))],
        out_specs=pl.BlockSpec((rows, tn), lambda j: (0, j)),
        out_shape=jax.ShapeDtypeStruct((rows, n), F32),
        compiler_params=_cparams(("arbitrary",)),
        name="ada_mod",
    )(c_rows, w, b.reshape(1, n))


def _inproj_kernel(x_ref, g_ref, sc_ref, sh_ref, *refs, n_plain, with_conv):
    n_w = n_plain + (3 if with_conv else 0)
    w_refs, o_refs = refs[:n_w], refs[n_w:]
    x = x_ref[0]
    h = _rms(x, g_ref[...]) * (1.0 + sc_ref[0]) + sh_ref[0]
    hb = h.astype(BF16)
    col = 512
    for w_ref, o_ref in zip(w_refs[:n_plain], o_refs[:n_plain]):
        n = w_ref.shape[1]
        for j in range(0, n, col):
            e = min(j + col, n)
            o_ref[0, :, j:e] = _dot(hb, w_ref[:, j:e]).astype(o_ref.dtype)
    if with_conv:
        wb_ref, wc_ref, wx_ref = w_refs[n_plain:]
        ob_ref, op_ref = o_refs[n_plain:]
        n = wb_ref.shape[1]
        for j in range(0, n, col):
            e = min(j + col, n)
            ob_ref[0, :, j:e] = _dot(hb, wb_ref[:, j:e]).astype(ob_ref.dtype)
            op_ref[0, :, j:e] = (_dot(hb, wc_ref[:, j:e]) * _dot(hb, wx_ref[:, j:e])).astype(op_ref.dtype)


def _inproj(x, gain, sc, sh, plain_ws, conv_ws, tm):
    b, t, d = x.shape
    with_conv = conv_ws is not None
    ws = list(plain_ws) + (list(conv_ws) if with_conv else [])
    out_w = [w.shape[1] for w in plain_ws] + ([conv_ws[0].shape[1]] * 2 if with_conv else [])
    row = lambda bi, i: (bi, i, 0)
    const = lambda bi, i: (0, 0)
    in_specs = [pl.BlockSpec((1, tm, d), row),
                pl.BlockSpec((1, d), const),
                pl.BlockSpec((1, 1, d), lambda bi, i: (bi, 0, 0)),
                pl.BlockSpec((1, 1, d), lambda bi, i: (bi, 0, 0))]
    in_specs += [pl.BlockSpec(w.shape, const, pipeline_mode=pl.Buffered(1)) for w in ws]
    return pl.pallas_call(
        functools.partial(_inproj_kernel, n_plain=len(plain_ws), with_conv=with_conv),
        grid=(b, t // tm),
        in_specs=in_specs,
        out_specs=[pl.BlockSpec((1, tm, n), row) for n in out_w],
        out_shape=[jax.ShapeDtypeStruct((b, t, n), BF16) for n in out_w],
        compiler_params=_cparams(("arbitrary", "arbitrary")),
        name="in_proj",
    )(x, gain.reshape(1, d), sc, sh, *ws)


def _shift_mix(cur, prev, nxt, mu, grid_mode, is_first, is_last):
    tm, w = cur.shape
    row = lax.broadcasted_iota(jnp.int32, (tm, w), 0)
    grp = lax.broadcasted_iota(jnp.int32, (tm, w), 1) % 4
    back = pltpu.roll(cur, 1, axis=0)
    fwd = pltpu.roll(cur, tm - 1, axis=0)
    if grid_mode:
        colpos = row % GRID_W
        left = jnp.where(colpos == 0, 0.0, back)
        right = jnp.where(colpos == GRID_W - 1, 0.0, fwd)
        prev = jnp.where(is_first, 0.0, prev)
        nxt = jnp.where(is_last, 0.0, nxt)
        up = jnp.concatenate([prev, cur[:tm - GRID_W]], axis=0)
        down = jnp.concatenate([cur[GRID_W:], nxt], axis=0)
        shifted = jnp.where(grp == 0, left, jnp.where(grp == 1, right, jnp.where(grp == 2, up, down)))
    else:
        before = jnp.where(row == 0, 0.0, back)
        after = jnp.where(row == tm - 1, 0.0, fwd)
        shifted = jnp.where(grp % 2 == 0, before, after)
    return cur + mu * (shifted - cur)


def _state_kernel(*refs, grid_mode, with_read, width):
    if grid_mode:
        (zm_ref, zmp_ref, zmn_ref, zl_ref, zlp_ref, zln_ref), refs = refs[:6], refs[6:]
    else:
        (zm_ref, zl_ref), refs = refs[:2], refs[2:]
        zmp_ref = zmn_ref = zlp_ref = zln_ref = None
    (mum_ref, mul_ref, w0_ref, wdec_ref, a0_ref, wic_ref, kk_ref, ka_ref), refs = refs[:8], refs[8:]
    if with_read:
        (rk_ref, wg_ref), refs = refs[:2], refs[2:]
    (kap_ref, v_ref, ld0_ref, ld1_ref, kd0_ref, kd1_ref, be0_ref, be1_ref), refs = refs[:8], refs[8:]
    if with_read:
        r_ref, bonus_ref, g_ref = refs

    i = pl.program_id(1)
    is_first = i == 0
    is_last = i == pl.num_programs(1) - 1
    ones_bd = _block_diag_mask(MXU_WIDTH, HEAD_DIM).astype(BF16)

    def shifted(z_ref, zp_ref, zn_ref, mu_ref, lo, hi):
        cur = z_ref[0, :, lo:hi].astype(F32)
        prev = zp_ref[0, :, lo:hi].astype(F32) if grid_mode else None
        nxt = zn_ref[0, :, lo:hi].astype(F32) if grid_mode else None
        return _shift_mix(cur, prev, nxt, mu_ref[:, lo:hi], grid_mode, is_first, is_last)

    lora = shifted(zl_ref, zlp_ref, zln_ref, mul_ref, 0, zl_ref.shape[2])
    lw = jnp.tanh(lora[:, :2 * DECAY_LORA]).astype(BF16)
    la = lora[:, 2 * DECAY_LORA:2 * DECAY_LORA + 2 * ICLR_LORA].astype(BF16)
    if with_read:
        lg = _sigmoid(lora[:, 2 * DECAY_LORA + 2 * ICLR_LORA:]).astype(BF16)

    for lo in range(0, width, MXU_WIDTH):
        hi = lo + MXU_WIDTH
        k = shifted(zm_ref, zmp_ref, zmn_ref, mum_ref, lo, hi)
        v = shifted(zm_ref, zmp_ref, zmn_ref, mum_ref, width + lo, width + hi)
        kk = k * kk_ref[:, lo:hi]
        n2 = _head_sum(kk * kk, ones_bd)
        kk = kk / jnp.maximum(jnp.sqrt(n2), NORM_EPS)
        kap_ref[0, :, lo:hi] = kk.astype(kap_ref.dtype)
        v_ref[0, :, lo:hi] = v.astype(v_ref.dtype)
        a_sum = None
        for d, (ld_ref, kd_ref, be_ref) in enumerate(((ld0_ref, kd0_ref, be0_ref), (ld1_ref, kd1_ref, be1_ref))):
            lo_d, hi_d = d * width + lo, d * width + hi
            pre_w = w0_ref[:, lo_d:hi_d] + _dot(lw, wdec_ref[:, lo_d:hi_d])
            w_log = -_softplus(-pre_w) - 0.5
            ld_ref[0, :, lo:hi] = -jnp.exp(w_log)
            a = _sigmoid(a0_ref[:, lo_d:hi_d] + _dot(la, wic_ref[:, lo_d:hi_d]))
            kd_ref[0, :, lo:hi] = (k * (1.0 + (a - 1.0) * ka_ref[:, lo:hi])).astype(kd_ref.dtype)
            be_ref[0, :, lo:hi] = (kk * a).astype(be_ref.dtype)
            a_sum = a if a_sum is None else a_sum + a
        if with_read:
            r = shifted(zm_ref, zmp_ref, zmn_ref, mum_ref, 2 * width + lo, 2 * width + hi)
            r_ref[0, :, lo:hi] = r.astype(r_ref.dtype)
            k_bonus = k * (1.0 + (0.5 * a_sum - 1.0) * ka_ref[:, lo:hi])
            s = _head_sum(r * k_bonus * rk_ref[:, lo:hi], ones_bd)
            bonus_ref[0, :, lo:hi] = (s * v).astype(bonus_ref.dtype)
            g_ref[0, :, lo:hi] = _dot(lg, wg_ref[:, lo:hi]).astype(g_ref.dtype)


def _state_terms(zm, zl, p, grid_mode, with_read, tm):
    b, t, _ = zm.shape
    width = p["k_k"].shape[1]
    nt = t // tm
    row = lambda bi, i: (bi, i, 0)
    const = lambda bi, i: (0, 0)
    ins, in_specs = [], []
    if grid_mode:
        hb = tm // GRID_W
        last = t // GRID_W - 1
        prev = lambda bi, i: (bi, jnp.maximum(i * hb - 1, 0), 0)
        nxt = lambda bi, i: (bi, jnp.minimum((i + 1) * hb, last), 0)
        for z in (zm, zl):
            ins += [z, z, z]
            in_specs += [pl.BlockSpec((1, tm, z.shape[2]), row),
                         pl.BlockSpec((1, GRID_W, z.shape[2]), prev),
                         pl.BlockSpec((1, GRID_W, z.shape[2]), nxt)]
    else:
        for z in (zm, zl):
            ins.append(z)
            in_specs.append(pl.BlockSpec((1, tm, z.shape[2]), row))
    names = ["mu_main", "mu_lora", "w0", "w_dec", "a0", "w_iclr", "k_k", "k_a"]
    if with_read:
        names += ["r_k", "w_gate"]
    for n in names:
        ins.append(p[n])
        in_specs.append(pl.BlockSpec(p[n].shape, const))
    out_dt = [BF16, BF16, F32, F32, BF16, BF16, BF16, BF16] + ([BF16] * 3 if with_read else [])
    return pl.pallas_call(
        functools.partial(_state_kernel, grid_mode=grid_mode, with_read=with_read, width=width),
        grid=(b, nt),
        in_specs=in_specs,
        out_specs=[pl.BlockSpec((1, tm, width), row) for _ in out_dt],
        out_shape=[jax.ShapeDtypeStruct((b, t, width), dt) for dt in out_dt],
        compiler_params=_cparams(("arbitrary", "arbitrary")),
        name="state_terms_grid" if grid_mode else "state_terms_seq",
    )(*ins)


def _pack_bd(y, bd_mask):
    reps = MXU_WIDTH // y.shape[0]
    return jnp.where(bd_mask, jnp.concatenate([y] * reps, axis=0), jnp.zeros((), y.dtype))


def _scan_chunk(insts, with_y):
    c = CHUNK
    bd_mask = _block_diag_mask(MXU_WIDTH, HEAD_DIM)
    ti = lax.broadcasted_iota(jnp.int32, (c, c), 0)
    tj = lax.broadcasted_iota(jnp.int32, (c, c), 1)
    t4 = lax.broadcasted_iota(jnp.int32, (c, MXU_WIDTH), 0)
    i4 = lax.broadcasted_iota(jnp.int32, (c, MXU_WIDTH), 1) % c
    eye4 = (t4 == i4).astype(F32)
    tri = ((tj <= ti).astype(BF16), (tj >= ti).astype(BF16))
    strict4 = (i4 < t4, i4 > t4)
    incl4 = (i4 <= t4, i4 >= t4)
    pack = lambda y: _pack_bd(y, bd_mask)

    for it in insts:
        rev = it["reverse"]
        ld = it["ld"]
        cl = sum(_dot(tri[rev], t) for t in _split3(ld))
        cl_end = cl[0:1] if rev else cl[c - 1:c]
        rel = cl - cl_end
        e_k = jnp.exp(-rel)
        it["kt"] = (it["kd"] * e_k).astype(BF16)
        it["bt"] = (it["be"] * e_k).astype(BF16)
        it["vb"] = it["v"].astype(BF16)
        kq = (it["kap"] * jnp.exp(rel - ld)).astype(BF16)
        it["lhs"] = jnp.concatenate([kq, (it["r"] * jnp.exp(rel)).astype(BF16)], axis=0) if with_y else kq
        it["sd"] = it["s"] * jnp.exp(cl_end)
    for it in insts:
        it["st"] = _dot_nt(it["lhs"], it["sd"].astype(BF16))
        it["a_b"] = _dot_nt(it["lhs"], pack(it["bt"]))
        it["a_k"] = _dot_nt(it["lhs"], pack(it["kt"]))

    for it in insts:
        a = jnp.where(strict4[it["reverse"]], it["a_b"][:c], 0.0)
        it["t_inv"] = eye4 - a
        ab = a.astype(BF16)
        it["pw"] = _dot(ab, pack(ab)).astype(BF16)
    n = 2
    while 2 * n < c:
        for it in insts:
            both = _dot(jnp.concatenate([it["pw"], it["t_inv"].astype(BF16)], axis=0), pack(it["pw"]))
            it["pw"] = both[:c].astype(BF16)
            it["t_inv"] = it["t_inv"] + both[c:]
        n *= 2
    for it in insts:
        it["t_inv"] = it["t_inv"] + _dot(it["t_inv"].astype(BF16), pack(it["pw"]))

    for it in insts:
        rev = it["reverse"]
        a_kk = jnp.where(strict4[rev], it["a_k"][:c], 0.0)
        if with_y:
            a_rk = jnp.where(incl4[rev], it["a_k"][c:], 0.0)
            it["av"] = _dot(jnp.concatenate([a_kk, a_rk], axis=0).astype(BF16), pack(it["vb"]))
        else:
            it["av"] = _dot(a_kk.astype(BF16), pack(it["vb"]))
    for it in insts:
        rhs = it["st"][:c] + it["av"][:c]
        it["ub"] = _dot(it["t_inv"].astype(BF16), pack(rhs.astype(BF16))).astype(BF16)
    out = []
    for it in insts:
        y = None
        if with_y:
            a_rb = jnp.where(incl4[it["reverse"]], it["a_b"][c:], 0.0)
            y = it["st"][c:] + it["av"][c:] - _dot(a_rb.astype(BF16), pack(it["ub"]))
        delta = _dot_tn(jnp.concatenate([it["vb"], it["ub"]], axis=0),
                        jnp.concatenate([it["kt"], -it["bt"]], axis=0))
        out.append((it["sd"] + jnp.where(bd_mask, delta, 0.0), y))
    return out


def _scan_kernel(*refs, with_y, width):
    n_in = 12 if with_y else 10
    in_refs, refs = refs[:n_in], refs[n_in:]
    s0_ref, refs = refs[0], refs[1:]
    if with_y:
        y_refs, refs = refs[:2], refs[2:]
    sfin_ref, s_scr = refs
    c_idx = pl.program_id(1)
    n_groups = width // MXU_WIDTH

    @pl.when(c_idx == 0)
    def _():
        s_scr[...] = s0_ref[0]

    per_dir = 6 if with_y else 5
    names = ("ld", "kap", "kd", "be", "v", "r")[:per_dir]
    insts = []
    for d in range(2):
        d_refs = in_refs[d * per_dir:(d + 1) * per_dir]
        for g in range(n_groups):
            lo, hi = g * MXU_WIDTH, (g + 1) * MXU_WIDTH
            it = {n: ref[0, :, lo:hi].astype(F32) for n, ref in zip(names, d_refs)}
            it.update(reverse=d, s=s_scr[d, g], d=d, g=g)
            insts.append(it)
    for it, (s_new, y) in zip(insts, _scan_chunk(insts, with_y)):
        s_scr[it["d"], it["g"]] = s_new
        if with_y:
            lo = it["g"] * MXU_WIDTH
            y_refs[it["d"]][0, :, lo:lo + MXU_WIDTH] = y

    @pl.when(c_idx == pl.num_programs(1) - 1)
    def _():
        sfin_ref[0] = s_scr[...]


def _wkv_scan(dirs, kap, v, r, s0):
    b, t, width = kap.shape
    nc = t // CHUNK
    with_y = r is not None
    fwd = lambda bi, ci: (bi, ci, 0)
    bwd = lambda bi, ci: (bi, nc - 1 - ci, 0)
    ins, in_specs = [], []
    for d, imap in enumerate((fwd, bwd)):
        ld, kd, be = dirs[d]
        for arr in (ld, kap, kd, be, v) + ((r,) if with_y else ()):
            ins.append(arr)
            in_specs.append(pl.BlockSpec((1, CHUNK, width), imap))
    n_groups = width // MXU_WIDTH
    s_shape = (2, n_groups, MXU_WIDTH, MXU_WIDTH)
    s_spec = pl.BlockSpec((1,) + s_shape, lambda bi, ci: (bi, 0, 0, 0, 0))
    ins.append(s0)
    in_specs.append(s_spec)
    out_specs, out_shape = [], []
    if with_y:
        out_specs += [pl.BlockSpec((1, CHUNK, width), fwd), pl.BlockSpec((1, CHUNK, width), bwd)]
        out_shape += [jax.ShapeDtypeStruct((b, t, width), F32)] * 2
    out_specs.append(s_spec)
    out_shape.append(jax.ShapeDtypeStruct((b,) + s_shape, F32))
    return pl.pallas_call(
        functools.partial(_scan_kernel, with_y=with_y, width=width),
        grid=(b, nc),
        in_specs=in_specs,
        out_specs=out_specs,
        out_shape=out_shape,
        scratch_shapes=[pltpu.VMEM(s_shape, F32)],
        compiler_params=_cparams(("arbitrary", "arbitrary")),
        name="wkv_scan" if with_y else "wkv_scan_ctx",
    )(*ins)


def _merge_kernel(yf_ref, yb_ref, bonus_ref, g_ref, zb_ref, p_ref, pp_ref, pn_ref, zg_ref, x_ref,
                  lnw_ref, lnb_ref, cw_ref, wout_ref, npost_ref, g1_ref, npre_ref, sc2_ref, sh2_ref,
                  wr_ref, br_ref,
                  x2_ref, h2_ref, gate_ref, idx_ref, rank_ref, cnt_ref, run_scr, *, width):
    first = jnp.logical_and(pl.program_id(0) == 0, pl.program_id(1) == 0)
    i = pl.program_id(1)
    tm = x_ref.shape[1]

    @pl.when(first)
    def _():
        run_scr[...] = jnp.zeros_like(run_scr)

    ones_bd = _block_diag_mask(MXU_WIDTH, HEAD_DIM).astype(BF16)
    row = lax.broadcasted_iota(jnp.int32, (tm, MXU_WIDTH), 0)
    halo = pp_ref.shape[1]
    parts = []
    for lo in range(0, width, MXU_WIDTH):
        hi = lo + MXU_WIDTH
        y = yf_ref[0, :, lo:hi] + yb_ref[0, :, lo:hi]
        mean = _head_sum(y, ones_bd) * (1.0 / HEAD_DIM)
        yc = y - mean
        var = _head_sum(yc * yc, ones_bd) * (1.0 / HEAD_DIM)
        yn = yc * lax.rsqrt(var + GN_EPS) * lnw_ref[:, lo:hi] + lnb_ref[:, lo:hi]
        y_rwkv = (yn + bonus_ref[0, :, lo:hi].astype(F32)) * g_ref[0, :, lo:hi].astype(F32)
        p = p_ref[0, :, lo:hi].astype(F32)
        p_prev = jnp.where(i == 0, 0.0, pp_ref[0, halo - 1:halo, lo:hi].astype(F32))
        p_next = jnp.where(i == pl.num_programs(1) - 1, 0.0, pn_ref[0, 0:1, lo:hi].astype(F32))
        before = jnp.where(row == 0, p_prev, pltpu.roll(p, 1, axis=0))
        after = jnp.where(row == tm - 1, p_next, pltpu.roll(p, tm - 1, axis=0))
        conv = cw_ref[0:1, lo:hi] * before + cw_ref[1:2, lo:hi] * p + cw_ref[2:3, lo:hi] * after
        y_conv = zb_ref[0, :, lo:hi].astype(F32) * conv
        ga = _sigmoid(zg_ref[0, :, lo:hi].astype(F32))
        gb = _sigmoid(zg_ref[0, :, width + lo:width + hi].astype(F32))
        parts.append((ga * y_rwkv + gb * y_conv).astype(BF16))
    merged = jnp.concatenate(parts, axis=1)
    mix = _dot(merged, wout_ref[...])
    x2 = x_ref[0] + g1_ref[0] * _rms(mix, npost_ref[...])
    x2_ref[0] = x2
    h2 = _rms(x2, npre_ref[...]) * (1.0 + sc2_ref[0]) + sh2_ref[0]
    h2_ref[0] = h2.astype(h2_ref.dtype)

    logits = _dot(h2, wr_ref[...]) + br_ref[...]
    lane = lax.broadcasted_iota(jnp.int32, logits.shape, 1)
    vals, idxs = [], []
    sel = jnp.zeros(logits.shape, F32)
    work = logits
    for _ in range(TOP_K):
        m = jnp.max(work, axis=-1, keepdims=True)
        idx = jnp.min(jnp.where(work == m, lane, logits.shape[1]), axis=-1, keepdims=True)
        hit = lane == idx
        vals.append(m)
        idxs.append(idx)
        sel = jnp.where(hit, 1.0, sel)
        work = jnp.where(hit, -jnp.inf, work)
    exps = [jnp.exp(vk - vals[0]) for vk in vals]
    denom = exps[0] + exps[1] + exps[2] + exps[3]
    r_i = lax.broadcasted_iota(jnp.int32, (tm, tm), 0)
    c_i = lax.broadcasted_iota(jnp.int32, (tm, tm), 1)
    before_cnt = _dot((c_i < r_i).astype(BF16), sel.astype(BF16)) + run_scr[...]
    gate_out = jnp.zeros(logits.shape, F32)
    idx_out = jnp.zeros(logits.shape, jnp.int32)
    rank_out = jnp.zeros(logits.shape, jnp.int32)
    for k in range(TOP_K):
        rank_k = jnp.sum(jnp.where(lane == idxs[k], before_cnt, 0.0), axis=-1, keepdims=True)
        gate_out = jnp.where(lane == k, exps[k] / denom, gate_out)
        idx_out = jnp.where(lane == k, idxs[k], idx_out)
        rank_out = jnp.where(lane == k, rank_k.astype(jnp.int32), rank_out)
    gate_ref[0] = gate_out
    idx_ref[0] = idx_out
    rank_ref[0] = rank_out
    run_scr[...] = run_scr[...] + jnp.sum(sel, axis=0, keepdims=True)
    cnt_ref[...] = run_scr[...]


def _merge_route(yf, yb, bonus, g, zb, p, zg, x, pr, tm):
    b, t, d = x.shape
    width = yf.shape[2]
    halo = 8
    hb = tm // halo
    last = t // halo - 1
    row = lambda bi, i: (bi, i, 0)
    const = lambda bi, i: (0, 0)
    per_b = lambda bi, i: (bi, 0, 0)
    prev = lambda bi, i: (bi, jnp.maximum(i * hb - 1, 0), 0)
    nxt = lambda bi, i: (bi, jnp.minimum((i + 1) * hb, last), 0)
    lanes = 128
    in_specs = [pl.BlockSpec((1, tm, width), row)] * 6
    in_specs += [pl.BlockSpec((1, halo, width), prev), pl.BlockSpec((1, halo, width), nxt),
                 pl.BlockSpec((1, tm, 2 * width), row), pl.BlockSpec((1, tm, d), row)]
    params = [pr["ln_x_w"], pr["ln_x_b"], pr["conv_w"], pr["w_out"], pr["norm_post_mix"]]
    in_specs += [pl.BlockSpec(a.shape, const) for a in params]
    in_specs.append(pl.BlockSpec((1, 1, d), per_b))
    in_specs.append(pl.BlockSpec(pr["norm_pre_ffn"].shape, const))
    in_specs += [pl.BlockSpec((1, 1, d), per_b)] * 2
    in_specs += [pl.BlockSpec(pr["w_router"].shape, const), pl.BlockSpec(pr["b_router"].shape, const)]
    out_specs = [pl.BlockSpec((1, tm, d), row), pl.BlockSpec((1, tm, d), row)]
    out_specs += [pl.BlockSpec((1, tm, lanes), row)] * 3
    out_specs.append(pl.BlockSpec((1, lanes), const))
    out_shape = [jax.ShapeDtypeStruct((b, t, d), F32), jax.ShapeDtypeStruct((b, t, d), F32),
                 jax.ShapeDtypeStruct((b, t, lanes), F32), jax.ShapeDtypeStruct((b, t, lanes), jnp.int32),
                 jax.ShapeDtypeStruct((b, t, lanes), jnp.int32), jax.ShapeDtypeStruct((1, lanes), F32)]
    return pl.pallas_call(
        functools.partial(_merge_kernel, width=width),
        grid=(b, t // tm),
        in_specs=in_specs,
        out_specs=out_specs,
        out_shape=out_shape,
        scratch_shapes=[pltpu.VMEM((1, lanes), F32)],
        compiler_params=_cparams(("arbitrary", "arbitrary")),
        name="merge_route",
    )(yf, yb, bonus, g, zb, p, p, p, zg, x, *params, pr["g1"], pr["norm_pre_ffn"], pr["sc2"], pr["sh2"],
      pr["w_router"], pr["b_router"])


def _row_copy(src_ref, src_row, dst_ref, dst_row, sem):
    return pltpu.make_async_copy(src_ref.at[pl.ds(src_row, 1)], dst_ref.at[pl.ds(dst_row, 1)], sem)


def _dispatch_kernel(dest_ref, nrows_ref, h2_ref, xb_ref, zrow, sem, zsem, *, tm, n_pad):
    base = pl.program_id(0) * (tm * TOP_K)

    @pl.when(pl.program_id(0) == 0)
    def _():
        zrow[...] = jnp.zeros_like(zrow)

        def per_block(j, carry):
            def fill(r, c2):
                _row_copy(zrow, 0, xb_ref, j * MOE_BLOCK + r, zsem).start()
                return c2
            return lax.fori_loop(nrows_ref[j], MOE_BLOCK, fill, carry)

        lax.fori_loop(0, nrows_ref.shape[0], per_block, 0)
        pltpu.make_async_copy(xb_ref.at[pl.ds(0, n_pad)], xb_ref.at[pl.ds(0, n_pad)], zsem).wait()

    def issue(r, carry):
        for k in range(TOP_K):
            _row_copy(h2_ref, r, xb_ref, dest_ref[base + r * TOP_K + k], sem).start()
        return carry

    lax.fori_loop(0, tm, issue, 0, unroll=4)
    for _ in range(TOP_K):
        pltpu.make_async_copy(h2_ref, xb_ref.at[pl.ds(0, tm)], sem).wait()


def _dispatch(h2, dest, block_rows, tm):
    n_tok, d = h2.shape
    n_rows = block_rows.shape[0] * MOE_BLOCK
    grid_spec = pltpu.PrefetchScalarGridSpec(
        num_scalar_prefetch=2,
        grid=(n_tok // tm,),
        in_specs=[pl.BlockSpec((tm, d), lambda i, dest, nr: (i, 0))],
        out_specs=pl.BlockSpec(memory_space=pl.ANY),
        scratch_shapes=[pltpu.VMEM((8, d), h2.dtype), pltpu.SemaphoreType.DMA, pltpu.SemaphoreType.DMA],
    )
    return pl.pallas_call(
        functools.partial(_dispatch_kernel, tm=tm, n_pad=n_rows - dest.shape[0]),
        grid_spec=grid_spec,
        out_shape=jax.ShapeDtypeStruct((n_rows, d), h2.dtype),
        compiler_params=_cparams(("arbitrary",)),
        name="moe_dispatch",
    )(dest, block_rows, h2)


def _expert_kernel(be_ref, nrows_ref, xb_ref, win_ref, bin_ref, wout_ref, bout_ref, o_ref, win_scr, wout_scr,
                   *, d_ff):
    i = pl.program_id(0)
    changed = jnp.logical_or(i == 0, be_ref[i] != be_ref[jnp.maximum(i - 1, 0)])
    n_rows = nrows_ref[i]

    @pl.when(jnp.logical_and(changed, n_rows > 0))
    def _():
        win_scr[...] = win_ref[0].astype(BF16)
        wout_scr[...] = wout_ref[0].astype(BF16)

    @pl.when(n_rows > 0)
    def _():
        row = lax.broadcasted_iota(jnp.int32, xb_ref.shape, 0)
        xb = jnp.where(row < n_rows, xb_ref[...], 0.0).astype(BF16)
        gu = _dot(xb, win_scr[...]) + bin_ref[0]
        gate = jnp.minimum(gu[:, :d_ff], SWIGLU_LIMIT)
        up = jnp.clip(gu[:, d_ff:], -SWIGLU_LIMIT, SWIGLU_LIMIT)
        act = (up + 1.0) * gate * _sigmoid(SWIGLU_ALPHA * gate)
        o_ref[...] = _dot(act.astype(BF16), wout_scr[...]) + bout_ref[0]

    @pl.when(n_rows == 0)
    def _():
        o_ref[...] = jnp.zeros_like(o_ref)


def _experts(xb, block_e, block_rows, w_in, b_in, w_out, b_out):
    n_rows, d = xb.shape
    n_e, _, d_ff2 = w_in.shape
    d_ff = d_ff2 // 2
    n_blocks = n_rows // MOE_BLOCK
    grid_spec = pltpu.PrefetchScalarGridSpec(
        num_scalar_prefetch=2,
        grid=(n_blocks,),
        in_specs=[pl.BlockSpec((MOE_BLOCK, d), lambda i, be, nr: (i, 0)),
                  pl.BlockSpec((1, d, d_ff2), lambda i, be, nr: (be[i], 0, 0)),
                  pl.BlockSpec((1, 1, d_ff2), lambda i, be, nr: (be[i], 0, 0)),
                  pl.BlockSpec((1, d_ff, d), lambda i, be, nr: (be[i], 0, 0)),
                  pl.BlockSpec((1, 1, d), lambda i, be, nr: (be[i], 0, 0))],
        out_specs=pl.BlockSpec((MOE_BLOCK, d), lambda i, be, nr: (i, 0)),
        scratch_shapes=[pltpu.VMEM((d, d_ff2), BF16), pltpu.VMEM((d_ff, d), BF16)],
    )
    return pl.pallas_call(
        functools.partial(_expert_kernel, d_ff=d_ff),
        grid_spec=grid_spec,
        out_shape=jax.ShapeDtypeStruct((n_rows, d), F32),
        compiler_params=_cparams(("arbitrary",)),
        name="moe_experts",
    )(block_e, block_rows, xb, w_in, b_in.reshape(n_e, 1, d_ff2), w_out, b_out.reshape(n_e, 1, d))


def _combine_kernel(dest_ref, yb_ref, gate_ref, x2_ref, npost_ref, g2_ref, o_ref, ybuf, sem, *, tm):
    tile = pl.program_id(0) * pl.num_programs(1) + pl.program_id(1)
    base = tile * (tm * TOP_K)

    def issue(r, carry):
        for k in range(TOP_K):
            _row_copy(yb_ref, dest_ref[base + r * TOP_K + k], ybuf.at[k], r, sem).start()
        return carry

    lax.fori_loop(0, tm, issue, 0, unroll=4)
    for k in range(TOP_K):
        pltpu.make_async_copy(yb_ref.at[pl.ds(0, tm)], ybuf.at[k], sem).wait()
    gates = gate_ref[0]
    f = None
    for k in range(TOP_K):
        term = gates[:, k:k + 1] * ybuf[k]
        f = term if f is None else f + term
    o_ref[0] = x2_ref[0] + g2_ref[0] * _rms(f, npost_ref[...])


def _combine(yb_rows, dest, gates, x2, npost, g2, tm):
    b, t, d = x2.shape
    row = lambda bi, i, dest: (bi, i, 0)
    grid_spec = pltpu.PrefetchScalarGridSpec(
        num_scalar_prefetch=1,
        grid=(b, t // tm),
        in_specs=[pl.BlockSpec(memory_space=pl.ANY),
                  pl.BlockSpec((1, tm, gates.shape[2]), row),
                  pl.BlockSpec((1, tm, d), row),
                  pl.BlockSpec((1, d), lambda bi, i, dest: (0, 0)),
                  pl.BlockSpec((1, 1, d), lambda bi, i, dest: (bi, 0, 0))],
        out_specs=pl.BlockSpec((1, tm, d), row),
        scratch_shapes=[pltpu.VMEM((TOP_K, tm, d), F32), pltpu.SemaphoreType.DMA],
    )
    return pl.pallas_call(
        functools.partial(_combine_kernel, tm=tm),
        grid_spec=grid_spec,
        out_shape=jax.ShapeDtypeStruct((b, t, d), F32),
        compiler_params=_cparams(("arbitrary", "arbitrary")),
        name="moe_combine",
    )(dest, yb_rows, gates, x2, npost, g2)


def _lora_up(w_up):
    _, r, w = w_up.shape
    z = jnp.zeros((r, w), w_up.dtype)
    return jnp.concatenate([jnp.concatenate([w_up[0], z], axis=1),
                            jnp.concatenate([z, w_up[1]], axis=1)], axis=0).astype(BF16)


def _row_tile(t, pref):
    return pref if t % pref == 0 else t


def kernel(x, c, ctx, c_ctx, w_ada, b_ada, norm_pre_mix, norm_post_mix, norm_pre_ffn, norm_post_ffn,
           w_in, mu_shift, w0, w_decay_up, a0, w_iclr_up, k_k, k_a, r_k, w_gate_up, ln_x_w, ln_x_b,
           conv_w, w_out, w_router, b_router, w_exp_in, b_exp_in, w_exp_out, b_exp_out):
    b, t, d = x.shape
    t_ctx = ctx.shape[1]
    depth = w_ada.shape[0]
    width = k_k.shape[1]
    n_groups = width // MXU_WIDTH
    k_off, v_off = 0, width
    decay_off = 2 * width
    iclr_off = decay_off + 2 * DECAY_LORA
    r_off = iclr_off + 2 * ICLR_LORA
    glora_off = r_off + width
    conv_off = glora_off + GATE_LORA
    gate_off = conv_off + 3 * width

    xc = ctx
    for l in range(depth):
        last = l == depth - 1
        rows = jnp.concatenate([c, c_ctx[None, :], jnp.zeros((8 - b - 1, d), F32)], axis=0)
        mod = _ada(rows, w_ada[l], b_ada[l])
        sh1, sc1, g1, sh2, sc2, g2 = [mod[:b, None, j * d:(j + 1) * d] for j in range(6)]
        csh1, csc1, cg1, csh2, csc2, cg2 = [jnp.broadcast_to(mod[b:b + 1, None, j * d:(j + 1) * d], (b, 1, d))
                                            for j in range(6)]

        wl = w_in[l]
        cols = lambda lo, n: wl[:, lo:lo + n]
        w_main = jnp.concatenate([cols(k_off, width), cols(v_off, width), cols(r_off, width)], axis=1).astype(BF16)
        w_lora = jnp.concatenate([cols(decay_off, 2 * DECAY_LORA), cols(iclr_off, 2 * ICLR_LORA),
                                  cols(glora_off, GATE_LORA)], axis=1).astype(BF16)
        w_conv = tuple(cols(conv_off + j * width, width).astype(BF16) for j in range(3))
        w_gate = cols(gate_off, 2 * d).astype(BF16)
        mu = mu_shift[l]
        mu_main = jnp.concatenate([mu[k_off:k_off + width], mu[v_off:v_off + width], mu[r_off:r_off + width]])[None, :]
        mu_lora = jnp.concatenate([mu[decay_off:decay_off + 2 * DECAY_LORA], mu[iclr_off:iclr_off + 2 * ICLR_LORA],
                                   mu[glora_off:glora_off + GATE_LORA]])[None, :]
        sp = {
            "mu_main": mu_main, "mu_lora": mu_lora,
            "w0": w0[l].reshape(1, 2 * width), "w_dec": _lora_up(w_decay_up[l]),
            "a0": a0[l].reshape(1, 2 * width), "w_iclr": _lora_up(w_iclr_up[l]),
            "k_k": k_k[l][None, :], "k_a": k_a[l][None, :],
            "r_k": r_k[l].reshape(1, width), "w_gate": w_gate_up[l].astype(BF16),
        }
        if not last:
            raise NotImplementedError("context stream update for non-final layers")

        n_state_main = 2 * width
        zc_main, zc_lora = _inproj(xc, norm_pre_mix[l], csc1, csh1,
                                   [w_main[:, :n_state_main], w_lora[:, :2 * DECAY_LORA + 2 * ICLR_LORA]],
                                   None, _row_tile(t_ctx, 256))
        sp_ctx = dict(sp, mu_main=mu_main[:, :n_state_main], mu_lora=mu_lora[:, :2 * DECAY_LORA + 2 * ICLR_LORA])
        kap_c, v_c, ld0_c, ld1_c, kd0_c, kd1_c, be0_c, be1_c = _state_terms(
            zc_main, zc_lora, sp_ctx, grid_mode=False, with_read=False, tm=t_ctx)
        s0 = jnp.zeros((b, 2, n_groups, MXU_WIDTH, MXU_WIDTH), F32)
        (s_ctx,) = _wkv_scan([(ld0_c, kd0_c, be0_c), (ld1_c, kd1_c, be1_c)], kap_c, v_c, None, s0)

        zm, zl, zg, zb, p = _inproj(x, norm_pre_mix[l], sc1, sh1, [w_main, w_lora, w_gate], w_conv,
                                    _row_tile(t, 512))
        kap, v, ld0, ld1, kd0, kd1, be0, be1, r, bonus, g = _state_terms(
            zm, zl, sp, grid_mode=True, with_read=True, tm=_row_tile(t, 512))
        yf, yb, _ = _wkv_scan([(ld0, kd0, be0), (ld1, kd1, be1)], kap, v, r, s_ctx)

        lanes = 128
        pad_e = lanes - N_EXPERTS
        pr = {
            "ln_x_w": ln_x_w[l][None, :], "ln_x_b": ln_x_b[l][None, :], "conv_w": conv_w[l],
            "w_out": w_out[l].astype(BF16), "norm_post_mix": norm_post_mix[l][None, :], "g1": g1,
            "norm_pre_ffn": norm_pre_ffn[l][None, :], "sc2": sc2, "sh2": sh2,
            "w_router": jnp.pad(w_router[l], ((0, 0), (0, pad_e))),
            "b_router": jnp.pad(b_router[l], (0, pad_e), constant_values=-jnp.inf)[None, :],
        }
        x2, h2, gates, top_idx, rank, counts = _merge_route(yf, yb, bonus, g, zb, p, zg, x, pr, _row_tile(t, 256))

        n_tok = b * t
        nk = n_tok * TOP_K
        counts = counts[0, :N_EXPERTS].astype(jnp.int32)
        padded = (counts + MOE_BLOCK - 1) // MOE_BLOCK * MOE_BLOCK
        pad_ends = jnp.cumsum(padded)
        pad_starts = pad_ends - padded
        experts = jnp.arange(N_EXPERTS, dtype=jnp.int32)
        e_flat = top_idx[..., :TOP_K].reshape(nk)
        dest = jnp.sum(jnp.where(e_flat[:, None] == experts[None, :], pad_starts[None, :], 0), axis=1) \
            + rank[..., :TOP_K].reshape(nk)
        n_blocks = -(-(nk + N_EXPERTS * (MOE_BLOCK - 1)) // MOE_BLOCK)
        n_rows = n_blocks * MOE_BLOCK
        blk_start = jnp.arange(n_blocks, dtype=jnp.int32) * MOE_BLOCK
        block_e = jnp.minimum(jnp.sum((pad_ends[None, :] <= blk_start[:, None]).astype(jnp.int32), axis=1),
                              N_EXPERTS - 1)
        own = block_e[:, None] == experts[None, :]
        seg_end = jnp.sum(jnp.where(own, (pad_starts + counts)[None, :], 0), axis=1)
        block_rows = jnp.clip(seg_end - blk_start, 0, MOE_BLOCK).astype(jnp.int32)
        xb = _dispatch(h2.reshape(n_tok, d), dest, block_rows, _row_tile(n_tok, 256))
        yb_rows = _experts(xb, block_e, block_rows, w_exp_in[l], b_exp_in[l], w_exp_out[l], b_exp_out[l])
        x = _combine(yb_rows, dest, gates, x2, norm_post_ffn[l][None, :], g2, _row_tile(t, 256))
    return x
```

```python
import functools

import jax
import jax.numpy as jnp
from jax import lax
from jax.experimental import pallas as pl
from jax.experimental.pallas import tpu as pltpu

F32 = jnp.float32
BF16 = jnp.bfloat16

HEAD_DIM = 64
GRID_W = 64
DECAY_LORA = 64
ICLR_LORA = 64
GATE_LORA = 128
GN_EPS = 64e-5
NORM_EPS = 1e-12
CONV_K = 3
N_EXPERTS = 32
TOP_K = 4
SWIGLU_LIMIT = 7.0
SWIGLU_ALPHA = 1.702
MOE_BLOCK = 256
RMS_EPS = 1e-6

MXU_WIDTH = 256
HEADS_PER_GROUP = MXU_WIDTH // HEAD_DIM
CHUNK = 64
SCAN_SUB = 4
DECAY_SCALE = 0.6065306597126334
VMEM_LIMIT = 56 * 1024 * 1024


def _cparams(sem):
    return pltpu.CompilerParams(dimension_semantics=sem, vmem_limit_bytes=VMEM_LIMIT)


def _dot(a, b):
    return jnp.dot(a, b, preferred_element_type=F32)


def _dot_nt(a, b):
    return lax.dot_general(a, b, (((1,), (1,)), ((), ())), preferred_element_type=F32)


def _dot_tn(a, b):
    return lax.dot_general(a, b, (((0,), (0,)), ((), ())), preferred_element_type=F32)


def _split3(x):
    h = x.astype(BF16)
    r = x - h.astype(F32)
    m = r.astype(BF16)
    l = (r - m.astype(F32)).astype(BF16)
    return h, m, l


def _block_diag_mask(n, blk):
    r = lax.broadcasted_iota(jnp.int32, (n, n), 0) // blk
    c = lax.broadcasted_iota(jnp.int32, (n, n), 1) // blk
    return r == c


def _head_sum(x, ones_bd):
    h = x.astype(BF16)
    l = (x - h.astype(F32)).astype(BF16)
    return _dot(h, ones_bd) + _dot(l, ones_bd)


def _rms(x, gain):
    return x * lax.rsqrt(jnp.mean(x * x, axis=-1, keepdims=True) + RMS_EPS) * gain


def _sigmoid(x):
    return 1.0 / (1.0 + jnp.exp(-x))


def _ada_kernel(c_ref, w_ref, b_ref, o_ref):
    c = c_ref[...]
    s = c * _sigmoid(c)
    o_ref[...] = _dot(s, w_ref[...]) + b_ref[...]


def _ada(c_rows, w, b):
    rows, d = c_rows.shape
    n = w.shape[1]
    tn = 1024
    return pl.pallas_call(
        _ada_kernel,
        grid=(n // tn,),
        in_specs=[pl.BlockSpec((rows, d), lambda j: (0, 0)),
                  pl.BlockSpec((d, tn), lambda j: (0, j)),
                  pl.BlockSpec((1, tn), lambda j: (0, j))],
        out_specs=pl.BlockSpec((rows, tn), lambda j: (0, j)),
        out_shape=jax.ShapeDtypeStruct((rows, n), F32),
        compiler_params=_cparams(("arbitrary",)),
        name="ada_mod",
    )(c_rows, w, b.reshape(1, n))


def _inproj_kernel(x_ref, g_ref, sc_ref, sh_ref, *refs, n_plain, with_conv):
    n_w = n_plain + (3 if with_conv else 0)
    w_refs, o_refs = refs[:n_w], refs[n_w:]
    x = x_ref[0]
    h = _rms(x, g_ref[...]) * (1.0 + sc_ref[0]) + sh_ref[0]
    hb = h.astype(BF16)
    col = 512
    for w_ref, o_ref in zip(w_refs[:n_plain], o_refs[:n_plain]):
        n = w_ref.shape[1]
        for j in range(0, n, col):
            e = min(j + col, n)
            o_ref[0, :, j:e] = _dot(hb, w_ref[:, j:e]).astype(o_ref.dtype)
    if with_conv:
        wb_ref, wc_ref, wx_ref = w_refs[n_plain:]
        ob_ref, op_ref = o_refs[n_plain:]
        n = wb_ref.shape[1]
        for j in range(0, n, col):
            e = min(j + col, n)
            ob_ref[0, :, j:e] = _dot(hb, wb_ref[:, j:e]).astype(ob_ref.dtype)
            op_ref[0, :, j:e] = (_dot(hb, wc_ref[:, j:e]) * _dot(hb, wx_ref[:, j:e])).astype(op_ref.dtype)


def _inproj(x, gain, sc, sh, plain_ws, conv_ws, tm):
    b, t, d = x.shape
    with_conv = conv_ws is not None
    ws = list(plain_ws) + (list(conv_ws) if with_conv else [])
    out_w = [w.shape[1] for w in plain_ws] + ([conv_ws[0].shape[1]] * 2 if with_conv else [])
    row = lambda bi, i: (bi, i, 0)
    const = lambda bi, i: (0, 0)
    in_specs = [pl.BlockSpec((1, tm, d), row),
                pl.BlockSpec((1, d), const),
                pl.BlockSpec((1, 1, d), lambda bi, i: (bi, 0, 0)),
                pl.BlockSpec((1, 1, d), lambda bi, i: (bi, 0, 0))]
    in_specs += [pl.BlockSpec(w.shape, const, pipeline_mode=pl.Buffered(1)) for w in ws]
    return pl.pallas_call(
        functools.partial(_inproj_kernel, n_plain=len(plain_ws), with_conv=with_conv),
        grid=(b, t // tm),
        in_specs=in_specs,
        out_specs=[pl.BlockSpec((1, tm, n), row) for n in out_w],
        out_shape=[jax.ShapeDtypeStruct((b, t, n), BF16) for n in out_w],
        compiler_params=_cparams(("arbitrary", "arbitrary")),
        name="in_proj",
    )(x, gain.reshape(1, d), sc, sh, *ws)


def _shift_mix(cur, prev, nxt, mu, grid_mode, is_first, is_last):
    tm, w = cur.shape
    row = lax.broadcasted_iota(jnp.int32, (tm, w), 0)
    grp = lax.broadcasted_iota(jnp.int32, (tm, w), 1) % 4
    back = pltpu.roll(cur, 1, axis=0)
    fwd = pltpu.roll(cur, tm - 1, axis=0)
    if grid_mode:
        colpos = row % GRID_W
        left = jnp.where(colpos == 0, 0.0, back)
        right = jnp.where(colpos == GRID_W - 1, 0.0, fwd)
        prev = jnp.where(is_first, 0.0, prev)
        nxt = jnp.where(is_last, 0.0, nxt)
        up = jnp.concatenate([prev, cur[:tm - GRID_W]], axis=0)
        down = jnp.concatenate([cur[GRID_W:], nxt], axis=0)
        shifted = jnp.where(grp == 0, left, jnp.where(grp == 1, right, jnp.where(grp == 2, up, down)))
    else:
        before = jnp.where(row == 0, 0.0, back)
        after = jnp.where(row == tm - 1, 0.0, fwd)
        shifted = jnp.where(grp % 2 == 0, before, after)
    return cur + mu * (shifted - cur)


def _state_kernel(*refs, grid_mode, with_read, width):
    if grid_mode:
        (zm_ref, zmp_ref, zmn_ref, zl_ref, zlp_ref, zln_ref), refs = refs[:6], refs[6:]
    else:
        (zm_ref, zl_ref), refs = refs[:2], refs[2:]
        zmp_ref = zmn_ref = zlp_ref = zln_ref = None
    (mum_ref, mul_ref, w0_ref, wdec_ref, a0_ref, wic_ref, kk_ref, ka_ref), refs = refs[:8], refs[8:]
    if with_read:
        (rk_ref, wg_ref), refs = refs[:2], refs[2:]
    (kap_ref, v_ref, ld0_ref, ld1_ref, kd0_ref, kd1_ref, be0_ref, be1_ref), refs = refs[:8], refs[8:]
    if with_read:
        r_ref, bonus_ref, g_ref = refs

    i = pl.program_id(1)
    is_first = i == 0
    is_last = i == pl.num_programs(1) - 1
    ones_bd = _block_diag_mask(MXU_WIDTH, HEAD_DIM).astype(BF16)

    def shifted(z_ref, zp_ref, zn_ref, mu_ref, lo, hi):
        cur = z_ref[0, :, lo:hi].astype(F32)
        prev = zp_ref[0, :, lo:hi].astype(F32) if grid_mode else None
        nxt = zn_ref[0, :, lo:hi].astype(F32) if grid_mode else None
        return _shift_mix(cur, prev, nxt, mu_ref[:, lo:hi], grid_mode, is_first, is_last)

    lora = shifted(zl_ref, zlp_ref, zln_ref, mul_ref, 0, zl_ref.shape[2])
    lw = jnp.tanh(lora[:, :2 * DECAY_LORA]).astype(BF16)
    la = lora[:, 2 * DECAY_LORA:2 * DECAY_LORA + 2 * ICLR_LORA].astype(BF16)
    if with_read:
        lg = _sigmoid(lora[:, 2 * DECAY_LORA + 2 * ICLR_LORA:]).astype(BF16)

    for lo in range(0, width, MXU_WIDTH):
        hi = lo + MXU_WIDTH
        k = shifted(zm_ref, zmp_ref, zmn_ref, mum_ref, lo, hi)
        v = shifted(zm_ref, zmp_ref, zmn_ref, mum_ref, width + lo, width + hi)
        kk = k * kk_ref[:, lo:hi]
        n2 = _head_sum(kk * kk, ones_bd)
        kk = kk / jnp.maximum(jnp.sqrt(n2), NORM_EPS)
        kap_ref[0, :, lo:hi] = kk.astype(kap_ref.dtype)
        v_ref[0, :, lo:hi] = v.astype(v_ref.dtype)
        a_sum = None
        for d, (ld_ref, kd_ref, be_ref) in enumerate(((ld0_ref, kd0_ref, be0_ref), (ld1_ref, kd1_ref, be1_ref))):
            lo_d, hi_d = d * width + lo, d * width + hi
            pre_w = w0_ref[:, lo_d:hi_d] + _dot(lw, wdec_ref[:, lo_d:hi_d])
            ld_ref[0, :, lo:hi] = -DECAY_SCALE * _sigmoid(pre_w)
            a = _sigmoid(a0_ref[:, lo_d:hi_d] + _dot(la, wic_ref[:, lo_d:hi_d]))
            kd_ref[0, :, lo:hi] = (k * (1.0 + (a - 1.0) * ka_ref[:, lo:hi])).astype(kd_ref.dtype)
            be_ref[0, :, lo:hi] = (kk * a).astype(be_ref.dtype)
            a_sum = a if a_sum is None else a_sum + a
        if with_read:
            r = shifted(zm_ref, zmp_ref, zmn_ref, mum_ref, 2 * width + lo, 2 * width + hi)
            r_ref[0, :, lo:hi] = r.astype(r_ref.dtype)
            k_bonus = k * (1.0 + (0.5 * a_sum - 1.0) * ka_ref[:, lo:hi])
            s = _head_sum(r * k_bonus * rk_ref[:, lo:hi], ones_bd)
            bonus_ref[0, :, lo:hi] = (s * v).astype(bonus_ref.dtype)
            g_ref[0, :, lo:hi] = _dot(lg, wg_ref[:, lo:hi]).astype(g_ref.dtype)


def _state_terms(zm, zl, p, grid_mode, with_read, tm):
    b, t, _ = zm.shape
    width = p["k_k"].shape[1]
    nt = t // tm
    row = lambda bi, i: (bi, i, 0)
    const = lambda bi, i: (0, 0)
    ins, in_specs = [], []
    if grid_mode:
        hb = tm // GRID_W
        last = t // GRID_W - 1
        prev = lambda bi, i: (bi, jnp.maximum(i * hb - 1, 0), 0)
        nxt = lambda bi, i: (bi, jnp.minimum((i + 1) * hb, last), 0)
        for z in (zm, zl):
            ins += [z, z, z]
            in_specs += [pl.BlockSpec((1, tm, z.shape[2]), row),
                         pl.BlockSpec((1, GRID_W, z.shape[2]), prev),
                         pl.BlockSpec((1, GRID_W, z.shape[2]), nxt)]
    else:
        for z in (zm, zl):
            ins.append(z)
            in_specs.append(pl.BlockSpec((1, tm, z.shape[2]), row))
    names = ["mu_main", "mu_lora", "w0", "w_dec", "a0", "w_iclr", "k_k", "k_a"]
    if with_read:
        names += ["r_k", "w_gate"]
    for n in names:
        ins.append(p[n])
        in_specs.append(pl.BlockSpec(p[n].shape, const))
    out_dt = [BF16, BF16, F32, F32, BF16, BF16, BF16, BF16] + ([BF16] * 3 if with_read else [])
    return pl.pallas_call(
        functools.partial(_state_kernel, grid_mode=grid_mode, with_read=with_read, width=width),
        grid=(b, nt),
        in_specs=in_specs,
        out_specs=[pl.BlockSpec((1, tm, width), row) for _ in out_dt],
        out_shape=[jax.ShapeDtypeStruct((b, t, width), dt) for dt in out_dt],
        compiler_params=_cparams(("arbitrary", "arbitrary")),
        name="state_terms_grid" if grid_mode else "state_terms_seq",
    )(*ins)


def _pack_bd(y, bd_mask):
    reps = MXU_WIDTH // y.shape[0]
    return jnp.where(bd_mask, jnp.concatenate([y] * reps, axis=0), jnp.zeros((), y.dtype))


def _scan_chunk(insts, with_y):
    c = CHUNK
    bd_mask = _block_diag_mask(MXU_WIDTH, HEAD_DIM)
    ti = lax.broadcasted_iota(jnp.int32, (c, c), 0)
    tj = lax.broadcasted_iota(jnp.int32, (c, c), 1)
    t4 = lax.broadcasted_iota(jnp.int32, (c, MXU_WIDTH), 0)
    i4 = lax.broadcasted_iota(jnp.int32, (c, MXU_WIDTH), 1) % c
    eye4 = (t4 == i4).astype(F32)
    tri = ((tj <= ti).astype(BF16), (tj >= ti).astype(BF16))
    strict4 = (i4 < t4, i4 > t4)
    incl4 = (i4 <= t4, i4 >= t4)
    pack = lambda y: _pack_bd(y, bd_mask)

    for it in insts:
        rev = it["reverse"]
        ld = it["ld"]
        cl = sum(_dot(tri[rev], t) for t in _split3(ld))
        cl_end = cl[0:1] if rev else cl[c - 1:c]
        rel = cl - cl_end
        e_k = jnp.exp(-rel)
        it["kt"] = (it["kd"] * e_k).astype(BF16)
        it["bt"] = (it["be"] * e_k).astype(BF16)
        it["vb"] = it["v"].astype(BF16)
        kq = (it["kap"] * jnp.exp(rel - ld)).astype(BF16)
        it["lhs"] = jnp.concatenate([kq, (it["r"] * jnp.exp(rel)).astype(BF16)], axis=0) if with_y else kq
        it["sd"] = it["s"] * jnp.exp(cl_end)
    for it in insts:
        it["st"] = _dot_nt(it["lhs"], it["sd"].astype(BF16))
        it["a_b"] = _dot_nt(it["lhs"], pack(it["bt"]))
        it["a_k"] = _dot_nt(it["lhs"], pack(it["kt"]))

    for it in insts:
        a = jnp.where(strict4[it["reverse"]], it["a_b"][:c], 0.0)
        it["t_inv"] = eye4 - a
        ab = a.astype(BF16)
        it["pw"] = _dot(ab, pack(ab)).astype(BF16)
    n = 2
    while 2 * n < c:
        for it in insts:
            both = _dot(jnp.concatenate([it["pw"], it["t_inv"].astype(BF16)], axis=0), pack(it["pw"]))
            it["pw"] = both[:c].astype(BF16)
            it["t_inv"] = it["t_inv"] + both[c:]
        n *= 2
    for it in insts:
        it["t_inv"] = it["t_inv"] + _dot(it["t_inv"].astype(BF16), pack(it["pw"]))

    for it in insts:
        rev = it["reverse"]
        a_kk = jnp.where(strict4[rev], it["a_k"][:c], 0.0)
        if with_y:
            a_rk = jnp.where(incl4[rev], it["a_k"][c:], 0.0)
            it["av"] = _dot(jnp.concatenate([a_kk, a_rk], axis=0).astype(BF16), pack(it["vb"]))
        else:
            it["av"] = _dot(a_kk.astype(BF16), pack(it["vb"]))
    for it in insts:
        rhs = it["st"][:c] + it["av"][:c]
        it["ub"] = _dot(it["t_inv"].astype(BF16), pack(rhs.astype(BF16))).astype(BF16)
    out = []
    for it in insts:
        y = None
        if with_y:
            a_rb = jnp.where(incl4[it["reverse"]], it["a_b"][c:], 0.0)
            y = it["st"][c:] + it["av"][c:] - _dot(a_rb.astype(BF16), pack(it["ub"]))
        delta = _dot_tn(jnp.concatenate([it["vb"], it["ub"]], axis=0),
                        jnp.concatenate([it["kt"], -it["bt"]], axis=0))
        out.append((it["sd"] + jnp.where(bd_mask, delta, 0.0), y))
    return out


def _scan_kernel(*refs, with_y, width):
    n_in = 12 if with_y else 10
    in_refs, refs = refs[:n_in], refs[n_in:]
    s0_ref, refs = refs[0], refs[1:]
    if with_y:
        y_refs, refs = refs[:2], refs[2:]
    sfin_ref, s_scr = refs
    c_idx = pl.program_id(1)
    n_groups = width // MXU_WIDTH

    @pl.when(c_idx == 0)
    def _():
        s_scr[...] = s0_ref[0]

    per_dir = 6 if with_y else 5
    names = ("ld", "kap", "kd", "be", "v", "r")[:per_dir]
    n_sub = in_refs[0].shape[1] // CHUNK

    def sub_chunk(j, carry):
        insts = []
        for d in range(2):
            rows = pl.ds(pl.multiple_of((j if d == 0 else n_sub - 1 - j) * CHUNK, CHUNK), CHUNK)
            d_refs = in_refs[d * per_dir:(d + 1) * per_dir]
            for g in range(n_groups):
                lo, hi = g * MXU_WIDTH, (g + 1) * MXU_WIDTH
                it = {n: ref[0, rows, lo:hi].astype(F32) for n, ref in zip(names, d_refs)}
                it.update(reverse=d, s=s_scr[d, g], d=d, g=g, rows=rows)
                insts.append(it)
        for it, (s_new, y) in zip(insts, _scan_chunk(insts, with_y)):
            s_scr[it["d"], it["g"]] = s_new
            if with_y:
                lo = it["g"] * MXU_WIDTH
                y_refs[it["d"]][0, it["rows"], lo:lo + MXU_WIDTH] = y
        return carry

    lax.fori_loop(0, n_sub, sub_chunk, 0)

    @pl.when(c_idx == pl.num_programs(1) - 1)
    def _():
        sfin_ref[0] = s_scr[...]


def _wkv_scan(dirs, kap, v, r, s0):
    b, t, width = kap.shape
    rows = SCAN_SUB * CHUNK if t % (SCAN_SUB * CHUNK) == 0 else CHUNK
    nc = t // rows
    with_y = r is not None
    fwd = lambda bi, ci: (bi, ci, 0)
    bwd = lambda bi, ci: (bi, nc - 1 - ci, 0)
    ins, in_specs = [], []
    for d, imap in enumerate((fwd, bwd)):
        ld, kd, be = dirs[d]
        for arr in (ld, kap, kd, be, v) + ((r,) if with_y else ()):
            ins.append(arr)
            in_specs.append(pl.BlockSpec((1, rows, width), imap))
    n_groups = width // MXU_WIDTH
    s_shape = (2, n_groups, MXU_WIDTH, MXU_WIDTH)
    s_spec = pl.BlockSpec((1,) + s_shape, lambda bi, ci: (bi, 0, 0, 0, 0))
    ins.append(s0)
    in_specs.append(s_spec)
    out_specs, out_shape = [], []
    if with_y:
        out_specs += [pl.BlockSpec((1, rows, width), fwd), pl.BlockSpec((1, rows, width), bwd)]
        out_shape += [jax.ShapeDtypeStruct((b, t, width), F32)] * 2
    out_specs.append(s_spec)
    out_shape.append(jax.ShapeDtypeStruct((b,) + s_shape, F32))
    return pl.pallas_call(
        functools.partial(_scan_kernel, with_y=with_y, width=width),
        grid=(b, nc),
        in_specs=in_specs,
        out_specs=out_specs,
        out_shape=out_shape,
        scratch_shapes=[pltpu.VMEM(s_shape, F32)],
        compiler_params=_cparams(("arbitrary", "arbitrary")),
        name="wkv_scan" if with_y else "wkv_scan_ctx",
    )(*ins)


def _merge_kernel(yf_ref, yb_ref, bonus_ref, g_ref, zb_ref, p_ref, pp_ref, pn_ref, zg_ref, x_ref,
                  lnw_ref, lnb_ref, cw_ref, wout_ref, npost_ref, g1_ref, npre_ref, sc2_ref, sh2_ref,
                  wr_ref, br_ref,
                  x2_ref, h2_ref, gate_ref, idx_ref, rank_ref, cnt_ref, run_scr, *, width):
    first = jnp.logical_and(pl.program_id(0) == 0, pl.program_id(1) == 0)
    i = pl.program_id(1)
    tm = x_ref.shape[1]

    @pl.when(first)
    def _():
        run_scr[...] = jnp.zeros_like(run_scr)

    ones_bd = _block_diag_mask(MXU_WIDTH, HEAD_DIM).astype(BF16)
    row = lax.broadcasted_iota(jnp.int32, (tm, MXU_WIDTH), 0)
    halo = pp_ref.shape[1]
    parts = []
    for lo in range(0, width, MXU_WIDTH):
        hi = lo + MXU_WIDTH
        y = yf_ref[0, :, lo:hi] + yb_ref[0, :, lo:hi]
        mean = _head_sum(y, ones_bd) * (1.0 / HEAD_DIM)
        yc = y - mean
        var = _head_sum(yc * yc, ones_bd) * (1.0 / HEAD_DIM)
        yn = yc * lax.rsqrt(var + GN_EPS) * lnw_ref[:, lo:hi] + lnb_ref[:, lo:hi]
        y_rwkv = (yn + bonus_ref[0, :, lo:hi].astype(F32)) * g_ref[0, :, lo:hi].astype(F32)
        p = p_ref[0, :, lo:hi].astype(F32)
        p_prev = jnp.where(i == 0, 0.0, pp_ref[0, halo - 1:halo, lo:hi].astype(F32))
        p_next = jnp.where(i == pl.num_programs(1) - 1, 0.0, pn_ref[0, 0:1, lo:hi].astype(F32))
        before = jnp.where(row == 0, p_prev, pltpu.roll(p, 1, axis=0))
        after = jnp.where(row == tm - 1, p_next, pltpu.roll(p, tm - 1, axis=0))
        conv = cw_ref[0:1, lo:hi] * before + cw_ref[1:2, lo:hi] * p + cw_ref[2:3, lo:hi] * after
        y_conv = zb_ref[0, :, lo:hi].astype(F32) * conv
        ga = _sigmoid(zg_ref[0, :, lo:hi].astype(F32))
        gb = _sigmoid(zg_ref[0, :, width + lo:width + hi].astype(F32))
        parts.append((ga * y_rwkv + gb * y_conv).astype(BF16))
    merged = jnp.concatenate(parts, axis=1)
    mix = _dot(merged, wout_ref[...])
    x2 = x_ref[0] + g1_ref[0] * _rms(mix, npost_ref[...])
    x2_ref[0] = x2
    h2 = _rms(x2, npre_ref[...]) * (1.0 + sc2_ref[0]) + sh2_ref[0]
    h2_ref[0] = h2.astype(h2_ref.dtype)

    logits = _dot(h2, wr_ref[...]) + br_ref[...]
    lane = lax.broadcasted_iota(jnp.int32, logits.shape, 1)
    vals, idxs = [], []
    sel = jnp.zeros(logits.shape, F32)
    work = logits
    for _ in range(TOP_K):
        m = jnp.max(work, axis=-1, keepdims=True)
        idx = jnp.min(jnp.where(work == m, lane, logits.shape[1]), axis=-1, keepdims=True)
        hit = lane == idx
        vals.append(m)
        idxs.append(idx)
        sel = jnp.where(hit, 1.0, sel)
        work = jnp.where(hit, -jnp.inf, work)
    exps = [jnp.exp(vk - vals[0]) for vk in vals]
    denom = exps[0] + exps[1] + exps[2] + exps[3]
    r_i = lax.broadcasted_iota(jnp.int32, (tm, tm), 0)
    c_i = lax.broadcasted_iota(jnp.int32, (tm, tm), 1)
    before_cnt = _dot((c_i < r_i).astype(BF16), sel.astype(BF16)) + run_scr[...]
    gate_out = jnp.zeros(logits.shape, F32)
    idx_out = jnp.zeros(logits.shape, jnp.int32)
    rank_out = jnp.zeros(logits.shape, jnp.int32)
    for k in range(TOP_K):
        rank_k = jnp.sum(jnp.where(lane == idxs[k], before_cnt, 0.0), axis=-1, keepdims=True)
        gate_out = jnp.where(lane == k, exps[k] / denom, gate_out)
        idx_out = jnp.where(lane == k, idxs[k], idx_out)
        rank_out = jnp.where(lane == k, rank_k.astype(jnp.int32), rank_out)
    gate_ref[0] = gate_out
    idx_ref[0] = idx_out
    rank_ref[0] = rank_out
    run_scr[...] = run_scr[...] + jnp.sum(sel, axis=0, keepdims=True)
    cnt_ref[...] = run_scr[...]


def _merge_route(yf, yb, bonus, g, zb, p, zg, x, pr, tm):
    b, t, d = x.shape
    width = yf.shape[2]
    halo = 8
    hb = tm // halo
    last = t // halo - 1
    row = lambda bi, i: (bi, i, 0)
    const = lambda bi, i: (0, 0)
    per_b = lambda bi, i: (bi, 0, 0)
    prev = lambda bi, i: (bi, jnp.maximum(i * hb - 1, 0), 0)
    nxt = lambda bi, i: (bi, jnp.minimum((i + 1) * hb, last), 0)
    lanes = 128
    in_specs = [pl.BlockSpec((1, tm, width), row)] * 6
    in_specs += [pl.BlockSpec((1, halo, width), prev), pl.BlockSpec((1, halo, width), nxt),
                 pl.BlockSpec((1, tm, 2 * width), row), pl.BlockSpec((1, tm, d), row)]
    params = [pr["ln_x_w"], pr["ln_x_b"], pr["conv_w"], pr["w_out"], pr["norm_post_mix"]]
    in_specs += [pl.BlockSpec(a.shape, const) for a in params]
    in_specs.append(pl.BlockSpec((1, 1, d), per_b))
    in_specs.append(pl.BlockSpec(pr["norm_pre_ffn"].shape, const))
    in_specs += [pl.BlockSpec((1, 1, d), per_b)] * 2
    in_specs += [pl.BlockSpec(pr["w_router"].shape, const), pl.BlockSpec(pr["b_router"].shape, const)]
    out_specs = [pl.BlockSpec((1, tm, d), row), pl.BlockSpec((1, tm, d), row)]
    out_specs += [pl.BlockSpec((1, tm, lanes), row)] * 3
    out_specs.append(pl.BlockSpec((1, lanes), const))
    out_shape = [jax.ShapeDtypeStruct((b, t, d), F32), jax.ShapeDtypeStruct((b, t, d), F32),
                 jax.ShapeDtypeStruct((b, t, lanes), F32), jax.ShapeDtypeStruct((b, t, lanes), jnp.int32),
                 jax.ShapeDtypeStruct((b, t, lanes), jnp.int32), jax.ShapeDtypeStruct((1, lanes), F32)]
    return pl.pallas_call(
        functools.partial(_merge_kernel, width=width),
        grid=(b, t // tm),
        in_specs=in_specs,
        out_specs=out_specs,
        out_shape=out_shape,
        scratch_shapes=[pltpu.VMEM((1, lanes), F32)],
        compiler_params=_cparams(("arbitrary", "arbitrary")),
        name="merge_route",
    )(yf, yb, bonus, g, zb, p, p, p, zg, x, *params, pr["g1"], pr["norm_pre_ffn"], pr["sc2"], pr["sh2"],
      pr["w_router"], pr["b_router"])


def _row_copy(src_ref, src_row, dst_ref, dst_row, sem):
    return pltpu.make_async_copy(src_ref.at[pl.ds(src_row, 1)], dst_ref.at[pl.ds(dst_row, 1)], sem)


def _dispatch_kernel(dest_ref, nrows_ref, h2_ref, xb_ref, zrow, sem, zsem, *, tm, n_pad):
    base = pl.program_id(0) * (tm * TOP_K)

    @pl.when(pl.program_id(0) == 0)
    def _():
        zrow[...] = jnp.zeros_like(zrow)

        def per_block(j, carry):
            def fill(r, c2):
                _row_copy(zrow, 0, xb_ref, j * MOE_BLOCK + r, zsem).start()
                return c2
            return lax.fori_loop(nrows_ref[j], MOE_BLOCK, fill, carry)

        lax.fori_loop(0, nrows_ref.shape[0], per_block, 0)
        pltpu.make_async_copy(xb_ref.at[pl.ds(0, n_pad)], xb_ref.at[pl.ds(0, n_pad)], zsem).wait()

    def issue(r, carry):
        for k in range(TOP_K):
            _row_copy(h2_ref, r, xb_ref, dest_ref[base + r * TOP_K + k], sem).start(priority=k % 2)
        return carry

    lax.fori_loop(0, tm, issue, 0, unroll=8)
    for _ in range(TOP_K):
        pltpu.make_async_copy(h2_ref, xb_ref.at[pl.ds(0, tm)], sem).wait()


def _dispatch(h2, dest, block_rows, tm):
    n_tok, d = h2.shape
    n_rows = block_rows.shape[0] * MOE_BLOCK
    grid_spec = pltpu.PrefetchScalarGridSpec(
        num_scalar_prefetch=2,
        grid=(n_tok // tm,),
        in_specs=[pl.BlockSpec((tm, d), lambda i, dest, nr: (i, 0))],
        out_specs=pl.BlockSpec(memory_space=pl.ANY),
        scratch_shapes=[pltpu.VMEM((8, d), h2.dtype), pltpu.SemaphoreType.DMA, pltpu.SemaphoreType.DMA],
    )
    return pl.pallas_call(
        functools.partial(_dispatch_kernel, tm=tm, n_pad=n_rows - dest.shape[0]),
        grid_spec=grid_spec,
        out_shape=jax.ShapeDtypeStruct((n_rows, d), h2.dtype),
        compiler_params=_cparams(("arbitrary",)),
        name="moe_dispatch",
    )(dest, block_rows, h2)


def _expert_kernel(be_ref, nrows_ref, xb_ref, win_ref, bin_ref, wout_ref, bout_ref, o_ref, win_scr, wout_scr,
                   *, d_ff):
    i = pl.program_id(0)
    changed = jnp.logical_or(i == 0, be_ref[i] != be_ref[jnp.maximum(i - 1, 0)])
    n_rows = nrows_ref[i]

    @pl.when(jnp.logical_and(changed, n_rows > 0))
    def _():
        win_scr[...] = win_ref[0].astype(BF16)
        wout_scr[...] = wout_ref[0].astype(BF16)

    @pl.when(n_rows > 0)
    def _():
        row = lax.broadcasted_iota(jnp.int32, xb_ref.shape, 0)
        xb = jnp.where(row < n_rows, xb_ref[...], 0.0).astype(BF16)
        gu = _dot(xb, win_scr[...]) + bin_ref[0]
        gate = jnp.minimum(gu[:, :d_ff], SWIGLU_LIMIT)
        up = jnp.clip(gu[:, d_ff:], -SWIGLU_LIMIT, SWIGLU_LIMIT)
        act = (up + 1.0) * gate * _sigmoid(SWIGLU_ALPHA * gate)
        o_ref[...] = _dot(act.astype(BF16), wout_scr[...]) + bout_ref[0]

    @pl.when(n_rows == 0)
    def _():
        o_ref[...] = jnp.zeros_like(o_ref)


def _experts(xb, block_e, block_rows, w_in, b_in, w_out, b_out):
    n_rows, d = xb.shape
    n_e, _, d_ff2 = w_in.shape
    d_ff = d_ff2 // 2
    n_blocks = n_rows // MOE_BLOCK
    grid_spec = pltpu.PrefetchScalarGridSpec(
        num_scalar_prefetch=2,
        grid=(n_blocks,),
        in_specs=[pl.BlockSpec((MOE_BLOCK, d), lambda i, be, nr: (i, 0)),
                  pl.BlockSpec((1, d, d_ff2), lambda i, be, nr: (be[i], 0, 0)),
                  pl.BlockSpec((1, 1, d_ff2), lambda i, be, nr: (be[i], 0, 0)),
                  pl.BlockSpec((1, d_ff, d), lambda i, be, nr: (be[i], 0, 0)),
                  pl.BlockSpec((1, 1, d), lambda i, be, nr: (be[i], 0, 0))],
        out_specs=pl.BlockSpec((MOE_BLOCK, d), lambda i, be, nr: (i, 0)),
        scratch_shapes=[pltpu.VMEM((d, d_ff2), BF16), pltpu.VMEM((d_ff, d), BF16)],
    )
    return pl.pallas_call(
        functools.partial(_expert_kernel, d_ff=d_ff),
        grid_spec=grid_spec,
        out_shape=jax.ShapeDtypeStruct((n_rows, d), F32),
        compiler_params=_cparams(("arbitrary",)),
        name="moe_experts",
    )(block_e, block_rows, xb, w_in, b_in.reshape(n_e, 1, d_ff2), w_out, b_out.reshape(n_e, 1, d))


def _combine_kernel(dest_ref, yb_ref, gate_ref, x2_ref, npost_ref, g2_ref, o_ref, ybuf, sem, *, tm):
    tile = pl.program_id(0) * pl.num_programs(1) + pl.program_id(1)
    base = tile * (tm * TOP_K)

    def issue(r, carry):
        for k in range(TOP_K):
            _row_copy(yb_ref, dest_ref[base + r * TOP_K + k], ybuf.at[k], r, sem).start(priority=k % 2)
        return carry

    lax.fori_loop(0, tm, issue, 0, unroll=8)
    for k in range(TOP_K):
        pltpu.make_async_copy(yb_ref.at[pl.ds(0, tm)], ybuf.at[k], sem).wait()
    gates = gate_ref[0]
    f = None
    for k in range(TOP_K):
        term = gates[:, k:k + 1] * ybuf[k]
        f = term if f is None else f + term
    o_ref[0] = x2_ref[0] + g2_ref[0] * _rms(f, npost_ref[...])


def _combine(yb_rows, dest, gates, x2, npost, g2, tm):
    b, t, d = x2.shape
    row = lambda bi, i, dest: (bi, i, 0)
    grid_spec = pltpu.PrefetchScalarGridSpec(
        num_scalar_prefetch=1,
        grid=(b, t // tm),
        in_specs=[pl.BlockSpec(memory_space=pl.ANY),
                  pl.BlockSpec((1, tm, gates.shape[2]), row),
                  pl.BlockSpec((1, tm, d), row),
                  pl.BlockSpec((1, d), lambda bi, i, dest: (0, 0)),
                  pl.BlockSpec((1, 1, d), lambda bi, i, dest: (bi, 0, 0))],
        out_specs=pl.BlockSpec((1, tm, d), row),
        scratch_shapes=[pltpu.VMEM((TOP_K, tm, d), F32), pltpu.SemaphoreType.DMA],
    )
    return pl.pallas_call(
        functools.partial(_combine_kernel, tm=tm),
        grid_spec=grid_spec,
        out_shape=jax.ShapeDtypeStruct((b, t, d), F32),
        compiler_params=_cparams(("arbitrary", "arbitrary")),
        name="moe_combine",
    )(dest, yb_rows, gates, x2, npost, g2)


def _lora_up(w_up):
    _, r, w = w_up.shape
    z = jnp.zeros((r, w), w_up.dtype)
    return jnp.concatenate([jnp.concatenate([w_up[0], z], axis=1),
                            jnp.concatenate([z, w_up[1]], axis=1)], axis=0).astype(BF16)


def _row_tile(t, pref):
    return pref if t % pref == 0 else t


def kernel(x, c, ctx, c_ctx, w_ada, b_ada, norm_pre_mix, norm_post_mix, norm_pre_ffn, norm_post_ffn,
           w_in, mu_shift, w0, w_decay_up, a0, w_iclr_up, k_k, k_a, r_k, w_gate_up, ln_x_w, ln_x_b,
           conv_w, w_out, w_router, b_router, w_exp_in, b_exp_in, w_exp_out, b_exp_out):
    b, t, d = x.shape
    t_ctx = ctx.shape[1]
    depth = w_ada.shape[0]
    width = k_k.shape[1]
    n_groups = width // MXU_WIDTH
    k_off, v_off = 0, width
    decay_off = 2 * width
    iclr_off = decay_off + 2 * DECAY_LORA
    r_off = iclr_off + 2 * ICLR_LORA
    glora_off = r_off + width
    conv_off = glora_off + GATE_LORA
    gate_off = conv_off + 3 * width

    xc = ctx
    for l in range(depth):
        last = l == depth - 1
        rows = jnp.concatenate([c, c_ctx[None, :], jnp.zeros((8 - b - 1, d), F32)], axis=0)
        mod = _ada(rows, w_ada[l], b_ada[l])
        sh1, sc1, g1, sh2, sc2, g2 = [mod[:b, None, j * d:(j + 1) * d] for j in range(6)]
        csh1, csc1, cg1, csh2, csc2, cg2 = [jnp.broadcast_to(mod[b:b + 1, None, j * d:(j + 1) * d], (b, 1, d))
                                            for j in range(6)]

        wl = w_in[l]
        cols = lambda lo, n: wl[:, lo:lo + n]
        w_main = jnp.concatenate([cols(k_off, width), cols(v_off, width), cols(r_off, width)], axis=1).astype(BF16)
        w_lora = jnp.concatenate([cols(decay_off, 2 * DECAY_LORA), cols(iclr_off, 2 * ICLR_LORA),
                                  cols(glora_off, GATE_LORA)], axis=1).astype(BF16)
        w_conv = tuple(cols(conv_off + j * width, width).astype(BF16) for j in range(3))
        w_gate = cols(gate_off, 2 * d).astype(BF16)
        mu = mu_shift[l]
        mu_main = jnp.concatenate([mu[k_off:k_off + width], mu[v_off:v_off + width], mu[r_off:r_off + width]])[None, :]
        mu_lora = jnp.concatenate([mu[decay_off:decay_off + 2 * DECAY_LORA], mu[iclr_off:iclr_off + 2 * ICLR_LORA],
                                   mu[glora_off:glora_off + GATE_LORA]])[None, :]
        sp = {
            "mu_main": mu_main, "mu_lora": mu_lora,
            "w0": w0[l].reshape(1, 2 * width), "w_dec": _lora_up(w_decay_up[l]),
            "a0": a0[l].reshape(1, 2 * width), "w_iclr": _lora_up(w_iclr_up[l]),
            "k_k": k_k[l][None, :], "k_a": k_a[l][None, :],
            "r_k": r_k[l].reshape(1, width), "w_gate": w_gate_up[l].astype(BF16),
        }
        if not last:
            raise NotImplementedError("context stream update for non-final layers")

        n_state_main = 2 * width
        zc_main, zc_lora = _inproj(xc, norm_pre_mix[l], csc1, csh1,
                                   [w_main[:, :n_state_main], w_lora[:, :2 * DECAY_LORA + 2 * ICLR_LORA]],
                                   None, _row_tile(t_ctx, 256))
        sp_ctx = dict(sp, mu_main=mu_main[:, :n_state_main], mu_lora=mu_lora[:, :2 * DECAY_LORA + 2 * ICLR_LORA])
        kap_c, v_c, ld0_c, ld1_c, kd0_c, kd1_c, be0_c, be1_c = _state_terms(
            zc_main, zc_lora, sp_ctx, grid_mode=False, with_read=False, tm=t_ctx)
        s0 = jnp.zeros((b, 2, n_groups, MXU_WIDTH, MXU_WIDTH), F32)
        (s_ctx,) = _wkv_scan([(ld0_c, kd0_c, be0_c), (ld1_c, kd1_c, be1_c)], kap_c, v_c, None, s0)

        zm, zl, zg, zb, p = _inproj(x, norm_pre_mix[l], sc1, sh1, [w_main, w_lora, w_gate], w_conv,
                                    _row_tile(t, 512))
        kap, v, ld0, ld1, kd0, kd1, be0, be1, r, bonus, g = _state_terms(
            zm, zl, sp, grid_mode=True, with_read=True, tm=_row_tile(t, 512))
        yf, yb, _ = _wkv_scan([(ld0, kd0, be0), (ld1, kd1, be1)], kap, v, r, s_ctx)

        lanes = 128
        pad_e = lanes - N_EXPERTS
        pr = {
            "ln_x_w": ln_x_w[l][None, :], "ln_x_b": ln_x_b[l][None, :], "conv_w": conv_w[l],
            "w_out": w_out[l].astype(BF16), "norm_post_mix": norm_post_mix[l][None, :], "g1": g1,
            "norm_pre_ffn": norm_pre_ffn[l][None, :], "sc2": sc2, "sh2": sh2,
            "w_router": jnp.pad(w_router[l], ((0, 0), (0, pad_e))),
            "b_router": jnp.pad(b_router[l], (0, pad_e), constant_values=-jnp.inf)[None, :],
        }
        x2, h2, gates, top_idx, rank, counts = _merge_route(yf, yb, bonus, g, zb, p, zg, x, pr, _row_tile(t, 256))

        n_tok = b * t
        nk = n_tok * TOP_K
        counts = counts[0, :N_EXPERTS].astype(jnp.int32)
        padded = (counts + MOE_BLOCK - 1) // MOE_BLOCK * MOE_BLOCK
        pad_ends = jnp.cumsum(padded)
        pad_starts = pad_ends - padded
        experts = jnp.arange(N_EXPERTS, dtype=jnp.int32)
        e_flat = top_idx[..., :TOP_K].reshape(nk)
        dest = jnp.sum(jnp.where(e_flat[:, None] == experts[None, :], pad_starts[None, :], 0), axis=1) \
            + rank[..., :TOP_K].reshape(nk)
        n_blocks = -(-(nk + N_EXPERTS * (MOE_BLOCK - 1)) // MOE_BLOCK)
        n_rows = n_blocks * MOE_BLOCK
        blk_start = jnp.arange(n_blocks, dtype=jnp.int32) * MOE_BLOCK
        block_e = jnp.minimum(jnp.sum((pad_ends[None, :] <= blk_start[:, None]).astype(jnp.int32), axis=1),
                              N_EXPERTS - 1)
        own = block_e[:, None] == experts[None, :]
        seg_end = jnp.sum(jnp.where(own, (pad_starts + counts)[None, :], 0), axis=1)
        block_rows = jnp.clip(seg_end - blk_start, 0, MOE_BLOCK).astype(jnp.int32)
        xb = _dispatch(h2.reshape(n_tok, d), dest, block_rows, _row_tile(n_tok, 256))
        yb_rows = _experts(xb, block_e, block_rows, w_exp_in[l], b_exp_in[l], w_exp_out[l], b_exp_out[l])
        x = _combine(yb_rows, dest, gates, x2, norm_post_ffn[l][None, :], g2, _row_tile(t, 256))
    return x
```

```python
import functools

import jax
import jax.numpy as jnp
from jax import lax
from jax.experimental import pallas as pl
from jax.experimental.pallas import tpu as pltpu

F32 = jnp.float32
BF16 = jnp.bfloat16

HEAD_DIM = 64
GRID_W = 64
DECAY_LORA = 64
ICLR_LORA = 64
GATE_LORA = 128
GN_EPS = 64e-5
NORM_EPS = 1e-12
CONV_K = 3
N_EXPERTS = 32
TOP_K = 4
SWIGLU_LIMIT = 7.0
SWIGLU_ALPHA = 1.702
MOE_BLOCK = 256
RMS_EPS = 1e-6

MXU_WIDTH = 256
HEADS_PER_GROUP = MXU_WIDTH // HEAD_DIM
CHUNK = 64
SCAN_SUB = 4
DECAY_SCALE = 0.6065306597126334
VMEM_LIMIT = 56 * 1024 * 1024


def _cparams(sem):
    return pltpu.CompilerParams(dimension_semantics=sem, vmem_limit_bytes=VMEM_LIMIT)


def _dot(a, b):
    return jnp.dot(a, b, preferred_element_type=F32)


def _dot_nt(a, b):
    return lax.dot_general(a, b, (((1,), (1,)), ((), ())), preferred_element_type=F32)


def _dot_tn(a, b):
    return lax.dot_general(a, b, (((0,), (0,)), ((), ())), preferred_element_type=F32)


def _split3(x):
    h = x.astype(BF16)
    r = x - h.astype(F32)
    m = r.astype(BF16)
    l = (r - m.astype(F32)).astype(BF16)
    return h, m, l


def _block_diag_mask(n, blk):
    r = lax.broadcasted_iota(jnp.int32, (n, n), 0) // blk
    c = lax.broadcasted_iota(jnp.int32, (n, n), 1) // blk
    return r == c


def _head_sum(x, ones_bd):
    h = x.astype(BF16)
    l = (x - h.astype(F32)).astype(BF16)
    return _dot(h, ones_bd) + _dot(l, ones_bd)


def _rms(x, gain):
    return x * lax.rsqrt(jnp.mean(x * x, axis=-1, keepdims=True) + RMS_EPS) * gain


def _sigmoid(x):
    return 1.0 / (1.0 + jnp.exp(-x))


def _ada_kernel(c_ref, w_ref, b_ref, o_ref):
    c = c_ref[...]
    s = c * _sigmoid(c)
    o_ref[...] = _dot(s, w_ref[...]) + b_ref[...]


def _ada(c_rows, w, b):
    rows, d = c_rows.shape
    n = w.shape[1]
    tn = 1024
    return pl.pallas_call(
        _ada_kernel,
        grid=(n // tn,),
        in_specs=[pl.BlockSpec((rows, d), lambda j: (0, 0)),
                  pl.BlockSpec((d, tn), lambda j: (0, j)),
                  pl.BlockSpec((1, tn), lambda j: (0, j))],
        out_specs=pl.BlockSpec((rows, tn), lambda j: (0, j)),
        out_shape=jax.ShapeDtypeStruct((rows, n), F32),
        compiler_params=_cparams(("arbitrary",)),
        name="ada_mod",
    )(c_rows, w, b.reshape(1, n))


def _inproj_kernel(x_ref, g_ref, sc_ref, sh_ref, *refs, n_plain, with_conv):
    n_w = n_plain + (3 if with_conv else 0)
    w_refs, o_refs = refs[:n_w], refs[n_w:]
    x = x_ref[0]
    h = _rms(x, g_ref[...]) * (1.0 + sc_ref[0]) + sh_ref[0]
    hb = h.astype(BF16)
    col = 512
    for w_ref, o_ref in zip(w_refs[:n_plain], o_refs[:n_plain]):
        n = w_ref.shape[1]
        for j in range(0, n, col):
            e = min(j + col, n)
            o_ref[0, :, j:e] = _dot(hb, w_ref[:, j:e]).astype(o_ref.dtype)
    if with_conv:
        wb_ref, wc_ref, wx_ref = w_refs[n_plain:]
        ob_ref, op_ref = o_refs[n_plain:]
        n = wb_ref.shape[1]
        for j in range(0, n, col):
            e = min(j + col, n)
            ob_ref[0, :, j:e] = _dot(hb, wb_ref[:, j:e]).astype(ob_ref.dtype)
            op_ref[0, :, j:e] = (_dot(hb, wc_ref[:, j:e]) * _dot(hb, wx_ref[:, j:e])).astype(op_ref.dtype)


def _inproj(x, gain, sc, sh, plain_ws, conv_ws, tm):
    b, t, d = x.shape
    with_conv = conv_ws is not None
    ws = list(plain_ws) + (list(conv_ws) if with_conv else [])
    out_w = [w.shape[1] for w in plain_ws] + ([conv_ws[0].shape[1]] * 2 if with_conv else [])
    row = lambda bi, i: (bi, i, 0)
    const = lambda bi, i: (0, 0)
    in_specs = [pl.BlockSpec((1, tm, d), row),
                pl.BlockSpec((1, d), const),
                pl.BlockSpec((1, 1, d), lambda bi, i: (bi, 0, 0)),
                pl.BlockSpec((1, 1, d), lambda bi, i: (bi, 0, 0))]
    in_specs += [pl.BlockSpec(w.shape, const, pipeline_mode=pl.Buffered(1)) for w in ws]
    return pl.pallas_call(
        functools.partial(_inproj_kernel, n_plain=len(plain_ws), with_conv=with_conv),
        grid=(b, t // tm),
        in_specs=in_specs,
        out_specs=[pl.BlockSpec((1, tm, n), row) for n in out_w],
        out_shape=[jax.ShapeDtypeStruct((b, t, n), BF16) for n in out_w],
        compiler_params=_cparams(("arbitrary", "arbitrary")),
        name="in_proj",
    )(x, gain.reshape(1, d), sc, sh, *ws)


def _shift_mix(cur, prev, nxt, mu, grid_mode, is_first, is_last):
    tm, w = cur.shape
    row = lax.broadcasted_iota(jnp.int32, (tm, w), 0)
    grp = lax.broadcasted_iota(jnp.int32, (tm, w), 1) % 4
    back = pltpu.roll(cur, 1, axis=0)
    fwd = pltpu.roll(cur, tm - 1, axis=0)
    if grid_mode:
        colpos = row % GRID_W
        left = jnp.where(colpos == 0, 0.0, back)
        right = jnp.where(colpos == GRID_W - 1, 0.0, fwd)
        prev = jnp.where(is_first, 0.0, prev)
        nxt = jnp.where(is_last, 0.0, nxt)
        up = jnp.concatenate([prev, cur[:tm - GRID_W]], axis=0)
        down = jnp.concatenate([cur[GRID_W:], nxt], axis=0)
        shifted = jnp.where(grp == 0, left, jnp.where(grp == 1, right, jnp.where(grp == 2, up, down)))
    else:
        before = jnp.where(row == 0, 0.0, back)
        after = jnp.where(row == tm - 1, 0.0, fwd)
        shifted = jnp.where(grp % 2 == 0, before, after)
    return cur + mu * (shifted - cur)


def _state_kernel(*refs, grid_mode, with_read, width):
    if grid_mode:
        (zm_ref, zmp_ref, zmn_ref, zl_ref, zlp_ref, zln_ref), refs = refs[:6], refs[6:]
    else:
        (zm_ref, zl_ref), refs = refs[:2], refs[2:]
        zmp_ref = zmn_ref = zlp_ref = zln_ref = None
    (mum_ref, mul_ref, w0_ref, wdec_ref, a0_ref, wic_ref, kk_ref, ka_ref), refs = refs[:8], refs[8:]
    if with_read:
        (rk_ref, wg_ref), refs = refs[:2], refs[2:]
    (kap_ref, v_ref, ld0_ref, ld1_ref, kd0_ref, kd1_ref, be0_ref, be1_ref), refs = refs[:8], refs[8:]
    if with_read:
        r_ref, bonus_ref, g_ref = refs

    i = pl.program_id(1)
    is_first = i == 0
    is_last = i == pl.num_programs(1) - 1
    ones_bd = _block_diag_mask(MXU_WIDTH, HEAD_DIM).astype(BF16)

    def shifted(z_ref, zp_ref, zn_ref, mu_ref, lo, hi):
        cur = z_ref[0, :, lo:hi].astype(F32)
        prev = zp_ref[0, :, lo:hi].astype(F32) if grid_mode else None
        nxt = zn_ref[0, :, lo:hi].astype(F32) if grid_mode else None
        return _shift_mix(cur, prev, nxt, mu_ref[:, lo:hi], grid_mode, is_first, is_last)

    lora = shifted(zl_ref, zlp_ref, zln_ref, mul_ref, 0, zl_ref.shape[2])
    lw = jnp.tanh(lora[:, :2 * DECAY_LORA]).astype(BF16)
    la = lora[:, 2 * DECAY_LORA:2 * DECAY_LORA + 2 * ICLR_LORA].astype(BF16)
    if with_read:
        lg = _sigmoid(lora[:, 2 * DECAY_LORA + 2 * ICLR_LORA:]).astype(BF16)

    for lo in range(0, width, MXU_WIDTH):
        hi = lo + MXU_WIDTH
        k = shifted(zm_ref, zmp_ref, zmn_ref, mum_ref, lo, hi)
        v = shifted(zm_ref, zmp_ref, zmn_ref, mum_ref, width + lo, width + hi)
        kk = k * kk_ref[:, lo:hi]
        n2 = _head_sum(kk * kk, ones_bd)
        kk = kk / jnp.maximum(jnp.sqrt(n2), NORM_EPS)
        kap_ref[0, :, lo:hi] = kk.astype(kap_ref.dtype)
        v_ref[0, :, lo:hi] = v.astype(v_ref.dtype)
        a_sum = None
        for d, (ld_ref, kd_ref, be_ref) in enumerate(((ld0_ref, kd0_ref, be0_ref), (ld1_ref, kd1_ref, be1_ref))):
            lo_d, hi_d = d * width + lo, d * width + hi
            pre_w = w0_ref[:, lo_d:hi_d] + _dot(lw, wdec_ref[:, lo_d:hi_d])
            ld_ref[0, :, lo:hi] = -DECAY_SCALE * _sigmoid(pre_w)
            a = _sigmoid(a0_ref[:, lo_d:hi_d] + _dot(la, wic_ref[:, lo_d:hi_d]))
            kd_ref[0, :, lo:hi] = (k * (1.0 + (a - 1.0) * ka_ref[:, lo:hi])).astype(kd_ref.dtype)
            be_ref[0, :, lo:hi] = (kk * a).astype(be_ref.dtype)
            a_sum = a if a_sum is None else a_sum + a
        if with_read:
            r = shifted(zm_ref, zmp_ref, zmn_ref, mum_ref, 2 * width + lo, 2 * width + hi)
            r_ref[0, :, lo:hi] = r.astype(r_ref.dtype)
            k_bonus = k * (1.0 + (0.5 * a_sum - 1.0) * ka_ref[:, lo:hi])
            s = _head_sum(r * k_bonus * rk_ref[:, lo:hi], ones_bd)
            bonus_ref[0, :, lo:hi] = (s * v).astype(bonus_ref.dtype)
            g_ref[0, :, lo:hi] = _dot(lg, wg_ref[:, lo:hi]).astype(g_ref.dtype)


def _state_terms(zm, zl, p, grid_mode, with_read, tm):
    b, t, _ = zm.shape
    width = p["k_k"].shape[1]
    nt = t // tm
    row = lambda bi, i: (bi, i, 0)
    const = lambda bi, i: (0, 0)
    ins, in_specs = [], []
    if grid_mode:
        hb = tm // GRID_W
        last = t // GRID_W - 1
        prev = lambda bi, i: (bi, jnp.maximum(i * hb - 1, 0), 0)
        nxt = lambda bi, i: (bi, jnp.minimum((i + 1) * hb, last), 0)
        for z in (zm, zl):
            ins += [z, z, z]
            in_specs += [pl.BlockSpec((1, tm, z.shape[2]), row),
                         pl.BlockSpec((1, GRID_W, z.shape[2]), prev),
                         pl.BlockSpec((1, GRID_W, z.shape[2]), nxt)]
    else:
        for z in (zm, zl):
            ins.append(z)
            in_specs.append(pl.BlockSpec((1, tm, z.shape[2]), row))
    names = ["mu_main", "mu_lora", "w0", "w_dec", "a0", "w_iclr", "k_k", "k_a"]
    if with_read:
        names += ["r_k", "w_gate"]
    for n in names:
        ins.append(p[n])
        in_specs.append(pl.BlockSpec(p[n].shape, const))
    out_dt = [BF16, BF16, F32, F32, BF16, BF16, BF16, BF16] + ([BF16] * 3 if with_read else [])
    return pl.pallas_call(
        functools.partial(_state_kernel, grid_mode=grid_mode, with_read=with_read, width=width),
        grid=(b, nt),
        in_specs=in_specs,
        out_specs=[pl.BlockSpec((1, tm, width), row) for _ in out_dt],
        out_shape=[jax.ShapeDtypeStruct((b, t, width), dt) for dt in out_dt],
        compiler_params=_cparams(("arbitrary", "arbitrary")),
        name="state_terms_grid" if grid_mode else "state_terms_seq",
    )(*ins)


def _pack_bd(y, bd_mask):
    reps = MXU_WIDTH // y.shape[0]
    return jnp.where(bd_mask, jnp.concatenate([y] * reps, axis=0), jnp.zeros((), y.dtype))


def _scan_chunk(insts, with_y):
    c = CHUNK
    bd_mask = _block_diag_mask(MXU_WIDTH, HEAD_DIM)
    ti = lax.broadcasted_iota(jnp.int32, (c, c), 0)
    tj = lax.broadcasted_iota(jnp.int32, (c, c), 1)
    t4 = lax.broadcasted_iota(jnp.int32, (c, MXU_WIDTH), 0)
    i4 = lax.broadcasted_iota(jnp.int32, (c, MXU_WIDTH), 1) % c
    eye4 = (t4 == i4).astype(F32)
    tri = ((tj <= ti).astype(BF16), (tj >= ti).astype(BF16))
    strict4 = (i4 < t4, i4 > t4)
    incl4 = (i4 <= t4, i4 >= t4)
    pack = lambda y: _pack_bd(y, bd_mask)

    for it in insts:
        rev = it["reverse"]
        ld = it["ld"]
        cl = sum(_dot(tri[rev], t) for t in _split3(ld))
        cl_end = cl[0:1] if rev else cl[c - 1:c]
        rel = cl - cl_end
        e_k = jnp.exp(-rel)
        it["kt"] = (it["kd"] * e_k).astype(BF16)
        it["bt"] = (it["be"] * e_k).astype(BF16)
        it["vb"] = it["v"].astype(BF16)
        kq = (it["kap"] * jnp.exp(rel - ld)).astype(BF16)
        it["lhs"] = jnp.concatenate([kq, (it["r"] * jnp.exp(rel)).astype(BF16)], axis=0) if with_y else kq
        it["sd"] = it["s"] * jnp.exp(cl_end)
    for it in insts:
        it["st"] = _dot_nt(it["lhs"], it["sd"].astype(BF16))
        it["a_b"] = _dot_nt(it["lhs"], pack(it["bt"]))
        it["a_k"] = _dot_nt(it["lhs"], pack(it["kt"]))

    for it in insts:
        a = jnp.where(strict4[it["reverse"]], it["a_b"][:c], 0.0)
        it["t_inv"] = eye4 - a
        ab = a.astype(BF16)
        it["pw"] = _dot(ab, pack(ab)).astype(BF16)
    n = 2
    while 2 * n < c:
        for it in insts:
            both = _dot(jnp.concatenate([it["pw"], it["t_inv"].astype(BF16)], axis=0), pack(it["pw"]))
            it["pw"] = both[:c].astype(BF16)
            it["t_inv"] = it["t_inv"] + both[c:]
        n *= 2
    for it in insts:
        it["t_inv"] = it["t_inv"] + _dot(it["t_inv"].astype(BF16), pack(it["pw"]))

    for it in insts:
        rev = it["reverse"]
        a_kk = jnp.where(strict4[rev], it["a_k"][:c], 0.0)
        if with_y:
            a_rk = jnp.where(incl4[rev], it["a_k"][c:], 0.0)
            it["av"] = _dot(jnp.concatenate([a_kk, a_rk], axis=0).astype(BF16), pack(it["vb"]))
        else:
            it["av"] = _dot(a_kk.astype(BF16), pack(it["vb"]))
    for it in insts:
        rhs = it["st"][:c] + it["av"][:c]
        it["ub"] = _dot(it["t_inv"].astype(BF16), pack(rhs.astype(BF16))).astype(BF16)
    out = []
    for it in insts:
        y = None
        if with_y:
            a_rb = jnp.where(incl4[it["reverse"]], it["a_b"][c:], 0.0)
            y = it["st"][c:] + it["av"][c:] - _dot(a_rb.astype(BF16), pack(it["ub"]))
        delta = _dot_tn(jnp.concatenate([it["vb"], it["ub"]], axis=0),
                        jnp.concatenate([it["kt"], -it["bt"]], axis=0))
        out.append((it["sd"] + jnp.where(bd_mask, delta, 0.0), y))
    return out


def _scan_kernel(*refs, with_y, width):
    n_in = 12 if with_y else 10
    in_refs, refs = refs[:n_in], refs[n_in:]
    s0_ref, refs = refs[0], refs[1:]
    if with_y:
        y_refs, refs = refs[:2], refs[2:]
    sfin_ref, s_scr = refs
    c_idx = pl.program_id(1)
    n_groups = width // MXU_WIDTH

    @pl.when(c_idx == 0)
    def _():
        s_scr[...] = s0_ref[0]

    per_dir = 6 if with_y else 5
    names = ("ld", "kap", "kd", "be", "v", "r")[:per_dir]
    n_sub = in_refs[0].shape[1] // CHUNK

    def sub_chunk(j, carry):
        insts = []
        for d in range(2):
            rows = pl.ds(pl.multiple_of((j if d == 0 else n_sub - 1 - j) * CHUNK, CHUNK), CHUNK)
            d_refs = in_refs[d * per_dir:(d + 1) * per_dir]
            for g in range(n_groups):
                lo, hi = g * MXU_WIDTH, (g + 1) * MXU_WIDTH
                it = {n: ref[0, rows, lo:hi].astype(F32) for n, ref in zip(names, d_refs)}
                it.update(reverse=d, s=s_scr[d, g], d=d, g=g, rows=rows)
                insts.append(it)
        for it, (s_new, y) in zip(insts, _scan_chunk(insts, with_y)):
            s_scr[it["d"], it["g"]] = s_new
            if with_y:
                lo = it["g"] * MXU_WIDTH
                y_refs[it["d"]][0, it["rows"], lo:lo + MXU_WIDTH] = y
        return carry

    lax.fori_loop(0, n_sub, sub_chunk, 0)

    @pl.when(c_idx == pl.num_programs(1) - 1)
    def _():
        sfin_ref[0] = s_scr[...]


def _wkv_scan(dirs, kap, v, r, s0):
    b, t, width = kap.shape
    rows = SCAN_SUB * CHUNK if t % (SCAN_SUB * CHUNK) == 0 else CHUNK
    nc = t // rows
    with_y = r is not None
    fwd = lambda bi, ci: (bi, ci, 0)
    bwd = lambda bi, ci: (bi, nc - 1 - ci, 0)
    ins, in_specs = [], []
    for d, imap in enumerate((fwd, bwd)):
        ld, kd, be = dirs[d]
        for arr in (ld, kap, kd, be, v) + ((r,) if with_y else ()):
            ins.append(arr)
            in_specs.append(pl.BlockSpec((1, rows, width), imap))
    n_groups = width // MXU_WIDTH
    s_shape = (2, n_groups, MXU_WIDTH, MXU_WIDTH)
    s_spec = pl.BlockSpec((1,) + s_shape, lambda bi, ci: (bi, 0, 0, 0, 0))
    ins.append(s0)
    in_specs.append(s_spec)
    out_specs, out_shape = [], []
    if with_y:
        out_specs += [pl.BlockSpec((1, rows, width), fwd), pl.BlockSpec((1, rows, width), bwd)]
        out_shape += [jax.ShapeDtypeStruct((b, t, width), F32)] * 2
    out_specs.append(s_spec)
    out_shape.append(jax.ShapeDtypeStruct((b,) + s_shape, F32))
    return pl.pallas_call(
        functools.partial(_scan_kernel, with_y=with_y, width=width),
        grid=(b, nc),
        in_specs=in_specs,
        out_specs=out_specs,
        out_shape=out_shape,
        scratch_shapes=[pltpu.VMEM(s_shape, F32)],
        compiler_params=_cparams(("arbitrary", "arbitrary")),
        name="wkv_scan" if with_y else "wkv_scan_ctx",
    )(*ins)


def _merge_kernel(yf_ref, yb_ref, bonus_ref, g_ref, zb_ref, p_ref, pp_ref, pn_ref, zg_ref, x_ref,
                  lnw_ref, lnb_ref, cw_ref, wout_ref, npost_ref, g1_ref, npre_ref, sc2_ref, sh2_ref,
                  wr_ref, br_ref,
                  x2_ref, h2_ref, gate_ref, idx_ref, rank_ref, cnt_ref, run_scr, *, width):
    first = jnp.logical_and(pl.program_id(0) == 0, pl.program_id(1) == 0)
    i = pl.program_id(1)
    tm = x_ref.shape[1]

    @pl.when(first)
    def _():
        run_scr[...] = jnp.zeros_like(run_scr)

    ones_bd = _block_diag_mask(MXU_WIDTH, HEAD_DIM).astype(BF16)
    row = lax.broadcasted_iota(jnp.int32, (tm, MXU_WIDTH), 0)
    halo = pp_ref.shape[1]
    parts = []
    for lo in range(0, width, MXU_WIDTH):
        hi = lo + MXU_WIDTH
        y = yf_ref[0, :, lo:hi] + yb_ref[0, :, lo:hi]
        mean = _head_sum(y, ones_bd) * (1.0 / HEAD_DIM)
        yc = y - mean
        var = _head_sum(yc * yc, ones_bd) * (1.0 / HEAD_DIM)
        yn = yc * lax.rsqrt(var + GN_EPS) * lnw_ref[:, lo:hi] + lnb_ref[:, lo:hi]
        y_rwkv = (yn + bonus_ref[0, :, lo:hi].astype(F32)) * g_ref[0, :, lo:hi].astype(F32)
        p = p_ref[0, :, lo:hi].astype(F32)
        p_prev = jnp.where(i == 0, 0.0, pp_ref[0, halo - 1:halo, lo:hi].astype(F32))
        p_next = jnp.where(i == pl.num_programs(1) - 1, 0.0, pn_ref[0, 0:1, lo:hi].astype(F32))
        before = jnp.where(row == 0, p_prev, pltpu.roll(p, 1, axis=0))
        after = jnp.where(row == tm - 1, p_next, pltpu.roll(p, tm - 1, axis=0))
        conv = cw_ref[0:1, lo:hi] * before + cw_ref[1:2, lo:hi] * p + cw_ref[2:3, lo:hi] * after
        y_conv = zb_ref[0, :, lo:hi].astype(F32) * conv
        ga = _sigmoid(zg_ref[0, :, lo:hi].astype(F32))
        gb = _sigmoid(zg_ref[0, :, width + lo:width + hi].astype(F32))
        parts.append((ga * y_rwkv + gb * y_conv).astype(BF16))
    merged = jnp.concatenate(parts, axis=1)
    mix = _dot(merged, wout_ref[...])
    x2 = x_ref[0] + g1_ref[0] * _rms(mix, npost_ref[...])
    x2_ref[0] = x2
    h2 = _rms(x2, npre_ref[...]) * (1.0 + sc2_ref[0]) + sh2_ref[0]
    h2_ref[0] = h2.astype(h2_ref.dtype)

    logits = _dot(h2, wr_ref[...]) + br_ref[...]
    lane = lax.broadcasted_iota(jnp.int32, logits.shape, 1)
    vals, idxs = [], []
    sel = jnp.zeros(logits.shape, F32)
    work = logits
    for _ in range(TOP_K):
        m = jnp.max(work, axis=-1, keepdims=True)
        idx = jnp.min(jnp.where(work == m, lane, logits.shape[1]), axis=-1, keepdims=True)
        hit = lane == idx
        vals.append(m)
        idxs.append(idx)
        sel = jnp.where(hit, 1.0, sel)
        work = jnp.where(hit, -jnp.inf, work)
    exps = [jnp.exp(vk - vals[0]) for vk in vals]
    denom = exps[0] + exps[1] + exps[2] + exps[3]
    r_i = lax.broadcasted_iota(jnp.int32, (tm, tm), 0)
    c_i = lax.broadcasted_iota(jnp.int32, (tm, tm), 1)
    before_cnt = _dot((c_i < r_i).astype(BF16), sel.astype(BF16)) + run_scr[...]
    gate_out = jnp.zeros(logits.shape, F32)
    idx_out = jnp.zeros(logits.shape, jnp.int32)
    rank_out = jnp.zeros(logits.shape, jnp.int32)
    for k in range(TOP_K):
        rank_k = jnp.sum(jnp.where(lane == idxs[k], before_cnt, 0.0), axis=-1, keepdims=True)
        gate_out = jnp.where(lane == k, exps[k] / denom, gate_out)
        idx_out = jnp.where(lane == k, idxs[k], idx_out)
        rank_out = jnp.where(lane == k, rank_k.astype(jnp.int32), rank_out)
    gate_ref[0] = gate_out
    idx_ref[0] = idx_out
    rank_ref[0] = rank_out
    run_scr[...] = run_scr[...] + jnp.sum(sel, axis=0, keepdims=True)
    cnt_ref[...] = run_scr[...]


def _merge_route(yf, yb, bonus, g, zb, p, zg, x, pr, tm):
    b, t, d = x.shape
    width = yf.shape[2]
    halo = 8
    hb = tm // halo
    last = t // halo - 1
    row = lambda bi, i: (bi, i, 0)
    const = lambda bi, i: (0, 0)
    per_b = lambda bi, i: (bi, 0, 0)
    prev = lambda bi, i: (bi, jnp.maximum(i * hb - 1, 0), 0)
    nxt = lambda bi, i: (bi, jnp.minimum((i + 1) * hb, last), 0)
    lanes = 128
    in_specs = [pl.BlockSpec((1, tm, width), row)] * 6
    in_specs += [pl.BlockSpec((1, halo, width), prev), pl.BlockSpec((1, halo, width), nxt),
                 pl.BlockSpec((1, tm, 2 * width), row), pl.BlockSpec((1, tm, d), row)]
    params = [pr["ln_x_w"], pr["ln_x_b"], pr["conv_w"], pr["w_out"], pr["norm_post_mix"]]
    in_specs += [pl.BlockSpec(a.shape, const) for a in params]
    in_specs.append(pl.BlockSpec((1, 1, d), per_b))
    in_specs.append(pl.BlockSpec(pr["norm_pre_ffn"].shape, const))
    in_specs += [pl.BlockSpec((1, 1, d), per_b)] * 2
    in_specs += [pl.BlockSpec(pr["w_router"].shape, const), pl.BlockSpec(pr["b_router"].shape, const)]
    out_specs = [pl.BlockSpec((1, tm, d), row), pl.BlockSpec((1, tm, d), row)]
    out_specs += [pl.BlockSpec((1, tm, lanes), row)] * 3
    out_specs.append(pl.BlockSpec((1, lanes), const))
    out_shape = [jax.ShapeDtypeStruct((b, t, d), F32), jax.ShapeDtypeStruct((b, t, d), F32),
                 jax.ShapeDtypeStruct((b, t, lanes), F32), jax.ShapeDtypeStruct((b, t, lanes), jnp.int32),
                 jax.ShapeDtypeStruct((b, t, lanes), jnp.int32), jax.ShapeDtypeStruct((1, lanes), F32)]
    return pl.pallas_call(
        functools.partial(_merge_kernel, width=width),
        grid=(b, t // tm),
        in_specs=in_specs,
        out_specs=out_specs,
        out_shape=out_shape,
        scratch_shapes=[pltpu.VMEM((1, lanes), F32)],
        compiler_params=_cparams(("arbitrary", "arbitrary")),
        name="merge_route",
    )(yf, yb, bonus, g, zb, p, p, p, zg, x, *params, pr["g1"], pr["norm_pre_ffn"], pr["sc2"], pr["sh2"],
      pr["w_router"], pr["b_router"])


def _row_copy(src_ref, src_row, dst_ref, dst_row, sem):
    return pltpu.make_async_copy(src_ref.at[pl.ds(src_row, 1)], dst_ref.at[pl.ds(dst_row, 1)], sem)


def _dispatch_kernel(dest_ref, nrows_ref, h2_ref, xb_ref, zrow, hbuf, sem, zsem, *, tm, n_pad):
    base = pl.program_id(0) * (tm * TOP_K)

    @pl.when(pl.program_id(0) == 0)
    def _():
        zrow[...] = jnp.zeros_like(zrow)

        def per_block(j, carry):
            def fill(r, c2):
                _row_copy(zrow, 0, xb_ref, j * MOE_BLOCK + r, zsem).start()
                return c2
            return lax.fori_loop(nrows_ref[j], MOE_BLOCK, fill, carry)

        lax.fori_loop(0, nrows_ref.shape[0], per_block, 0)

    step = pl.program_id(0)
    slot = step % 2

    def wait_tile(s):
        for _ in range(TOP_K):
            pltpu.make_async_copy(hbuf.at[s], xb_ref.at[pl.ds(0, tm)], sem.at[s]).wait()

    hbuf[slot] = h2_ref[...]
    src = hbuf.at[slot]

    def issue(r, carry):
        for k in range(TOP_K):
            _row_copy(src, r, xb_ref, dest_ref[base + r * TOP_K + k], sem.at[slot]).start(priority=k % 2)
        return carry

    lax.fori_loop(0, tm, issue, 0, unroll=8)

    @pl.when(step > 0)
    def _():
        wait_tile(1 - slot)

    @pl.when(step == pl.num_programs(0) - 1)
    def _():
        wait_tile(slot)
        pltpu.make_async_copy(xb_ref.at[pl.ds(0, n_pad)], xb_ref.at[pl.ds(0, n_pad)], zsem).wait()


def _dispatch(h2, dest, block_rows, tm):
    n_tok, d = h2.shape
    n_rows = block_rows.shape[0] * MOE_BLOCK
    grid_spec = pltpu.PrefetchScalarGridSpec(
        num_scalar_prefetch=2,
        grid=(n_tok // tm,),
        in_specs=[pl.BlockSpec((tm, d), lambda i, dest, nr: (i, 0))],
        out_specs=pl.BlockSpec(memory_space=pl.ANY),
        scratch_shapes=[pltpu.VMEM((8, d), h2.dtype), pltpu.VMEM((2, tm, d), h2.dtype),
                        pltpu.SemaphoreType.DMA((2,)), pltpu.SemaphoreType.DMA],
    )
    return pl.pallas_call(
        functools.partial(_dispatch_kernel, tm=tm, n_pad=n_rows - dest.shape[0]),
        grid_spec=grid_spec,
        out_shape=jax.ShapeDtypeStruct((n_rows, d), h2.dtype),
        compiler_params=_cparams(("arbitrary",)),
        name="moe_dispatch",
    )(dest, block_rows, h2)


def _expert_kernel(be_ref, nrows_ref, first_ref, slot_ref, next_ref, xb_ref, win_hbm, bin_ref, wout_hbm, bout_ref,
                   o_ref, win_f32, wout_f32, win_scr, wout_scr, sem, *, d_ff):
    i = pl.program_id(0)
    n_rows = nrows_ref[i]

    def weight_copies(e, s):
        return (pltpu.make_async_copy(win_hbm.at[e], win_f32.at[s], sem.at[s, 0]),
                pltpu.make_async_copy(wout_hbm.at[e], wout_f32.at[s], sem.at[s, 1]))

    @pl.when(i == 0)
    def _():
        for cp in weight_copies(be_ref[0], 0):
            cp.start()

    @pl.when(first_ref[i] == 1)
    def _():
        s = slot_ref[i]
        for cp in weight_copies(be_ref[i], s):
            cp.wait()
        win_scr[...] = win_f32[s].astype(BF16)
        wout_scr[...] = wout_f32[s].astype(BF16)

        @pl.when(next_ref[i] >= 0)
        def _():
            for cp in weight_copies(next_ref[i], 1 - s):
                cp.start()

    @pl.when(n_rows > 0)
    def _():
        row = lax.broadcasted_iota(jnp.int32, xb_ref.shape, 0)
        xb = jnp.where(row < n_rows, xb_ref[...], 0.0).astype(BF16)
        gu = _dot(xb, win_scr[...]) + bin_ref[0]
        gate = jnp.minimum(gu[:, :d_ff], SWIGLU_LIMIT)
        up = jnp.clip(gu[:, d_ff:], -SWIGLU_LIMIT, SWIGLU_LIMIT)
        act = (up + 1.0) * gate * _sigmoid(SWIGLU_ALPHA * gate)
        o_ref[...] = _dot(act.astype(BF16), wout_scr[...]) + bout_ref[0]

    @pl.when(n_rows == 0)
    def _():
        o_ref[...] = jnp.zeros_like(o_ref)


def _experts(xb, block_e, block_rows, counts, w_in, b_in, w_out, b_out):
    n_rows, d = xb.shape
    n_e, _, d_ff2 = w_in.shape
    d_ff = d_ff2 // 2
    n_blocks = n_rows // MOE_BLOCK
    valid = block_rows > 0
    prev_e = jnp.concatenate([jnp.full((1,), -1, jnp.int32), block_e[:-1]])
    first = jnp.logical_and(valid, block_e != prev_e).astype(jnp.int32)
    slot = ((jnp.cumsum(first) - 1) % 2).astype(jnp.int32)
    experts = jnp.arange(n_e, dtype=jnp.int32)
    later = jnp.logical_and(experts[None, :] > experts[:, None], counts[None, :] > 0)
    next_of = jnp.min(jnp.where(later, experts[None, :], n_e), axis=1)
    next_of = jnp.where(next_of == n_e, -1, next_of)
    next_e = jnp.sum(jnp.where(block_e[:, None] == experts[None, :], next_of[None, :], 0), axis=1).astype(jnp.int32)
    blk = lambda i, *_: (i, 0)
    per_e = lambda i, be, *_: (be[i], 0, 0)
    grid_spec = pltpu.PrefetchScalarGridSpec(
        num_scalar_prefetch=5,
        grid=(n_blocks,),
        in_specs=[pl.BlockSpec((MOE_BLOCK, d), blk),
                  pl.BlockSpec(memory_space=pl.ANY),
                  pl.BlockSpec((1, 1, d_ff2), per_e),
                  pl.BlockSpec(memory_space=pl.ANY),
                  pl.BlockSpec((1, 1, d), per_e)],
        out_specs=pl.BlockSpec((MOE_BLOCK, d), blk),
        scratch_shapes=[pltpu.VMEM((2, d, d_ff2), F32), pltpu.VMEM((2, d_ff, d), F32),
                        pltpu.VMEM((d, d_ff2), BF16), pltpu.VMEM((d_ff, d), BF16),
                        pltpu.SemaphoreType.DMA((2, 2))],
    )
    return pl.pallas_call(
        functools.partial(_expert_kernel, d_ff=d_ff),
        grid_spec=grid_spec,
        out_shape=jax.ShapeDtypeStruct((n_rows, d), F32),
        compiler_params=_cparams(("arbitrary",)),
        name="moe_experts",
    )(block_e, block_rows, first, slot, next_e, xb, w_in, b_in.reshape(n_e, 1, d_ff2), w_out,
      b_out.reshape(n_e, 1, d))


def _combine_kernel(dest_ref, yb_ref, gate_ref, x2_ref, npost_ref, g2_ref, o_ref, ybuf, sem, *, tm):
    tile = pl.program_id(0) * pl.num_programs(1) + pl.program_id(1)
    n_tiles = pl.num_programs(0) * pl.num_programs(1)
    slot = tile % 2

    def gather(tile_idx, s):
        base = tile_idx * (tm * TOP_K)

        def issue(r, carry):
            for k in range(TOP_K):
                _row_copy(yb_ref, dest_ref[base + r * TOP_K + k], ybuf.at[s, k], r, sem.at[s]).start(priority=k % 2)
            return carry

        lax.fori_loop(0, tm, issue, 0, unroll=8)

    @pl.when(tile == 0)
    def _():
        gather(0, 0)

    @pl.when(tile + 1 < n_tiles)
    def _():
        gather(tile + 1, 1 - slot)

    for k in range(TOP_K):
        pltpu.make_async_copy(yb_ref.at[pl.ds(0, tm)], ybuf.at[slot, k], sem.at[slot]).wait()
    gates = gate_ref[0]
    f = None
    for k in range(TOP_K):
        term = gates[:, k:k + 1] * ybuf[slot, k]
        f = term if f is None else f + term
    o_ref[0] = x2_ref[0] + g2_ref[0] * _rms(f, npost_ref[...])


def _combine(yb_rows, dest, gates, x2, npost, g2, tm):
    b, t, d = x2.shape
    row = lambda bi, i, dest: (bi, i, 0)
    grid_spec = pltpu.PrefetchScalarGridSpec(
        num_scalar_prefetch=1,
        grid=(b, t // tm),
        in_specs=[pl.BlockSpec(memory_space=pl.ANY),
                  pl.BlockSpec((1, tm, gates.shape[2]), row),
                  pl.BlockSpec((1, tm, d), row),
                  pl.BlockSpec((1, d), lambda bi, i, dest: (0, 0)),
                  pl.BlockSpec((1, 1, d), lambda bi, i, dest: (bi, 0, 0))],
        out_specs=pl.BlockSpec((1, tm, d), row),
        scratch_shapes=[pltpu.VMEM((2, TOP_K, tm, d), F32), pltpu.SemaphoreType.DMA((2,))],
    )
    return pl.pallas_call(
        functools.partial(_combine_kernel, tm=tm),
        grid_spec=grid_spec,
        out_shape=jax.ShapeDtypeStruct((b, t, d), F32),
        compiler_params=_cparams(("arbitrary", "arbitrary")),
        name="moe_combine",
    )(dest, yb_rows, gates, x2, npost, g2)


def _lora_up(w_up):
    _, r, w = w_up.shape
    z = jnp.zeros((r, w), w_up.dtype)
    return jnp.concatenate([jnp.concatenate([w_up[0], z], axis=1),
                            jnp.concatenate([z, w_up[1]], axis=1)], axis=0).astype(BF16)


def _row_tile(t, pref):
    return pref if t % pref == 0 else t


def kernel(x, c, ctx, c_ctx, w_ada, b_ada, norm_pre_mix, norm_post_mix, norm_pre_ffn, norm_post_ffn,
           w_in, mu_shift, w0, w_decay_up, a0, w_iclr_up, k_k, k_a, r_k, w_gate_up, ln_x_w, ln_x_b,
           conv_w, w_out, w_router, b_router, w_exp_in, b_exp_in, w_exp_out, b_exp_out):
    b, t, d = x.shape
    t_ctx = ctx.shape[1]
    depth = w_ada.shape[0]
    width = k_k.shape[1]
    n_groups = width // MXU_WIDTH
    k_off, v_off = 0, width
    decay_off = 2 * width
    iclr_off = decay_off + 2 * DECAY_LORA
    r_off = iclr_off + 2 * ICLR_LORA
    glora_off = r_off + width
    conv_off = glora_off + GATE_LORA
    gate_off = conv_off + 3 * width

    xc = ctx
    for l in range(depth):
        last = l == depth - 1
        rows = jnp.concatenate([c, c_ctx[None, :], jnp.zeros((8 - b - 1, d), F32)], axis=0)
        mod = _ada(rows, w_ada[l], b_ada[l])
        sh1, sc1, g1, sh2, sc2, g2 = [mod[:b, None, j * d:(j + 1) * d] for j in range(6)]
        csh1, csc1, cg1, csh2, csc2, cg2 = [jnp.broadcast_to(mod[b:b + 1, None, j * d:(j + 1) * d], (b, 1, d))
                                            for j in range(6)]

        wl = w_in[l]
        cols = lambda lo, n: wl[:, lo:lo + n]
        w_main = jnp.concatenate([cols(k_off, width), cols(v_off, width), cols(r_off, width)], axis=1).astype(BF16)
        w_lora = jnp.concatenate([cols(decay_off, 2 * DECAY_LORA), cols(iclr_off, 2 * ICLR_LORA),
                                  cols(glora_off, GATE_LORA)], axis=1).astype(BF16)
        w_conv = tuple(cols(conv_off + j * width, width).astype(BF16) for j in range(3))
        w_gate = cols(gate_off, 2 * d).astype(BF16)
        mu = mu_shift[l]
        mu_main = jnp.concatenate([mu[k_off:k_off + width], mu[v_off:v_off + width], mu[r_off:r_off + width]])[None, :]
        mu_lora = jnp.concatenate([mu[decay_off:decay_off + 2 * DECAY_LORA], mu[iclr_off:iclr_off + 2 * ICLR_LORA],
                                   mu[glora_off:glora_off + GATE_LORA]])[None, :]
        sp = {
            "mu_main": mu_main, "mu_lora": mu_lora,
            "w0": w0[l].reshape(1, 2 * width), "w_dec": _lora_up(w_decay_up[l]),
            "a0": a0[l].reshape(1, 2 * width), "w_iclr": _lora_up(w_iclr_up[l]),
            "k_k": k_k[l][None, :], "k_a": k_a[l][None, :],
            "r_k": r_k[l].reshape(1, width), "w_gate": w_gate_up[l].astype(BF16),
        }
        if not last:
            raise NotImplementedError("context stream update for non-final layers")

        n_state_main = 2 * width
        zc_main, zc_lora = _inproj(xc, norm_pre_mix[l], csc1, csh1,
                                   [w_main[:, :n_state_main], w_lora[:, :2 * DECAY_LORA + 2 * ICLR_LORA]],
                                   None, _row_tile(t_ctx, 256))
        sp_ctx = dict(sp, mu_main=mu_main[:, :n_state_main], mu_lora=mu_lora[:, :2 * DECAY_LORA + 2 * ICLR_LORA])
        kap_c, v_c, ld0_c, ld1_c, kd0_c, kd1_c, be0_c, be1_c = _state_terms(
            zc_main, zc_lora, sp_ctx, grid_mode=False, with_read=False, tm=t_ctx)
        s0 = jnp.zeros((b, 2, n_groups, MXU_WIDTH, MXU_WIDTH), F32)
        (s_ctx,) = _wkv_scan([(ld0_c, kd0_c, be0_c), (ld1_c, kd1_c, be1_c)], kap_c, v_c, None, s0)

        zm, zl, zg, zb, p = _inproj(x, norm_pre_mix[l], sc1, sh1, [w_main, w_lora, w_gate], w_conv,
                                    _row_tile(t, 512))
        kap, v, ld0, ld1, kd0, kd1, be0, be1, r, bonus, g = _state_terms(
            zm, zl, sp, grid_mode=True, with_read=True, tm=_row_tile(t, 512))
        yf, yb, _ = _wkv_scan([(ld0, kd0, be0), (ld1, kd1, be1)], kap, v, r, s_ctx)

        lanes = 128
        pad_e = lanes - N_EXPERTS
        pr = {
            "ln_x_w": ln_x_w[l][None, :], "ln_x_b": ln_x_b[l][None, :], "conv_w": conv_w[l],
            "w_out": w_out[l].astype(BF16), "norm_post_mix": norm_post_mix[l][None, :], "g1": g1,
            "norm_pre_ffn": norm_pre_ffn[l][None, :], "sc2": sc2, "sh2": sh2,
            "w_router": jnp.pad(w_router[l], ((0, 0), (0, pad_e))),
            "b_router": jnp.pad(b_router[l], (0, pad_e), constant_values=-jnp.inf)[None, :],
        }
        x2, h2, gates, top_idx, rank, counts = _merge_route(yf, yb, bonus, g, zb, p, zg, x, pr, _row_tile(t, 256))

        n_tok = b * t
        nk = n_tok * TOP_K
        counts = counts[0, :N_EXPERTS].astype(jnp.int32)
        padded = (counts + MOE_BLOCK - 1) // MOE_BLOCK * MOE_BLOCK
        pad_ends = jnp.cumsum(padded)
        pad_starts = pad_ends - padded
        experts = jnp.arange(N_EXPERTS, dtype=jnp.int32)
        e_flat = top_idx[..., :TOP_K].reshape(nk)
        dest = jnp.sum(jnp.where(e_flat[:, None] == experts[None, :], pad_starts[None, :], 0), axis=1) \
            + rank[..., :TOP_K].reshape(nk)
        n_blocks = -(-(nk + N_EXPERTS * (MOE_BLOCK - 1)) // MOE_BLOCK)
        n_rows = n_blocks * MOE_BLOCK
        blk_start = jnp.arange(n_blocks, dtype=jnp.int32) * MOE_BLOCK
        block_e = jnp.minimum(jnp.sum((pad_ends[None, :] <= blk_start[:, None]).astype(jnp.int32), axis=1),
                              N_EXPERTS - 1)
        own = block_e[:, None] == experts[None, :]
        seg_end = jnp.sum(jnp.where(own, (pad_starts + counts)[None, :], 0), axis=1)
        block_rows = jnp.clip(seg_end - blk_start, 0, MOE_BLOCK).astype(jnp.int32)
        xb = _dispatch(h2.reshape(n_tok, d), dest, block_rows, _row_tile(n_tok, 256))
        yb_rows = _experts(xb, block_e, block_rows, counts, w_exp_in[l], b_exp_in[l], w_exp_out[l], b_exp_out[l])
        x = _combine(yb_rows, dest, gates, x2, norm_post_ffn[l][None, :], g2, _row_tile(t, 256))
    return x
```

```python
import functools

import jax
import jax.numpy as jnp
from jax import lax
from jax.experimental import pallas as pl
from jax.experimental.pallas import tpu as pltpu

F32 = jnp.float32
BF16 = jnp.bfloat16

HEAD_DIM = 64
GRID_W = 64
DECAY_LORA = 64
ICLR_LORA = 64
GATE_LORA = 128
GN_EPS = 64e-5
NORM_EPS = 1e-12
CONV_K = 3
N_EXPERTS = 32
TOP_K = 4
SWIGLU_LIMIT = 7.0
SWIGLU_ALPHA = 1.702
MOE_BLOCK = 256
RMS_EPS = 1e-6

MXU_WIDTH = 256
HEADS_PER_GROUP = MXU_WIDTH // HEAD_DIM
CHUNK = 64
SCAN_SUB = 2
DECAY_SCALE = 0.6065306597126334
MOE_TILE = 256
SEG_ROWS = 8
SEG_MAX = 160
assert SEG_MAX * SEG_ROWS >= MOE_TILE * TOP_K + N_EXPERTS * (SEG_ROWS - 1)
VMEM_LIMIT = 56 * 1024 * 1024


def _cparams(sem):
    return pltpu.CompilerParams(dimension_semantics=sem, vmem_limit_bytes=VMEM_LIMIT)


def _dot(a, b):
    return jnp.dot(a, b, preferred_element_type=F32)


def _dot_nt(a, b):
    return lax.dot_general(a, b, (((1,), (1,)), ((), ())), preferred_element_type=F32)


def _dot_tn(a, b):
    return lax.dot_general(a, b, (((0,), (0,)), ((), ())), preferred_element_type=F32)


def _split3(x):
    h = x.astype(BF16)
    r = x - h.astype(F32)
    m = r.astype(BF16)
    l = (r - m.astype(F32)).astype(BF16)
    return h, m, l


def _block_diag_mask(n, blk):
    r = lax.broadcasted_iota(jnp.int32, (n, n), 0) // blk
    c = lax.broadcasted_iota(jnp.int32, (n, n), 1) // blk
    return r == c


def _head_sum(x, ones_bd):
    h = x.astype(BF16)
    l = (x - h.astype(F32)).astype(BF16)
    return _dot(h, ones_bd) + _dot(l, ones_bd)


def _rms(x, gain):
    return x * lax.rsqrt(jnp.mean(x * x, axis=-1, keepdims=True) + RMS_EPS) * gain


def _sigmoid(x):
    return 1.0 / (1.0 + jnp.exp(-x))


def _ada_kernel(c_ref, w_ref, b_ref, o_ref):
    c = c_ref[...]
    s = c * _sigmoid(c)
    o_ref[...] = _dot(s, w_ref[...]) + b_ref[...]


def _ada(c_rows, w, b):
    rows, d = c_rows.shape
    n = w.shape[1]
    tn = 1024
    return pl.pallas_call(
        _ada_kernel,
        grid=(n // tn,),
        in_specs=[pl.BlockSpec((rows, d), lambda j: (0, 0)),
                  pl.BlockSpec((d, tn), lambda j: (0, j)),
                  pl.BlockSpec((1, tn), lambda j: (0, j))],
        out_specs=pl.BlockSpec((rows, tn), lambda j: (0, j)),
        out_shape=jax.ShapeDtypeStruct((rows, n), F32),
        compiler_params=_cparams(("arbitrary",)),
        name="ada_mod",
    )(c_rows, w, b.reshape(1, n))


def _inproj_kernel(x_ref, g_ref, sc_ref, sh_ref, *refs, n_plain, with_conv):
    n_w = n_plain + (3 if with_conv else 0)
    w_refs, o_refs = refs[:n_w], refs[n_w:]
    x = x_ref[0]
    h = _rms(x, g_ref[...]) * (1.0 + sc_ref[0]) + sh_ref[0]
    hb = h.astype(BF16)
    col = 512
    for w_ref, o_ref in zip(w_refs[:n_plain], o_refs[:n_plain]):
        n = w_ref.shape[1]
        for j in range(0, n, col):
            e = min(j + col, n)
            o_ref[0, :, j:e] = _dot(hb, w_ref[:, j:e]).astype(o_ref.dtype)
    if with_conv:
        wb_ref, wc_ref, wx_ref = w_refs[n_plain:]
        ob_ref, op_ref = o_refs[n_plain:]
        n = wb_ref.shape[1]
        for j in range(0, n, col):
            e = min(j + col, n)
            ob_ref[0, :, j:e] = _dot(hb, wb_ref[:, j:e]).astype(ob_ref.dtype)
            op_ref[0, :, j:e] = (_dot(hb, wc_ref[:, j:e]) * _dot(hb, wx_ref[:, j:e])).astype(op_ref.dtype)


def _inproj(x, gain, sc, sh, plain_ws, conv_ws, tm):
    b, t, d = x.shape
    with_conv = conv_ws is not None
    ws = list(plain_ws) + (list(conv_ws) if with_conv else [])
    out_w = [w.shape[1] for w in plain_ws] + ([conv_ws[0].shape[1]] * 2 if with_conv else [])
    row = lambda bi, i: (bi, i, 0)
    const = lambda bi, i: (0, 0)
    in_specs = [pl.BlockSpec((1, tm, d), row),
                pl.BlockSpec((1, d), const),
                pl.BlockSpec((1, 1, d), lambda bi, i: (bi, 0, 0)),
                pl.BlockSpec((1, 1, d), lambda bi, i: (bi, 0, 0))]
    in_specs += [pl.BlockSpec(w.shape, const, pipeline_mode=pl.Buffered(1)) for w in ws]
    return pl.pallas_call(
        functools.partial(_inproj_kernel, n_plain=len(plain_ws), with_conv=with_conv),
        grid=(b, t // tm),
        in_specs=in_specs,
        out_specs=[pl.BlockSpec((1, tm, n), row) for n in out_w],
        out_shape=[jax.ShapeDtypeStruct((b, t, n), BF16) for n in out_w],
        compiler_params=_cparams(("arbitrary", "arbitrary")),
        name="in_proj",
    )(x, gain.reshape(1, d), sc, sh, *ws)


def _shift_mix(cur, prev, nxt, mu, grid_mode, is_first, is_last):
    tm, w = cur.shape
    row = lax.broadcasted_iota(jnp.int32, (tm, w), 0)
    grp = lax.broadcasted_iota(jnp.int32, (tm, w), 1) % 4
    back = pltpu.roll(cur, 1, axis=0)
    fwd = pltpu.roll(cur, tm - 1, axis=0)
    if grid_mode:
        colpos = row % GRID_W
        left = jnp.where(colpos == 0, 0.0, back)
        right = jnp.where(colpos == GRID_W - 1, 0.0, fwd)
        prev = jnp.where(is_first, 0.0, prev)
        nxt = jnp.where(is_last, 0.0, nxt)
        up = jnp.concatenate([prev, cur[:tm - GRID_W]], axis=0)
        down = jnp.concatenate([cur[GRID_W:], nxt], axis=0)
        shifted = jnp.where(grp == 0, left, jnp.where(grp == 1, right, jnp.where(grp == 2, up, down)))
    else:
        before = jnp.where(row == 0, 0.0, back)
        after = jnp.where(row == tm - 1, 0.0, fwd)
        shifted = jnp.where(grp % 2 == 0, before, after)
    return cur + mu * (shifted - cur)


def _state_kernel(*refs, grid_mode, with_read, width):
    if grid_mode:
        (zm_ref, zmp_ref, zmn_ref, zl_ref, zlp_ref, zln_ref), refs = refs[:6], refs[6:]
    else:
        (zm_ref, zl_ref), refs = refs[:2], refs[2:]
        zmp_ref = zmn_ref = zlp_ref = zln_ref = None
    (mum_ref, mul_ref, w0_ref, wdec_ref, a0_ref, wic_ref, kk_ref, ka_ref), refs = refs[:8], refs[8:]
    if with_read:
        (rk_ref, wg_ref), refs = refs[:2], refs[2:]
    (kap_ref, v_ref, ld0_ref, ld1_ref, kd0_ref, kd1_ref, be0_ref, be1_ref), refs = refs[:8], refs[8:]
    if with_read:
        r_ref, bonus_ref, g_ref = refs

    i = pl.program_id(1)
    is_first = i == 0
    is_last = i == pl.num_programs(1) - 1
    ones_bd = _block_diag_mask(MXU_WIDTH, HEAD_DIM).astype(BF16)

    def shifted(z_ref, zp_ref, zn_ref, mu_ref, lo, hi):
        cur = z_ref[0, :, lo:hi].astype(F32)
        prev = zp_ref[0, :, lo:hi].astype(F32) if grid_mode else None
        nxt = zn_ref[0, :, lo:hi].astype(F32) if grid_mode else None
        return _shift_mix(cur, prev, nxt, mu_ref[:, lo:hi], grid_mode, is_first, is_last)

    lora = shifted(zl_ref, zlp_ref, zln_ref, mul_ref, 0, zl_ref.shape[2])
    lw = jnp.tanh(lora[:, :2 * DECAY_LORA]).astype(BF16)
    la = lora[:, 2 * DECAY_LORA:2 * DECAY_LORA + 2 * ICLR_LORA].astype(BF16)
    if with_read:
        lg = _sigmoid(lora[:, 2 * DECAY_LORA + 2 * ICLR_LORA:]).astype(BF16)

    for lo in range(0, width, MXU_WIDTH):
        hi = lo + MXU_WIDTH
        k = shifted(zm_ref, zmp_ref, zmn_ref, mum_ref, lo, hi)
        v = shifted(zm_ref, zmp_ref, zmn_ref, mum_ref, width + lo, width + hi)
        kk = k * kk_ref[:, lo:hi]
        n2 = _head_sum(kk * kk, ones_bd)
        kk = kk / jnp.maximum(jnp.sqrt(n2), NORM_EPS)
        kap_ref[0, :, lo:hi] = kk.astype(kap_ref.dtype)
        v_ref[0, :, lo:hi] = v.astype(v_ref.dtype)
        a_sum = None
        for d, (ld_ref, kd_ref, be_ref) in enumerate(((ld0_ref, kd0_ref, be0_ref), (ld1_ref, kd1_ref, be1_ref))):
            lo_d, hi_d = d * width + lo, d * width + hi
            pre_w = w0_ref[:, lo_d:hi_d] + _dot(lw, wdec_ref[:, lo_d:hi_d])
            ld_ref[0, :, lo:hi] = -DECAY_SCALE * _sigmoid(pre_w)
            a = _sigmoid(a0_ref[:, lo_d:hi_d] + _dot(la, wic_ref[:, lo_d:hi_d]))
            kd_ref[0, :, lo:hi] = (k * (1.0 + (a - 1.0) * ka_ref[:, lo:hi])).astype(kd_ref.dtype)
            be_ref[0, :, lo:hi] = (kk * a).astype(be_ref.dtype)
            a_sum = a if a_sum is None else a_sum + a
        if with_read:
            r = shifted(zm_ref, zmp_ref, zmn_ref, mum_ref, 2 * width + lo, 2 * width + hi)
            r_ref[0, :, lo:hi] = r.astype(r_ref.dtype)
            k_bonus = k * (1.0 + (0.5 * a_sum - 1.0) * ka_ref[:, lo:hi])
            s = _head_sum(r * k_bonus * rk_ref[:, lo:hi], ones_bd)
            bonus_ref[0, :, lo:hi] = (s * v).astype(bonus_ref.dtype)
            g_ref[0, :, lo:hi] = _dot(lg, wg_ref[:, lo:hi]).astype(g_ref.dtype)


def _state_terms(zm, zl, p, grid_mode, with_read, tm):
    b, t, _ = zm.shape
    width = p["k_k"].shape[1]
    nt = t // tm
    row = lambda bi, i: (bi, i, 0)
    const = lambda bi, i: (0, 0)
    ins, in_specs = [], []
    if grid_mode:
        hb = tm // GRID_W
        last = t // GRID_W - 1
        prev = lambda bi, i: (bi, jnp.maximum(i * hb - 1, 0), 0)
        nxt = lambda bi, i: (bi, jnp.minimum((i + 1) * hb, last), 0)
        for z in (zm, zl):
            ins += [z, z, z]
            in_specs += [pl.BlockSpec((1, tm, z.shape[2]), row),
                         pl.BlockSpec((1, GRID_W, z.shape[2]), prev),
                         pl.BlockSpec((1, GRID_W, z.shape[2]), nxt)]
    else:
        for z in (zm, zl):
            ins.append(z)
            in_specs.append(pl.BlockSpec((1, tm, z.shape[2]), row))
    names = ["mu_main", "mu_lora", "w0", "w_dec", "a0", "w_iclr", "k_k", "k_a"]
    if with_read:
        names += ["r_k", "w_gate"]
    for n in names:
        ins.append(p[n])
        in_specs.append(pl.BlockSpec(p[n].shape, const))
    out_dt = [BF16, BF16, F32, F32, BF16, BF16, BF16, BF16] + ([BF16] * 3 if with_read else [])
    return pl.pallas_call(
        functools.partial(_state_kernel, grid_mode=grid_mode, with_read=with_read, width=width),
        grid=(b, nt),
        in_specs=in_specs,
        out_specs=[pl.BlockSpec((1, tm, width), row) for _ in out_dt],
        out_shape=[jax.ShapeDtypeStruct((b, t, width), dt) for dt in out_dt],
        compiler_params=_cparams(("arbitrary", "arbitrary")),
        name="state_terms_grid" if grid_mode else "state_terms_seq",
    )(*ins)


def _pack_bd(y, bd_mask):
    reps = MXU_WIDTH // y.shape[0]
    return jnp.where(bd_mask, jnp.concatenate([y] * reps, axis=0), jnp.zeros((), y.dtype))


def _scan_chunk(insts, with_y):
    c = CHUNK
    bd_mask = _block_diag_mask(MXU_WIDTH, HEAD_DIM)
    ti = lax.broadcasted_iota(jnp.int32, (c, c), 0)
    tj = lax.broadcasted_iota(jnp.int32, (c, c), 1)
    t4 = lax.broadcasted_iota(jnp.int32, (c, MXU_WIDTH), 0)
    i4 = lax.broadcasted_iota(jnp.int32, (c, MXU_WIDTH), 1) % c
    eye4 = (t4 == i4).astype(F32)
    tri = ((tj <= ti).astype(BF16), (tj >= ti).astype(BF16))
    strict4 = (i4 < t4, i4 > t4)
    incl4 = (i4 <= t4, i4 >= t4)
    pack = lambda y: _pack_bd(y, bd_mask)

    for it in insts:
        rev = it["reverse"]
        ld = it["ld"]
        cl = sum(_dot(tri[rev], t) for t in _split3(ld))
        cl_end = cl[0:1] if rev else cl[c - 1:c]
        rel = cl - cl_end
        e_k = jnp.exp(-rel)
        it["kt"] = (it["kd"] * e_k).astype(BF16)
        it["bt"] = (it["be"] * e_k).astype(BF16)
        it["vb"] = it["v"].astype(BF16)
        kq = (it["kap"] * jnp.exp(rel - ld)).astype(BF16)
        it["lhs"] = jnp.concatenate([kq, (it["r"] * jnp.exp(rel)).astype(BF16)], axis=0) if with_y else kq
        it["sd"] = it["s"] * jnp.exp(cl_end)
    for it in insts:
        it["st"] = _dot_nt(it["lhs"], it["sd"].astype(BF16))
        it["a_b"] = _dot_nt(it["lhs"], pack(it["bt"]))
        it["a_k"] = _dot_nt(it["lhs"], pack(it["kt"]))

    for it in insts:
        a = jnp.where(strict4[it["reverse"]], it["a_b"][:c], 0.0)
        it["t_inv"] = eye4 - a
        ab = a.astype(BF16)
        it["pw"] = _dot(ab, pack(ab)).astype(BF16)
    n = 2
    while 2 * n < c:
        for it in insts:
            both = _dot(jnp.concatenate([it["pw"], it["t_inv"].astype(BF16)], axis=0), pack(it["pw"]))
            it["pw"] = both[:c].astype(BF16)
            it["t_inv"] = it["t_inv"] + both[c:]
        n *= 2
    for it in insts:
        it["t_inv"] = it["t_inv"] + _dot(it["t_inv"].astype(BF16), pack(it["pw"]))

    for it in insts:
        rev = it["reverse"]
        a_kk = jnp.where(strict4[rev], it["a_k"][:c], 0.0)
        if with_y:
            a_rk = jnp.where(incl4[rev], it["a_k"][c:], 0.0)
            it["av"] = _dot(jnp.concatenate([a_kk, a_rk], axis=0).astype(BF16), pack(it["vb"]))
        else:
            it["av"] = _dot(a_kk.astype(BF16), pack(it["vb"]))
    for it in insts:
        rhs = it["st"][:c] + it["av"][:c]
        it["ub"] = _dot(it["t_inv"].astype(BF16), pack(rhs.astype(BF16))).astype(BF16)
    out = []
    for it in insts:
        y = None
        if with_y:
            a_rb = jnp.where(incl4[it["reverse"]], it["a_b"][c:], 0.0)
            y = it["st"][c:] + it["av"][c:] - _dot(a_rb.astype(BF16), pack(it["ub"]))
        delta = _dot_tn(jnp.concatenate([it["vb"], it["ub"]], axis=0),
                        jnp.concatenate([it["kt"], -it["bt"]], axis=0))
        out.append((it["sd"] + jnp.where(bd_mask, delta, 0.0), y))
    return out


def _scan_kernel(*refs, with_y, width):
    n_in = 12 if with_y else 10
    in_refs, refs = refs[:n_in], refs[n_in:]
    s0_ref, refs = refs[0], refs[1:]
    if with_y:
        y_refs, refs = refs[:2], refs[2:]
    sfin_ref, s_scr = refs
    c_idx = pl.program_id(0)
    n_groups = width // MXU_WIDTH

    @pl.when(c_idx == 0)
    def _():
        s_scr[...] = s0_ref[...]

    per_dir = 6 if with_y else 5
    names = ("ld", "kap", "kd", "be", "v", "r")[:per_dir]
    n_batch = in_refs[0].shape[0]
    n_sub = in_refs[0].shape[1] // CHUNK

    def sub_chunk(j, carry):
        insts = []
        for bi in range(n_batch):
            for d in range(2):
                rows = pl.ds(pl.multiple_of((j if d == 0 else n_sub - 1 - j) * CHUNK, CHUNK), CHUNK)
                d_refs = in_refs[d * per_dir:(d + 1) * per_dir]
                for g in range(n_groups):
                    lo, hi = g * MXU_WIDTH, (g + 1) * MXU_WIDTH
                    it = {n: ref[bi, rows, lo:hi].astype(F32) for n, ref in zip(names, d_refs)}
                    it.update(reverse=d, s=s_scr[bi, d, g], at=(bi, d, g), rows=rows)
                    insts.append(it)
        for it, (s_new, y) in zip(insts, _scan_chunk(insts, with_y)):
            bi, d, g = it["at"]
            s_scr[bi, d, g] = s_new
            if with_y:
                y_refs[d][bi, it["rows"], g * MXU_WIDTH:(g + 1) * MXU_WIDTH] = y
        return carry

    lax.fori_loop(0, n_sub, sub_chunk, 0)

    @pl.when(c_idx == pl.num_programs(0) - 1)
    def _():
        sfin_ref[...] = s_scr[...]


def _wkv_scan(dirs, kap, v, r, s0):
    b, t, width = kap.shape
    rows = SCAN_SUB * CHUNK if t % (SCAN_SUB * CHUNK) == 0 else CHUNK
    nc = t // rows
    with_y = r is not None
    fwd = lambda ci: (0, ci, 0)
    bwd = lambda ci: (0, nc - 1 - ci, 0)
    ins, in_specs = [], []
    for d, imap in enumerate((fwd, bwd)):
        ld, kd, be = dirs[d]
        for arr in (ld, kap, kd, be, v) + ((r,) if with_y else ()):
            ins.append(arr)
            in_specs.append(pl.BlockSpec((b, rows, width), imap))
    n_groups = width // MXU_WIDTH
    s_shape = (b, 2, n_groups, MXU_WIDTH, MXU_WIDTH)
    s_spec = pl.BlockSpec(s_shape, lambda ci: (0, 0, 0, 0, 0))
    ins.append(s0)
    in_specs.append(s_spec)
    out_specs, out_shape = [], []
    if with_y:
        out_specs += [pl.BlockSpec((b, rows, width), fwd), pl.BlockSpec((b, rows, width), bwd)]
        out_shape += [jax.ShapeDtypeStruct((b, t, width), F32)] * 2
    out_specs.append(s_spec)
    out_shape.append(jax.ShapeDtypeStruct(s_shape, F32))
    return pl.pallas_call(
        functools.partial(_scan_kernel, with_y=with_y, width=width),
        grid=(nc,),
        in_specs=in_specs,
        out_specs=out_specs,
        out_shape=out_shape,
        scratch_shapes=[pltpu.VMEM(s_shape, F32)],
        compiler_params=_cparams(("arbitrary",)),
        name="wkv_scan" if with_y else "wkv_scan_ctx",
    )(*ins)


def _merge_kernel(yf_ref, yb_ref, bonus_ref, g_ref, zb_ref, p_ref, pp_ref, pn_ref, zg_ref, x_ref,
                  lnw_ref, lnb_ref, cw_ref, wout_ref, npost_ref, g1_ref, npre_ref, sc2_ref, sh2_ref,
                  wr_ref, br_ref,
                  x2_ref, h2_ref, gate_ref, pos_ref, nseg_ref, *, width):
    i = pl.program_id(1)
    tm = x_ref.shape[1]

    ones_bd = _block_diag_mask(MXU_WIDTH, HEAD_DIM).astype(BF16)
    row = lax.broadcasted_iota(jnp.int32, (tm, MXU_WIDTH), 0)
    halo = pp_ref.shape[1]
    parts = []
    for lo in range(0, width, MXU_WIDTH):
        hi = lo + MXU_WIDTH
        y = yf_ref[0, :, lo:hi] + yb_ref[0, :, lo:hi]
        mean = _head_sum(y, ones_bd) * (1.0 / HEAD_DIM)
        yc = y - mean
        var = _head_sum(yc * yc, ones_bd) * (1.0 / HEAD_DIM)
        yn = yc * lax.rsqrt(var + GN_EPS) * lnw_ref[:, lo:hi] + lnb_ref[:, lo:hi]
        y_rwkv = (yn + bonus_ref[0, :, lo:hi].astype(F32)) * g_ref[0, :, lo:hi].astype(F32)
        p = p_ref[0, :, lo:hi].astype(F32)
        p_prev = jnp.where(i == 0, 0.0, pp_ref[0, halo - 1:halo, lo:hi].astype(F32))
        p_next = jnp.where(i == pl.num_programs(1) - 1, 0.0, pn_ref[0, 0:1, lo:hi].astype(F32))
        before = jnp.where(row == 0, p_prev, pltpu.roll(p, 1, axis=0))
        after = jnp.where(row == tm - 1, p_next, pltpu.roll(p, tm - 1, axis=0))
        conv = cw_ref[0:1, lo:hi] * before + cw_ref[1:2, lo:hi] * p + cw_ref[2:3, lo:hi] * after
        y_conv = zb_ref[0, :, lo:hi].astype(F32) * conv
        ga = _sigmoid(zg_ref[0, :, lo:hi].astype(F32))
        gb = _sigmoid(zg_ref[0, :, width + lo:width + hi].astype(F32))
        parts.append((ga * y_rwkv + gb * y_conv).astype(BF16))
    merged = jnp.concatenate(parts, axis=1)
    mix = _dot(merged, wout_ref[...])
    x2 = x_ref[0] + g1_ref[0] * _rms(mix, npost_ref[...])
    x2_ref[0] = x2
    h2 = _rms(x2, npre_ref[...]) * (1.0 + sc2_ref[0]) + sh2_ref[0]
    h2_ref[0] = h2.astype(h2_ref.dtype)

    logits = _dot(h2, wr_ref[...]) + br_ref[...]
    lane = lax.broadcasted_iota(jnp.int32, logits.shape, 1)
    vals, idxs = [], []
    sel = jnp.zeros(logits.shape, F32)
    work = logits
    for _ in range(TOP_K):
        m = jnp.max(work, axis=-1, keepdims=True)
        idx = jnp.min(jnp.where(work == m, lane, logits.shape[1]), axis=-1, keepdims=True)
        hit = lane == idx
        vals.append(m)
        idxs.append(idx)
        sel = jnp.where(hit, 1.0, sel)
        work = jnp.where(hit, -jnp.inf, work)
    exps = [jnp.exp(vk - vals[0]) for vk in vals]
    denom = exps[0] + exps[1] + exps[2] + exps[3]
    r_i = lax.broadcasted_iota(jnp.int32, (tm, tm), 0)
    c_i = lax.broadcasted_iota(jnp.int32, (tm, tm), 1)
    before_cnt = _dot((c_i < r_i).astype(BF16), sel.astype(BF16))
    n_seg = jnp.floor((jnp.sum(sel, axis=0, keepdims=True) + (SEG_ROWS - 1.0)) * (1.0 / SEG_ROWS))
    e_r = lax.broadcasted_iota(jnp.int32, (logits.shape[1], logits.shape[1]), 0)
    e_c = lax.broadcasted_iota(jnp.int32, (logits.shape[1], logits.shape[1]), 1)
    seg_before = _dot(jnp.broadcast_to(n_seg, (8, logits.shape[1])).astype(BF16), (e_r < e_c).astype(BF16))[0:1]
    slot_base = before_cnt + seg_before * float(SEG_ROWS)
    gate_out = jnp.zeros(logits.shape, F32)
    pos_out = jnp.zeros(logits.shape, jnp.int32)
    for k in range(TOP_K):
        pos_k = jnp.sum(jnp.where(lane == idxs[k], slot_base, 0.0), axis=-1, keepdims=True)
        gate_out = jnp.where(lane == k, exps[k] / denom, gate_out)
        pos_out = jnp.where(lane == k, pos_k.astype(jnp.int32), pos_out)
    gate_ref[0] = gate_out
    pos_ref[0] = pos_out
    nseg_ref[0] = n_seg


def _merge_route(yf, yb, bonus, g, zb, p, zg, x, pr, tm):
    b, t, d = x.shape
    width = yf.shape[2]
    halo = 8
    hb = tm // halo
    last = t // halo - 1
    row = lambda bi, i: (bi, i, 0)
    const = lambda bi, i: (0, 0)
    per_b = lambda bi, i: (bi, 0, 0)
    prev = lambda bi, i: (bi, jnp.maximum(i * hb - 1, 0), 0)
    nxt = lambda bi, i: (bi, jnp.minimum((i + 1) * hb, last), 0)
    lanes = 128
    in_specs = [pl.BlockSpec((1, tm, width), row)] * 6
    in_specs += [pl.BlockSpec((1, halo, width), prev), pl.BlockSpec((1, halo, width), nxt),
                 pl.BlockSpec((1, tm, 2 * width), row), pl.BlockSpec((1, tm, d), row)]
    params = [pr["ln_x_w"], pr["ln_x_b"], pr["conv_w"], pr["w_out"], pr["norm_post_mix"]]
    in_specs += [pl.BlockSpec(a.shape, const) for a in params]
    in_specs.append(pl.BlockSpec((1, 1, d), per_b))
    in_specs.append(pl.BlockSpec(pr["norm_pre_ffn"].shape, const))
    in_specs += [pl.BlockSpec((1, 1, d), per_b)] * 2
    in_specs += [pl.BlockSpec(pr["w_router"].shape, const), pl.BlockSpec(pr["b_router"].shape, const)]
    nt = t // tm
    out_specs = [pl.BlockSpec((1, tm, d), row), pl.BlockSpec((1, tm, d), row)]
    out_specs += [pl.BlockSpec((1, tm, lanes), row)] * 2
    out_specs.append(pl.BlockSpec((1, 1, lanes), lambda bi, i: (bi * nt + i, 0, 0)))
    out_shape = [jax.ShapeDtypeStruct((b, t, d), F32), jax.ShapeDtypeStruct((b, t, d), F32),
                 jax.ShapeDtypeStruct((b, t, lanes), F32), jax.ShapeDtypeStruct((b, t, lanes), jnp.int32),
                 jax.ShapeDtypeStruct((b * nt, 1, lanes), F32)]
    return pl.pallas_call(
        functools.partial(_merge_kernel, width=width),
        grid=(b, nt),
        in_specs=in_specs,
        out_specs=out_specs,
        out_shape=out_shape,
        compiler_params=_cparams(("arbitrary", "arbitrary")),
        name="merge_route",
    )(yf, yb, bonus, g, zb, p, p, p, zg, x, *params, pr["g1"], pr["norm_pre_ffn"], pr["sc2"], pr["sh2"],
      pr["w_router"], pr["b_router"])


def _seg_copy(src_ref, src_row, dst_ref, dst_row, sem):
    al = lambda r: r if isinstance(r, int) else pl.multiple_of(r, SEG_ROWS)
    return pltpu.make_async_copy(src_ref.at[pl.ds(al(src_row), SEG_ROWS)],
                                 dst_ref.at[pl.ds(al(dst_row), SEG_ROWS)], sem)


def _sort_onehot(pos, n_sorted, values=None):
    col = lax.broadcasted_iota(jnp.int32, (pos.shape[0], n_sorted), 1)
    out = jnp.zeros((pos.shape[0], n_sorted), F32)
    for k in range(TOP_K):
        out = jnp.where(col == pos[:, k:k + 1], 1.0 if values is None else values[:, k:k + 1], out)
    return out.astype(BF16)


def _dispatch_kernel(dst_ref, nseg_ref, nrows_ref, h2_ref, pos_ref, xb_ref, zseg, sbuf, sem, zsem):
    step = pl.program_id(0)
    slot = step % 2
    n_fill_blocks = nrows_ref.shape[0]

    def fill_segments(fn):
        def per_block(j, carry):
            def one(q, c2):
                fn(_seg_copy(zseg, 0, xb_ref, j * MOE_BLOCK + q * SEG_ROWS, zsem))
                return c2
            return lax.fori_loop(nrows_ref[j] // SEG_ROWS, MOE_BLOCK // SEG_ROWS, one, carry)
        lax.fori_loop(0, n_fill_blocks, per_block, 0)

    @pl.when(step == 0)
    def _():
        zseg[...] = jnp.zeros_like(zseg)
        fill_segments(lambda cp: cp.start())

    onehot = _sort_onehot(pos_ref[...], sbuf.shape[1])
    sbuf[slot] = _dot_tn(onehot, h2_ref[...].astype(BF16))

    def segments(t_idx, s, fn):
        def one(g, carry):
            fn(_seg_copy(sbuf.at[s], g * SEG_ROWS, xb_ref, dst_ref[t_idx * SEG_MAX + g], sem.at[s]))
            return carry
        lax.fori_loop(0, nseg_ref[t_idx], one, 0)

    segments(step, slot, lambda cp: cp.start())

    @pl.when(step > 0)
    def _():
        segments(step - 1, 1 - slot, lambda cp: cp.wait())

    @pl.when(step == pl.num_programs(0) - 1)
    def _():
        segments(step, slot, lambda cp: cp.wait())
        fill_segments(lambda cp: cp.wait())


def _dispatch(h2, pos, dst_seg, n_seg, block_rows):
    n_tok, d = h2.shape
    n_rows = block_rows.shape[0] * MOE_BLOCK
    tile = lambda i, *_: (i, 0)
    grid_spec = pltpu.PrefetchScalarGridSpec(
        num_scalar_prefetch=3,
        grid=(n_tok // MOE_TILE,),
        in_specs=[pl.BlockSpec((MOE_TILE, d), tile), pl.BlockSpec((MOE_TILE, pos.shape[1]), tile)],
        out_specs=pl.BlockSpec(memory_space=pl.ANY),
        scratch_shapes=[pltpu.VMEM((SEG_ROWS, d), F32), pltpu.VMEM((2, SEG_MAX * SEG_ROWS, d), F32),
                        pltpu.SemaphoreType.DMA((2,)), pltpu.SemaphoreType.DMA],
    )
    return pl.pallas_call(
        _dispatch_kernel,
        grid_spec=grid_spec,
        out_shape=jax.ShapeDtypeStruct((n_rows, d), F32),
        compiler_params=_cparams(("arbitrary",)),
        name="moe_dispatch",
    )(dst_seg, n_seg, block_rows, h2, pos)


def _expert_kernel(be_ref, nrows_ref, first_ref, slot_ref, next_ref, xb_ref, win_hbm, bin_ref, wout_hbm, bout_ref,
                   o_ref, win_f32, wout_f32, win_scr, wout_scr, sem, *, d_ff):
    i = pl.program_id(0)
    n_rows = nrows_ref[i]

    def weight_copies(e, s):
        return (pltpu.make_async_copy(win_hbm.at[e], win_f32.at[s], sem.at[s, 0]),
                pltpu.make_async_copy(wout_hbm.at[e], wout_f32.at[s], sem.at[s, 1]))

    @pl.when(i == 0)
    def _():
        for cp in weight_copies(be_ref[0], 0):
            cp.start()

    @pl.when(first_ref[i] == 1)
    def _():
        s = slot_ref[i]
        for cp in weight_copies(be_ref[i], s):
            cp.wait()
        win_scr[...] = win_f32[s].astype(BF16)
        wout_scr[...] = wout_f32[s].astype(BF16)

        @pl.when(next_ref[i] >= 0)
        def _():
            for cp in weight_copies(next_ref[i], 1 - s):
                cp.start()

    @pl.when(n_rows > 0)
    def _():
        row = lax.broadcasted_iota(jnp.int32, xb_ref.shape, 0)
        xb = jnp.where(row < n_rows, xb_ref[...], 0.0).astype(BF16)
        gu = _dot(xb, win_scr[...]) + bin_ref[0]
        gate = jnp.minimum(gu[:, :d_ff], SWIGLU_LIMIT)
        up = jnp.clip(gu[:, d_ff:], -SWIGLU_LIMIT, SWIGLU_LIMIT)
        act = (up + 1.0) * gate * _sigmoid(SWIGLU_ALPHA * gate)
        o_ref[...] = _dot(act.astype(BF16), wout_scr[...]) + bout_ref[0]

    @pl.when(n_rows == 0)
    def _():
        o_ref[...] = jnp.zeros_like(o_ref)


def _experts(xb, block_e, block_rows, counts, w_in, b_in, w_out, b_out):
    n_rows, d = xb.shape
    n_e, _, d_ff2 = w_in.shape
    d_ff = d_ff2 // 2
    n_blocks = n_rows // MOE_BLOCK
    valid = block_rows > 0
    prev_e = jnp.concatenate([jnp.full((1,), -1, jnp.int32), block_e[:-1]])
    first = jnp.logical_and(valid, block_e != prev_e).astype(jnp.int32)
    slot = ((jnp.cumsum(first) - 1) % 2).astype(jnp.int32)
    experts = jnp.arange(n_e, dtype=jnp.int32)
    later = jnp.logical_and(experts[None, :] > experts[:, None], counts[None, :] > 0)
    next_of = jnp.min(jnp.where(later, experts[None, :], n_e), axis=1)
    next_of = jnp.where(next_of == n_e, -1, next_of)
    next_e = jnp.sum(jnp.where(block_e[:, None] == experts[None, :], next_of[None, :], 0), axis=1).astype(jnp.int32)
    blk = lambda i, *_: (i, 0)
    per_e = lambda i, be, *_: (be[i], 0, 0)
    grid_spec = pltpu.PrefetchScalarGridSpec(
        num_scalar_prefetch=5,
        grid=(n_blocks,),
        in_specs=[pl.BlockSpec((MOE_BLOCK, d), blk),
                  pl.BlockSpec(memory_space=pl.ANY),
                  pl.BlockSpec((1, 1, d_ff2), per_e),
                  pl.BlockSpec(memory_space=pl.ANY),
                  pl.BlockSpec((1, 1, d), per_e)],
        out_specs=pl.BlockSpec((MOE_BLOCK, d), blk),
        scratch_shapes=[pltpu.VMEM((2, d, d_ff2), F32), pltpu.VMEM((2, d_ff, d), F32),
                        pltpu.VMEM((d, d_ff2), BF16), pltpu.VMEM((d_ff, d), BF16),
                        pltpu.SemaphoreType.DMA((2, 2))],
    )
    return pl.pallas_call(
        functools.partial(_expert_kernel, d_ff=d_ff),
        grid_spec=grid_spec,
        out_shape=jax.ShapeDtypeStruct((n_rows, d), F32),
        compiler_params=_cparams(("arbitrary",)),
        name="moe_experts",
    )(block_e, block_rows, first, slot, next_e, xb, w_in, b_in.reshape(n_e, 1, d_ff2), w_out,
      b_out.reshape(n_e, 1, d))


def _combine_kernel(dst_ref, nseg_ref, yb_ref, gate_ref, pos_ref, x2_ref, npost_ref, g2_ref, o_ref, ybuf, sem):
    tile = pl.program_id(0) * pl.num_programs(1) + pl.program_id(1)
    n_tiles = pl.num_programs(0) * pl.num_programs(1)
    slot = tile % 2

    def segments(t_idx, s, fn):
        def one(g, carry):
            fn(_seg_copy(yb_ref, dst_ref[t_idx * SEG_MAX + g], ybuf.at[s], g * SEG_ROWS, sem.at[s]))
            return carry
        lax.fori_loop(0, nseg_ref[t_idx], one, 0)

    @pl.when(tile == 0)
    def _():
        ybuf[...] = jnp.zeros_like(ybuf)
        segments(0, 0, lambda cp: cp.start())

    @pl.when(tile + 1 < n_tiles)
    def _():
        segments(tile + 1, 1 - slot, lambda cp: cp.start())

    segments(tile, slot, lambda cp: cp.wait())
    weights = _sort_onehot(pos_ref[0], ybuf.shape[1], gate_ref[0])
    f = _dot(weights, ybuf[slot].astype(BF16))
    o_ref[0] = x2_ref[0] + g2_ref[0] * _rms(f, npost_ref[...])


def _combine(yb_rows, dst_seg, n_seg, gates, pos, x2, npost, g2):
    b, t, d = x2.shape
    row = lambda bi, i, *_: (bi, i, 0)
    grid_spec = pltpu.PrefetchScalarGridSpec(
        num_scalar_prefetch=2,
        grid=(b, t // MOE_TILE),
        in_specs=[pl.BlockSpec(memory_space=pl.ANY),
                  pl.BlockSpec((1, MOE_TILE, gates.shape[2]), row),
                  pl.BlockSpec((1, MOE_TILE, pos.shape[2]), row),
                  pl.BlockSpec((1, MOE_TILE, d), row),
                  pl.BlockSpec((1, d), lambda bi, i, *_: (0, 0)),
                  pl.BlockSpec((1, 1, d), lambda bi, i, *_: (bi, 0, 0))],
        out_specs=pl.BlockSpec((1, MOE_TILE, d), row),
        scratch_shapes=[pltpu.VMEM((2, SEG_MAX * SEG_ROWS, d), F32), pltpu.SemaphoreType.DMA((2,))],
    )
    return pl.pallas_call(
        _combine_kernel,
        grid_spec=grid_spec,
        out_shape=jax.ShapeDtypeStruct((b, t, d), F32),
        compiler_params=_cparams(("arbitrary", "arbitrary")),
        name="moe_combine",
    )(dst_seg, n_seg, yb_rows, gates, pos, x2, npost, g2)


def _lora_up(w_up):
    _, r, w = w_up.shape
    z = jnp.zeros((r, w), w_up.dtype)
    return jnp.concatenate([jnp.concatenate([w_up[0], z], axis=1),
                            jnp.concatenate([z, w_up[1]], axis=1)], axis=0).astype(BF16)


def _row_tile(t, pref):
    return pref if t % pref == 0 else t


def kernel(x, c, ctx, c_ctx, w_ada, b_ada, norm_pre_mix, norm_post_mix, norm_pre_ffn, norm_post_ffn,
           w_in, mu_shift, w0, w_decay_up, a0, w_iclr_up, k_k, k_a, r_k, w_gate_up, ln_x_w, ln_x_b,
           conv_w, w_out, w_router, b_router, w_exp_in, b_exp_in, w_exp_out, b_exp_out):
    b, t, d = x.shape
    t_ctx = ctx.shape[1]
    depth = w_ada.shape[0]
    width = k_k.shape[1]
    n_groups = width // MXU_WIDTH
    k_off, v_off = 0, width
    decay_off = 2 * width
    iclr_off = decay_off + 2 * DECAY_LORA
    r_off = iclr_off + 2 * ICLR_LORA
    glora_off = r_off + width
    conv_off = glora_off + GATE_LORA
    gate_off = conv_off + 3 * width

    xc = ctx
    for l in range(depth):
        last = l == depth - 1
        rows = jnp.concatenate([c, c_ctx[None, :], jnp.zeros((8 - b - 1, d), F32)], axis=0)
        mod = _ada(rows, w_ada[l], b_ada[l])
        sh1, sc1, g1, sh2, sc2, g2 = [mod[:b, None, j * d:(j + 1) * d] for j in range(6)]
        csh1, csc1, cg1, csh2, csc2, cg2 = [jnp.broadcast_to(mod[b:b + 1, None, j * d:(j + 1) * d], (b, 1, d))
                                            for j in range(6)]

        wl = w_in[l]
        cols = lambda lo, n: wl[:, lo:lo + n]
        w_main = jnp.concatenate([cols(k_off, width), cols(v_off, width), cols(r_off, width)], axis=1).astype(BF16)
        w_lora = jnp.concatenate([cols(decay_off, 2 * DECAY_LORA), cols(iclr_off, 2 * ICLR_LORA),
                                  cols(glora_off, GATE_LORA)], axis=1).astype(BF16)
        w_conv = tuple(cols(conv_off + j * width, width).astype(BF16) for j in range(3))
        w_gate = cols(gate_off, 2 * d).astype(BF16)
        mu = mu_shift[l]
        mu_main = jnp.concatenate([mu[k_off:k_off + width], mu[v_off:v_off + width], mu[r_off:r_off + width]])[None, :]
        mu_lora = jnp.concatenate([mu[decay_off:decay_off + 2 * DECAY_LORA], mu[iclr_off:iclr_off + 2 * ICLR_LORA],
                                   mu[glora_off:glora_off + GATE_LORA]])[None, :]
        sp = {
            "mu_main": mu_main, "mu_lora": mu_lora,
            "w0": w0[l].reshape(1, 2 * width), "w_dec": _lora_up(w_decay_up[l]),
            "a0": a0[l].reshape(1, 2 * width), "w_iclr": _lora_up(w_iclr_up[l]),
            "k_k": k_k[l][None, :], "k_a": k_a[l][None, :],
            "r_k": r_k[l].reshape(1, width), "w_gate": w_gate_up[l].astype(BF16),
        }
        if not last:
            raise NotImplementedError("context stream update for non-final layers")

        n_state_main = 2 * width
        zc_main, zc_lora = _inproj(xc, norm_pre_mix[l], csc1, csh1,
                                   [w_main[:, :n_state_main], w_lora[:, :2 * DECAY_LORA + 2 * ICLR_LORA]],
                                   None, _row_tile(t_ctx, 256))
        sp_ctx = dict(sp, mu_main=mu_main[:, :n_state_main], mu_lora=mu_lora[:, :2 * DECAY_LORA + 2 * ICLR_LORA])
        kap_c, v_c, ld0_c, ld1_c, kd0_c, kd1_c, be0_c, be1_c = _state_terms(
            zc_main, zc_lora, sp_ctx, grid_mode=False, with_read=False, tm=t_ctx)
        s0 = jnp.zeros((b, 2, n_groups, MXU_WIDTH, MXU_WIDTH), F32)
        (s_ctx,) = _wkv_scan([(ld0_c, kd0_c, be0_c), (ld1_c, kd1_c, be1_c)], kap_c, v_c, None, s0)

        zm, zl, zg, zb, p = _inproj(x, norm_pre_mix[l], sc1, sh1, [w_main, w_lora, w_gate], w_conv,
                                    _row_tile(t, 512))
        kap, v, ld0, ld1, kd0, kd1, be0, be1, r, bonus, g = _state_terms(
            zm, zl, sp, grid_mode=True, with_read=True, tm=_row_tile(t, 512))
        yf, yb, _ = _wkv_scan([(ld0, kd0, be0), (ld1, kd1, be1)], kap, v, r, s_ctx)

        lanes = 128
        pad_e = lanes - N_EXPERTS
        pr = {
            "ln_x_w": ln_x_w[l][None, :], "ln_x_b": ln_x_b[l][None, :], "conv_w": conv_w[l],
            "w_out": w_out[l].astype(BF16), "norm_post_mix": norm_post_mix[l][None, :], "g1": g1,
            "norm_pre_ffn": norm_pre_ffn[l][None, :], "sc2": sc2, "sh2": sh2,
            "w_router": jnp.pad(w_router[l], ((0, 0), (0, pad_e))),
            "b_router": jnp.pad(b_router[l], (0, pad_e), constant_values=-jnp.inf)[None, :],
        }
        x2, h2, gates, pos, tile_segs = _merge_route(yf, yb, bonus, g, zb, p, zg, x, pr, MOE_TILE)

        n_tok = b * t
        nk = n_tok * TOP_K
        n_tiles = n_tok // MOE_TILE
        experts = jnp.arange(N_EXPERTS, dtype=jnp.int32)
        run_rows = tile_segs[:, 0, :N_EXPERTS].astype(jnp.int32) * SEG_ROWS
        rows_before = jnp.cumsum(run_rows, axis=0) - run_rows
        counts = jnp.sum(run_rows, axis=0)
        padded = (counts + MOE_BLOCK - 1) // MOE_BLOCK * MOE_BLOCK
        pad_ends = jnp.cumsum(padded)
        pad_starts = pad_ends - padded
        run_off = jnp.cumsum(run_rows, axis=1) - run_rows
        n_seg = (jnp.sum(run_rows, axis=1) // SEG_ROWS).astype(jnp.int32)
        seg_row = jnp.arange(SEG_MAX, dtype=jnp.int32) * SEG_ROWS
        seg_e = jnp.minimum(jnp.sum(((run_off + run_rows)[:, None, :] <= seg_row[None, :, None]).astype(jnp.int32),
                                    axis=2), N_EXPERTS - 1)
        shift = pad_starts[None, :] + rows_before - run_off
        dst_seg = (jnp.sum(jnp.where(seg_e[..., None] == experts, shift[:, None, :], 0), axis=2)
                   + seg_row[None, :]).reshape(n_tiles * SEG_MAX).astype(jnp.int32)
        n_rows_max = nk + n_tiles * N_EXPERTS * (SEG_ROWS - 1) + N_EXPERTS * (MOE_BLOCK - 1)
        n_blocks = -(-n_rows_max // MOE_BLOCK)
        blk_start = jnp.arange(n_blocks, dtype=jnp.int32) * MOE_BLOCK
        block_e = jnp.minimum(jnp.sum((pad_ends[None, :] <= blk_start[:, None]).astype(jnp.int32), axis=1),
                              N_EXPERTS - 1)
        own = block_e[:, None] == experts[None, :]
        seg_end = jnp.sum(jnp.where(own, (pad_starts + counts)[None, :], 0), axis=1)
        block_rows = jnp.clip(seg_end - blk_start, 0, MOE_BLOCK).astype(jnp.int32)
        xb = _dispatch(h2.reshape(n_tok, d), pos.reshape(n_tok, pos.shape[2]), dst_seg, n_seg, block_rows)
        yb_rows = _experts(xb, block_e, block_rows, counts, w_exp_in[l], b_exp_in[l], w_exp_out[l], b_exp_out[l])
        x = _combine(yb_rows, dst_seg, n_seg, gates, pos, x2, norm_post_ffn[l][None, :], g2)
    return x
```

```python
import functools

import jax
import jax.numpy as jnp
from jax import lax
from jax.experimental import pallas as pl
from jax.experimental.pallas import tpu as pltpu

F32 = jnp.float32
BF16 = jnp.bfloat16

HEAD_DIM = 64
GRID_W = 64
DECAY_LORA = 64
ICLR_LORA = 64
GATE_LORA = 128
GN_EPS = 64e-5
NORM_EPS = 1e-12
CONV_K = 3
N_EXPERTS = 32
TOP_K = 4
SWIGLU_LIMIT = 7.0
SWIGLU_ALPHA = 1.702
MOE_BLOCK = 256
RMS_EPS = 1e-6

MXU_WIDTH = 256
HEADS_PER_GROUP = MXU_WIDTH // HEAD_DIM
CHUNK = 64
SCAN_SUB = 2
DECAY_SCALE = 0.6065306597126334
MOE_TILE = 256
EXPERT_SUB = 2
SEG_ROWS = 8
SEG_MAX = 160
assert SEG_MAX * SEG_ROWS >= MOE_TILE * TOP_K + N_EXPERTS * (SEG_ROWS - 1)
VMEM_LIMIT = 56 * 1024 * 1024


def _cparams(sem):
    return pltpu.CompilerParams(dimension_semantics=sem, vmem_limit_bytes=VMEM_LIMIT)


def _dot(a, b):
    return jnp.dot(a, b, preferred_element_type=F32)


def _dot_nt(a, b):
    return lax.dot_general(a, b, (((1,), (1,)), ((), ())), preferred_element_type=F32)


def _dot_tn(a, b):
    return lax.dot_general(a, b, (((0,), (0,)), ((), ())), preferred_element_type=F32)


def _split3(x):
    h = x.astype(BF16)
    r = x - h.astype(F32)
    m = r.astype(BF16)
    l = (r - m.astype(F32)).astype(BF16)
    return h, m, l


def _block_diag_mask(n, blk):
    r = lax.broadcasted_iota(jnp.int32, (n, n), 0) // blk
    c = lax.broadcasted_iota(jnp.int32, (n, n), 1) // blk
    return r == c


def _head_sum(x, ones_bd):
    h = x.astype(BF16)
    l = (x - h.astype(F32)).astype(BF16)
    return _dot(h, ones_bd) + _dot(l, ones_bd)


def _rms(x, gain):
    return x * lax.rsqrt(jnp.mean(x * x, axis=-1, keepdims=True) + RMS_EPS) * gain


def _sigmoid(x):
    return 1.0 / (1.0 + jnp.exp(-x))


def _ada_kernel(c_ref, w_ref, b_ref, o_ref):
    c = c_ref[...]
    s = c * _sigmoid(c)
    o_ref[...] = _dot(s, w_ref[...]) + b_ref[...]


def _ada(c_rows, w, b):
    rows, d = c_rows.shape
    n = w.shape[1]
    tn = 1024
    return pl.pallas_call(
        _ada_kernel,
        grid=(n // tn,),
        in_specs=[pl.BlockSpec((rows, d), lambda j: (0, 0)),
                  pl.BlockSpec((d, tn), lambda j: (0, j)),
                  pl.BlockSpec((1, tn), lambda j: (0, j))],
        out_specs=pl.BlockSpec((rows, tn), lambda j: (0, j)),
        out_shape=jax.ShapeDtypeStruct((rows, n), F32),
        compiler_params=_cparams(("arbitrary",)),
        name="ada_mod",
    )(c_rows, w, b.reshape(1, n))


def _inproj_kernel(x_ref, g_ref, sc_ref, sh_ref, *refs, n_plain, with_conv):
    n_w = n_plain + (3 if with_conv else 0)
    w_refs, o_refs = refs[:n_w], refs[n_w:]
    x = x_ref[0]
    h = _rms(x, g_ref[...]) * (1.0 + sc_ref[0]) + sh_ref[0]
    hb = h.astype(BF16)
    col = 512
    for w_ref, o_ref in zip(w_refs[:n_plain], o_refs[:n_plain]):
        n = w_ref.shape[1]
        for j in range(0, n, col):
            e = min(j + col, n)
            o_ref[0, :, j:e] = _dot(hb, w_ref[:, j:e]).astype(o_ref.dtype)
    if with_conv:
        wb_ref, wc_ref, wx_ref = w_refs[n_plain:]
        ob_ref, op_ref = o_refs[n_plain:]
        n = wb_ref.shape[1]
        for j in range(0, n, col):
            e = min(j + col, n)
            ob_ref[0, :, j:e] = _dot(hb, wb_ref[:, j:e]).astype(ob_ref.dtype)
            op_ref[0, :, j:e] = (_dot(hb, wc_ref[:, j:e]) * _dot(hb, wx_ref[:, j:e])).astype(op_ref.dtype)


def _inproj(x, gain, sc, sh, plain_ws, conv_ws, tm):
    b, t, d = x.shape
    with_conv = conv_ws is not None
    ws = list(plain_ws) + (list(conv_ws) if with_conv else [])
    out_w = [w.shape[1] for w in plain_ws] + ([conv_ws[0].shape[1]] * 2 if with_conv else [])
    row = lambda bi, i: (bi, i, 0)
    const = lambda bi, i: (0, 0)
    in_specs = [pl.BlockSpec((1, tm, d), row),
                pl.BlockSpec((1, d), const),
                pl.BlockSpec((1, 1, d), lambda bi, i: (bi, 0, 0)),
                pl.BlockSpec((1, 1, d), lambda bi, i: (bi, 0, 0))]
    in_specs += [pl.BlockSpec(w.shape, const, pipeline_mode=pl.Buffered(1)) for w in ws]
    return pl.pallas_call(
        functools.partial(_inproj_kernel, n_plain=len(plain_ws), with_conv=with_conv),
        grid=(b, t // tm),
        in_specs=in_specs,
        out_specs=[pl.BlockSpec((1, tm, n), row) for n in out_w],
        out_shape=[jax.ShapeDtypeStruct((b, t, n), BF16) for n in out_w],
        compiler_params=_cparams(("arbitrary", "arbitrary")),
        name="in_proj",
    )(x, gain.reshape(1, d), sc, sh, *ws)


def _shift_mix(cur, prev, nxt, mu, grid_mode, is_first, is_last):
    tm, w = cur.shape
    row = lax.broadcasted_iota(jnp.int32, (tm, w), 0)
    grp = lax.broadcasted_iota(jnp.int32, (tm, w), 1) % 4
    back = pltpu.roll(cur, 1, axis=0)
    fwd = pltpu.roll(cur, tm - 1, axis=0)
    if grid_mode:
        colpos = row % GRID_W
        left = jnp.where(colpos == 0, 0.0, back)
        right = jnp.where(colpos == GRID_W - 1, 0.0, fwd)
        prev = jnp.where(is_first, 0.0, prev)
        nxt = jnp.where(is_last, 0.0, nxt)
        up = jnp.concatenate([prev, cur[:tm - GRID_W]], axis=0)
        down = jnp.concatenate([cur[GRID_W:], nxt], axis=0)
        shifted = jnp.where(grp == 0, left, jnp.where(grp == 1, right, jnp.where(grp == 2, up, down)))
    else:
        before = jnp.where(row == 0, 0.0, back)
        after = jnp.where(row == tm - 1, 0.0, fwd)
        shifted = jnp.where(grp % 2 == 0, before, after)
    return cur + mu * (shifted - cur)


def _state_kernel(*refs, grid_mode, with_read, width):
    if grid_mode:
        (zm_ref, zmp_ref, zmn_ref, zl_ref, zlp_ref, zln_ref), refs = refs[:6], refs[6:]
    else:
        (zm_ref, zl_ref), refs = refs[:2], refs[2:]
        zmp_ref = zmn_ref = zlp_ref = zln_ref = None
    (mum_ref, mul_ref, w0_ref, wdec_ref, a0_ref, wic_ref, kk_ref, ka_ref), refs = refs[:8], refs[8:]
    if with_read:
        (rk_ref, wg_ref), refs = refs[:2], refs[2:]
    (kap_ref, v_ref, ld0_ref, ld1_ref, kd0_ref, kd1_ref, be0_ref, be1_ref), refs = refs[:8], refs[8:]
    if with_read:
        r_ref, bonus_ref, g_ref = refs

    i = pl.program_id(1)
    is_first = i == 0
    is_last = i == pl.num_programs(1) - 1
    ones_bd = _block_diag_mask(MXU_WIDTH, HEAD_DIM).astype(BF16)

    def shifted(z_ref, zp_ref, zn_ref, mu_ref, lo, hi):
        cur = z_ref[0, :, lo:hi].astype(F32)
        prev = zp_ref[0, :, lo:hi].astype(F32) if grid_mode else None
        nxt = zn_ref[0, :, lo:hi].astype(F32) if grid_mode else None
        return _shift_mix(cur, prev, nxt, mu_ref[:, lo:hi], grid_mode, is_first, is_last)

    lora = shifted(zl_ref, zlp_ref, zln_ref, mul_ref, 0, zl_ref.shape[2])
    lw = jnp.tanh(lora[:, :2 * DECAY_LORA]).astype(BF16)
    la = lora[:, 2 * DECAY_LORA:2 * DECAY_LORA + 2 * ICLR_LORA].astype(BF16)
    if with_read:
        lg = _sigmoid(lora[:, 2 * DECAY_LORA + 2 * ICLR_LORA:]).astype(BF16)

    for lo in range(0, width, MXU_WIDTH):
        hi = lo + MXU_WIDTH
        k = shifted(zm_ref, zmp_ref, zmn_ref, mum_ref, lo, hi)
        v = shifted(zm_ref, zmp_ref, zmn_ref, mum_ref, width + lo, width + hi)
        kk = k * kk_ref[:, lo:hi]
        n2 = _head_sum(kk * kk, ones_bd)
        kk = kk / jnp.maximum(jnp.sqrt(n2), NORM_EPS)
        kap_ref[0, :, lo:hi] = kk.astype(kap_ref.dtype)
        v_ref[0, :, lo:hi] = v.astype(v_ref.dtype)
        a_sum = None
        for d, (ld_ref, kd_ref, be_ref) in enumerate(((ld0_ref, kd0_ref, be0_ref), (ld1_ref, kd1_ref, be1_ref))):
            lo_d, hi_d = d * width + lo, d * width + hi
            pre_w = w0_ref[:, lo_d:hi_d] + _dot(lw, wdec_ref[:, lo_d:hi_d])
            ld_ref[0, :, lo:hi] = -DECAY_SCALE * _sigmoid(pre_w)
            a = _sigmoid(a0_ref[:, lo_d:hi_d] + _dot(la, wic_ref[:, lo_d:hi_d]))
            kd_ref[0, :, lo:hi] = (k * (1.0 + (a - 1.0) * ka_ref[:, lo:hi])).astype(kd_ref.dtype)
            be_ref[0, :, lo:hi] = (kk * a).astype(be_ref.dtype)
            a_sum = a if a_sum is None else a_sum + a
        if with_read:
            r = shifted(zm_ref, zmp_ref, zmn_ref, mum_ref, 2 * width + lo, 2 * width + hi)
            r_ref[0, :, lo:hi] = r.astype(r_ref.dtype)
            k_bonus = k * (1.0 + (0.5 * a_sum - 1.0) * ka_ref[:, lo:hi])
            s = _head_sum(r * k_bonus * rk_ref[:, lo:hi], ones_bd)
            bonus_ref[0, :, lo:hi] = (s * v).astype(bonus_ref.dtype)
            g_ref[0, :, lo:hi] = _dot(lg, wg_ref[:, lo:hi]).astype(g_ref.dtype)


def _state_terms(zm, zl, p, grid_mode, with_read, tm):
    b, t, _ = zm.shape
    width = p["k_k"].shape[1]
    nt = t // tm
    row = lambda bi, i: (bi, i, 0)
    const = lambda bi, i: (0, 0)
    ins, in_specs = [], []
    if grid_mode:
        hb = tm // GRID_W
        last = t // GRID_W - 1
        prev = lambda bi, i: (bi, jnp.maximum(i * hb - 1, 0), 0)
        nxt = lambda bi, i: (bi, jnp.minimum((i + 1) * hb, last), 0)
        for z in (zm, zl):
            ins += [z, z, z]
            in_specs += [pl.BlockSpec((1, tm, z.shape[2]), row),
                         pl.BlockSpec((1, GRID_W, z.shape[2]), prev),
                         pl.BlockSpec((1, GRID_W, z.shape[2]), nxt)]
    else:
        for z in (zm, zl):
            ins.append(z)
            in_specs.append(pl.BlockSpec((1, tm, z.shape[2]), row))
    names = ["mu_main", "mu_lora", "w0", "w_dec", "a0", "w_iclr", "k_k", "k_a"]
    if with_read:
        names += ["r_k", "w_gate"]
    for n in names:
        ins.append(p[n])
        in_specs.append(pl.BlockSpec(p[n].shape, const))
    out_dt = [BF16, BF16, F32, F32, BF16, BF16, BF16, BF16] + ([BF16] * 3 if with_read else [])
    return pl.pallas_call(
        functools.partial(_state_kernel, grid_mode=grid_mode, with_read=with_read, width=width),
        grid=(b, nt),
        in_specs=in_specs,
        out_specs=[pl.BlockSpec((1, tm, width), row) for _ in out_dt],
        out_shape=[jax.ShapeDtypeStruct((b, t, width), dt) for dt in out_dt],
        compiler_params=_cparams(("arbitrary", "arbitrary")),
        name="state_terms_grid" if grid_mode else "state_terms_seq",
    )(*ins)


def _pack_bd(y, bd_mask):
    reps = MXU_WIDTH // y.shape[0]
    return jnp.where(bd_mask, jnp.concatenate([y] * reps, axis=0), jnp.zeros((), y.dtype))


def _scan_chunk(insts, with_y):
    c = CHUNK
    bd_mask = _block_diag_mask(MXU_WIDTH, HEAD_DIM)
    ti = lax.broadcasted_iota(jnp.int32, (c, c), 0)
    tj = lax.broadcasted_iota(jnp.int32, (c, c), 1)
    t4 = lax.broadcasted_iota(jnp.int32, (c, MXU_WIDTH), 0)
    i4 = lax.broadcasted_iota(jnp.int32, (c, MXU_WIDTH), 1) % c
    eye4 = (t4 == i4).astype(F32)
    tri = ((tj <= ti).astype(BF16), (tj >= ti).astype(BF16))
    strict4 = (i4 < t4, i4 > t4)
    incl4 = (i4 <= t4, i4 >= t4)
    pack = lambda y: _pack_bd(y, bd_mask)

    for it in insts:
        rev = it["reverse"]
        ld = it["ld"]
        cl = sum(_dot(tri[rev], t) for t in _split3(ld))
        cl_end = cl[0:1] if rev else cl[c - 1:c]
        rel = cl - cl_end
        e_k = jnp.exp(-rel)
        it["kt"] = (it["kd"] * e_k).astype(BF16)
        it["bt"] = (it["be"] * e_k).astype(BF16)
        it["vb"] = it["v"].astype(BF16)
        kq = (it["kap"] * jnp.exp(rel - ld)).astype(BF16)
        it["lhs"] = jnp.concatenate([kq, (it["r"] * jnp.exp(rel)).astype(BF16)], axis=0) if with_y else kq
        it["sd"] = it["s"] * jnp.exp(cl_end)
    for it in insts:
        it["st"] = _dot_nt(it["lhs"], it["sd"].astype(BF16))
        it["a_b"] = _dot_nt(it["lhs"], pack(it["bt"]))
        it["a_k"] = _dot_nt(it["lhs"], pack(it["kt"]))

    for it in insts:
        a = jnp.where(strict4[it["reverse"]], it["a_b"][:c], 0.0)
        it["t_inv"] = eye4 - a
        ab = a.astype(BF16)
        it["pw"] = _dot(ab, pack(ab)).astype(BF16)
    n = 2
    while 2 * n < c:
        for it in insts:
            both = _dot(jnp.concatenate([it["pw"], it["t_inv"].astype(BF16)], axis=0), pack(it["pw"]))
            it["pw"] = both[:c].astype(BF16)
            it["t_inv"] = it["t_inv"] + both[c:]
        n *= 2
    for it in insts:
        it["t_inv"] = it["t_inv"] + _dot(it["t_inv"].astype(BF16), pack(it["pw"]))

    for it in insts:
        rev = it["reverse"]
        a_kk = jnp.where(strict4[rev], it["a_k"][:c], 0.0)
        if with_y:
            a_rk = jnp.where(incl4[rev], it["a_k"][c:], 0.0)
            it["av"] = _dot(jnp.concatenate([a_kk, a_rk], axis=0).astype(BF16), pack(it["vb"]))
        else:
            it["av"] = _dot(a_kk.astype(BF16), pack(it["vb"]))
    for it in insts:
        rhs = it["st"][:c] + it["av"][:c]
        it["ub"] = _dot(it["t_inv"].astype(BF16), pack(rhs.astype(BF16))).astype(BF16)
    out = []
    for it in insts:
        y = None
        if with_y:
            a_rb = jnp.where(incl4[it["reverse"]], it["a_b"][c:], 0.0)
            y = it["st"][c:] + it["av"][c:] - _dot(a_rb.astype(BF16), pack(it["ub"]))
        delta = _dot_tn(jnp.concatenate([it["vb"], it["ub"]], axis=0),
                        jnp.concatenate([it["kt"], -it["bt"]], axis=0))
        out.append((it["sd"] + jnp.where(bd_mask, delta, 0.0), y))
    return out


def _scan_kernel(*refs, with_y, width):
    n_in = 12 if with_y else 10
    in_refs, refs = refs[:n_in], refs[n_in:]
    s0_ref, refs = refs[0], refs[1:]
    if with_y:
        y_refs, refs = refs[:2], refs[2:]
    sfin_ref, s_scr = refs
    c_idx = pl.program_id(0)
    n_groups = width // MXU_WIDTH

    @pl.when(c_idx == 0)
    def _():
        s_scr[...] = s0_ref[...]

    per_dir = 6 if with_y else 5
    names = ("ld", "kap", "kd", "be", "v", "r")[:per_dir]
    n_batch = in_refs[0].shape[0]
    n_sub = in_refs[0].shape[1] // CHUNK

    def sub_chunk(j, carry):
        insts = []
        for bi in range(n_batch):
            for d in range(2):
                rows = pl.ds(pl.multiple_of((j if d == 0 else n_sub - 1 - j) * CHUNK, CHUNK), CHUNK)
                d_refs = in_refs[d * per_dir:(d + 1) * per_dir]
                for g in range(n_groups):
                    lo, hi = g * MXU_WIDTH, (g + 1) * MXU_WIDTH
                    it = {n: ref[bi, rows, lo:hi].astype(F32) for n, ref in zip(names, d_refs)}
                    it.update(reverse=d, s=s_scr[bi, d, g], at=(bi, d, g), rows=rows)
                    insts.append(it)
        for it, (s_new, y) in zip(insts, _scan_chunk(insts, with_y)):
            bi, d, g = it["at"]
            s_scr[bi, d, g] = s_new
            if with_y:
                y_refs[d][bi, it["rows"], g * MXU_WIDTH:(g + 1) * MXU_WIDTH] = y
        return carry

    lax.fori_loop(0, n_sub, sub_chunk, 0)

    @pl.when(c_idx == pl.num_programs(0) - 1)
    def _():
        sfin_ref[...] = s_scr[...]


def _wkv_scan(dirs, kap, v, r, s0):
    b, t, width = kap.shape
    rows = SCAN_SUB * CHUNK if t % (SCAN_SUB * CHUNK) == 0 else CHUNK
    nc = t // rows
    with_y = r is not None
    fwd = lambda ci: (0, ci, 0)
    bwd = lambda ci: (0, nc - 1 - ci, 0)
    ins, in_specs = [], []
    for d, imap in enumerate((fwd, bwd)):
        ld, kd, be = dirs[d]
        for arr in (ld, kap, kd, be, v) + ((r,) if with_y else ()):
            ins.append(arr)
            in_specs.append(pl.BlockSpec((b, rows, width), imap))
    n_groups = width // MXU_WIDTH
    s_shape = (b, 2, n_groups, MXU_WIDTH, MXU_WIDTH)
    s_spec = pl.BlockSpec(s_shape, lambda ci: (0, 0, 0, 0, 0))
    ins.append(s0)
    in_specs.append(s_spec)
    out_specs, out_shape = [], []
    if with_y:
        out_specs += [pl.BlockSpec((b, rows, width), fwd), pl.BlockSpec((b, rows, width), bwd)]
        out_shape += [jax.ShapeDtypeStruct((b, t, width), F32)] * 2
    out_specs.append(s_spec)
    out_shape.append(jax.ShapeDtypeStruct(s_shape, F32))
    return pl.pallas_call(
        functools.partial(_scan_kernel, with_y=with_y, width=width),
        grid=(nc,),
        in_specs=in_specs,
        out_specs=out_specs,
        out_shape=out_shape,
        scratch_shapes=[pltpu.VMEM(s_shape, F32)],
        compiler_params=_cparams(("arbitrary",)),
        name="wkv_scan" if with_y else "wkv_scan_ctx",
    )(*ins)


def _merge_kernel(yf_ref, yb_ref, bonus_ref, g_ref, zb_ref, p_ref, pp_ref, pn_ref, zg_ref, x_ref,
                  lnw_ref, lnb_ref, cw_ref, wout_ref, npost_ref, g1_ref, npre_ref, sc2_ref, sh2_ref,
                  wr_ref, br_ref,
                  x2_ref, h2_ref, gate_ref, pos_ref, nseg_ref, *, width):
    i = pl.program_id(1)
    tm = x_ref.shape[1]

    ones_bd = _block_diag_mask(MXU_WIDTH, HEAD_DIM).astype(BF16)
    row = lax.broadcasted_iota(jnp.int32, (tm, MXU_WIDTH), 0)
    halo = pp_ref.shape[1]
    parts = []
    for lo in range(0, width, MXU_WIDTH):
        hi = lo + MXU_WIDTH
        y = yf_ref[0, :, lo:hi] + yb_ref[0, :, lo:hi]
        mean = _head_sum(y, ones_bd) * (1.0 / HEAD_DIM)
        yc = y - mean
        var = _head_sum(yc * yc, ones_bd) * (1.0 / HEAD_DIM)
        yn = yc * lax.rsqrt(var + GN_EPS) * lnw_ref[:, lo:hi] + lnb_ref[:, lo:hi]
        y_rwkv = (yn + bonus_ref[0, :, lo:hi].astype(F32)) * g_ref[0, :, lo:hi].astype(F32)
        p = p_ref[0, :, lo:hi].astype(F32)
        p_prev = jnp.where(i == 0, 0.0, pp_ref[0, halo - 1:halo, lo:hi].astype(F32))
        p_next = jnp.where(i == pl.num_programs(1) - 1, 0.0, pn_ref[0, 0:1, lo:hi].astype(F32))
        before = jnp.where(row == 0, p_prev, pltpu.roll(p, 1, axis=0))
        after = jnp.where(row == tm - 1, p_next, pltpu.roll(p, tm - 1, axis=0))
        conv = cw_ref[0:1, lo:hi] * before + cw_ref[1:2, lo:hi] * p + cw_ref[2:3, lo:hi] * after
        y_conv = zb_ref[0, :, lo:hi].astype(F32) * conv
        ga = _sigmoid(zg_ref[0, :, lo:hi].astype(F32))
        gb = _sigmoid(zg_ref[0, :, width + lo:width + hi].astype(F32))
        parts.append((ga * y_rwkv + gb * y_conv).astype(BF16))
    merged = jnp.concatenate(parts, axis=1)
    mix = _dot(merged, wout_ref[...])
    x2 = x_ref[0] + g1_ref[0] * _rms(mix, npost_ref[...])
    x2_ref[0] = x2
    h2 = _rms(x2, npre_ref[...]) * (1.0 + sc2_ref[0]) + sh2_ref[0]
    h2_ref[0] = h2.astype(h2_ref.dtype)

    logits = _dot(h2, wr_ref[...]) + br_ref[...]
    lane = lax.broadcasted_iota(jnp.int32, logits.shape, 1)
    vals, idxs = [], []
    sel = jnp.zeros(logits.shape, F32)
    work = logits
    for _ in range(TOP_K):
        m = jnp.max(work, axis=-1, keepdims=True)
        idx = jnp.min(jnp.where(work == m, lane, logits.shape[1]), axis=-1, keepdims=True)
        hit = lane == idx
        vals.append(m)
        idxs.append(idx)
        sel = jnp.where(hit, 1.0, sel)
        work = jnp.where(hit, -jnp.inf, work)
    exps = [jnp.exp(vk - vals[0]) for vk in vals]
    denom = exps[0] + exps[1] + exps[2] + exps[3]
    r_i = lax.broadcasted_iota(jnp.int32, (tm, tm), 0)
    c_i = lax.broadcasted_iota(jnp.int32, (tm, tm), 1)
    before_cnt = _dot((c_i < r_i).astype(BF16), sel.astype(BF16))
    n_seg = jnp.floor((jnp.sum(sel, axis=0, keepdims=True) + (SEG_ROWS - 1.0)) * (1.0 / SEG_ROWS))
    e_r = lax.broadcasted_iota(jnp.int32, (logits.shape[1], logits.shape[1]), 0)
    e_c = lax.broadcasted_iota(jnp.int32, (logits.shape[1], logits.shape[1]), 1)
    seg_before = _dot(jnp.broadcast_to(n_seg, (8, logits.shape[1])).astype(BF16), (e_r < e_c).astype(BF16))[0:1]
    slot_base = before_cnt + seg_before * float(SEG_ROWS)
    gate_out = jnp.zeros(logits.shape, F32)
    pos_out = jnp.zeros(logits.shape, jnp.int32)
    for k in range(TOP_K):
        pos_k = jnp.sum(jnp.where(lane == idxs[k], slot_base, 0.0), axis=-1, keepdims=True)
        gate_out = jnp.where(lane == k, exps[k] / denom, gate_out)
        pos_out = jnp.where(lane == k, pos_k.astype(jnp.int32), pos_out)
    gate_ref[0] = gate_out
    pos_ref[0] = pos_out
    nseg_ref[0] = n_seg


def _merge_route(yf, yb, bonus, g, zb, p, zg, x, pr, tm):
    b, t, d = x.shape
    width = yf.shape[2]
    halo = 8
    hb = tm // halo
    last = t // halo - 1
    row = lambda bi, i: (bi, i, 0)
    const = lambda bi, i: (0, 0)
    per_b = lambda bi, i: (bi, 0, 0)
    prev = lambda bi, i: (bi, jnp.maximum(i * hb - 1, 0), 0)
    nxt = lambda bi, i: (bi, jnp.minimum((i + 1) * hb, last), 0)
    lanes = 128
    in_specs = [pl.BlockSpec((1, tm, width), row)] * 6
    in_specs += [pl.BlockSpec((1, halo, width), prev), pl.BlockSpec((1, halo, width), nxt),
                 pl.BlockSpec((1, tm, 2 * width), row), pl.BlockSpec((1, tm, d), row)]
    params = [pr["ln_x_w"], pr["ln_x_b"], pr["conv_w"], pr["w_out"], pr["norm_post_mix"]]
    in_specs += [pl.BlockSpec(a.shape, const) for a in params]
    in_specs.append(pl.BlockSpec((1, 1, d), per_b))
    in_specs.append(pl.BlockSpec(pr["norm_pre_ffn"].shape, const))
    in_specs += [pl.BlockSpec((1, 1, d), per_b)] * 2
    in_specs += [pl.BlockSpec(pr["w_router"].shape, const), pl.BlockSpec(pr["b_router"].shape, const)]
    nt = t // tm
    out_specs = [pl.BlockSpec((1, tm, d), row), pl.BlockSpec((1, tm, d), row)]
    out_specs += [pl.BlockSpec((1, tm, lanes), row)] * 2
    out_specs.append(pl.BlockSpec((1, 1, lanes), lambda bi, i: (bi * nt + i, 0, 0)))
    out_shape = [jax.ShapeDtypeStruct((b, t, d), F32), jax.ShapeDtypeStruct((b, t, d), F32),
                 jax.ShapeDtypeStruct((b, t, lanes), F32), jax.ShapeDtypeStruct((b, t, lanes), jnp.int32),
                 jax.ShapeDtypeStruct((b * nt, 1, lanes), F32)]
    return pl.pallas_call(
        functools.partial(_merge_kernel, width=width),
        grid=(b, nt),
        in_specs=in_specs,
        out_specs=out_specs,
        out_shape=out_shape,
        compiler_params=_cparams(("arbitrary", "arbitrary")),
        name="merge_route",
    )(yf, yb, bonus, g, zb, p, p, p, zg, x, *params, pr["g1"], pr["norm_pre_ffn"], pr["sc2"], pr["sh2"],
      pr["w_router"], pr["b_router"])


def _seg_copy(src_ref, src_row, dst_ref, dst_row, sem):
    al = lambda r: r if isinstance(r, int) else pl.multiple_of(r, SEG_ROWS)
    return pltpu.make_async_copy(src_ref.at[pl.ds(al(src_row), SEG_ROWS)],
                                 dst_ref.at[pl.ds(al(dst_row), SEG_ROWS)], sem)


def _start(cp, priority):
    cp.start(priority=priority)


def _wait(cp, priority):
    del priority
    cp.wait()


def _for_segment_pairs(n, copy, fn):
    def pair(i, carry):
        fn(copy(2 * i), 0)

        @pl.when(2 * i + 1 < n)
        def _():
            fn(copy(2 * i + 1), 1)
        return carry
    lax.fori_loop(0, (n + 1) // 2, pair, 0)


def _sort_onehot(pos, n_sorted, values=None):
    col = lax.broadcasted_iota(jnp.int32, (pos.shape[0], n_sorted), 1)
    out = jnp.zeros((pos.shape[0], n_sorted), F32)
    for k in range(TOP_K):
        out = jnp.where(col == pos[:, k:k + 1], 1.0 if values is None else values[:, k:k + 1], out)
    return out.astype(BF16)


def _dispatch_kernel(dst_ref, nseg_ref, nrows_ref, h2_ref, pos_ref, xb_ref, zseg, sbuf, sem, zsem):
    step = pl.program_id(0)
    slot = step % 2
    n_fill_blocks = nrows_ref.shape[0]

    def fill_segments(fn):
        def per_block(j, carry):
            def one(q, c2):
                fn(_seg_copy(zseg, 0, xb_ref, j * MOE_BLOCK + q * SEG_ROWS, zsem))
                return c2
            return lax.fori_loop(nrows_ref[j] // SEG_ROWS, MOE_BLOCK // SEG_ROWS, one, carry)
        lax.fori_loop(0, n_fill_blocks, per_block, 0)

    @pl.when(step == 0)
    def _():
        zseg[...] = jnp.zeros_like(zseg)
        fill_segments(lambda cp: cp.start())

    onehot = _sort_onehot(pos_ref[...], sbuf.shape[1])
    sbuf[slot] = _dot_tn(onehot, h2_ref[...].astype(BF16))

    def segments(t_idx, s, fn):
        copy = lambda g: _seg_copy(sbuf.at[s], g * SEG_ROWS, xb_ref, dst_ref[t_idx * SEG_MAX + g], sem.at[s])
        _for_segment_pairs(nseg_ref[t_idx], copy, fn)

    segments(step, slot, _start)

    @pl.when(step > 0)
    def _():
        segments(step - 1, 1 - slot, _wait)

    @pl.when(step == pl.num_programs(0) - 1)
    def _():
        segments(step, slot, _wait)
        fill_segments(lambda cp: cp.wait())


def _dispatch(h2, pos, dst_seg, n_seg, block_rows):
    n_tok, d = h2.shape
    n_rows = block_rows.shape[0] * MOE_BLOCK
    tile = lambda i, *_: (i, 0)
    grid_spec = pltpu.PrefetchScalarGridSpec(
        num_scalar_prefetch=3,
        grid=(n_tok // MOE_TILE,),
        in_specs=[pl.BlockSpec((MOE_TILE, d), tile), pl.BlockSpec((MOE_TILE, pos.shape[1]), tile)],
        out_specs=pl.BlockSpec(memory_space=pl.ANY),
        scratch_shapes=[pltpu.VMEM((SEG_ROWS, d), F32), pltpu.VMEM((2, SEG_MAX * SEG_ROWS, d), F32),
                        pltpu.SemaphoreType.DMA((2,)), pltpu.SemaphoreType.DMA],
    )
    return pl.pallas_call(
        _dispatch_kernel,
        grid_spec=grid_spec,
        out_shape=jax.ShapeDtypeStruct((n_rows, d), F32),
        compiler_params=_cparams(("arbitrary",)),
        name="moe_dispatch",
    )(dst_seg, n_seg, block_rows, h2, pos)


def _expert_kernel(be_ref, nrows_ref, first_ref, slot_ref, next_ref, xb_ref, win_hbm, bin_ref, wout_hbm, bout_ref,
                   o_ref, win_f32, wout_f32, win_scr, wout_scr, sem, *, d_ff):
    step = pl.program_id(0)
    n_sub = xb_ref.shape[0] // MOE_BLOCK

    def weight_copies(e, s):
        return (pltpu.make_async_copy(win_hbm.at[e], win_f32.at[s], sem.at[s, 0]),
                pltpu.make_async_copy(wout_hbm.at[e], wout_f32.at[s], sem.at[s, 1]))

    @pl.when(step == 0)
    def _():
        for cp in weight_copies(be_ref[0], 0):
            cp.start()

    def block(sub, carry):
        i = step * n_sub + sub
        rows = pl.ds(pl.multiple_of(sub * MOE_BLOCK, MOE_BLOCK), MOE_BLOCK)
        n_rows = nrows_ref[i]
        e = be_ref[i]

        @pl.when(first_ref[i] == 1)
        def _():
            s = slot_ref[i]
            for cp in weight_copies(e, s):
                cp.wait()
            win_scr[...] = win_f32[s].astype(BF16)
            wout_scr[...] = wout_f32[s].astype(BF16)

            @pl.when(next_ref[i] >= 0)
            def _():
                for cp in weight_copies(next_ref[i], 1 - s):
                    cp.start()

        @pl.when(n_rows > 0)
        def _():
            row = lax.broadcasted_iota(jnp.int32, (MOE_BLOCK, xb_ref.shape[1]), 0)
            xb = jnp.where(row < n_rows, xb_ref[rows, :], 0.0).astype(BF16)
            gu = _dot(xb, win_scr[...]) + bin_ref[e]
            gate = jnp.minimum(gu[:, :d_ff], SWIGLU_LIMIT)
            up = jnp.clip(gu[:, d_ff:], -SWIGLU_LIMIT, SWIGLU_LIMIT)
            act = (up + 1.0) * gate * _sigmoid(SWIGLU_ALPHA * gate)
            o_ref[rows, :] = _dot(act.astype(BF16), wout_scr[...]) + bout_ref[e]

        @pl.when(n_rows == 0)
        def _():
            o_ref[rows, :] = jnp.zeros((MOE_BLOCK, o_ref.shape[1]), o_ref.dtype)
        return carry

    lax.fori_loop(0, n_sub, block, 0)


def _experts(xb, block_e, block_rows, counts, w_in, b_in, w_out, b_out):
    n_rows, d = xb.shape
    n_e, _, d_ff2 = w_in.shape
    d_ff = d_ff2 // 2
    n_blocks = n_rows // MOE_BLOCK
    valid = block_rows > 0
    prev_e = jnp.concatenate([jnp.full((1,), -1, jnp.int32), block_e[:-1]])
    first = jnp.logical_and(valid, block_e != prev_e).astype(jnp.int32)
    slot = ((jnp.cumsum(first) - 1) % 2).astype(jnp.int32)
    experts = jnp.arange(n_e, dtype=jnp.int32)
    later = jnp.logical_and(experts[None, :] > experts[:, None], counts[None, :] > 0)
    next_of = jnp.min(jnp.where(later, experts[None, :], n_e), axis=1)
    next_of = jnp.where(next_of == n_e, -1, next_of)
    next_e = jnp.sum(jnp.where(block_e[:, None] == experts[None, :], next_of[None, :], 0), axis=1).astype(jnp.int32)
    blk = lambda i, *_: (i, 0)
    whole = lambda i, *_: (0, 0, 0)
    step_rows = EXPERT_SUB * MOE_BLOCK
    grid_spec = pltpu.PrefetchScalarGridSpec(
        num_scalar_prefetch=5,
        grid=(n_blocks // EXPERT_SUB,),
        in_specs=[pl.BlockSpec((step_rows, d), blk),
                  pl.BlockSpec(memory_space=pl.ANY),
                  pl.BlockSpec((n_e, 1, d_ff2), whole),
                  pl.BlockSpec(memory_space=pl.ANY),
                  pl.BlockSpec((n_e, 1, d), whole)],
        out_specs=pl.BlockSpec((step_rows, d), blk),
        scratch_shapes=[pltpu.VMEM((2, d, d_ff2), F32), pltpu.VMEM((2, d_ff, d), F32),
                        pltpu.VMEM((d, d_ff2), BF16), pltpu.VMEM((d_ff, d), BF16),
                        pltpu.SemaphoreType.DMA((2, 2))],
    )
    return pl.pallas_call(
        functools.partial(_expert_kernel, d_ff=d_ff),
        grid_spec=grid_spec,
        out_shape=jax.ShapeDtypeStruct((n_rows, d), F32),
        compiler_params=_cparams(("arbitrary",)),
        name="moe_experts",
    )(block_e, block_rows, first, slot, next_e, xb, w_in, b_in.reshape(n_e, 1, d_ff2), w_out,
      b_out.reshape(n_e, 1, d))


def _combine_kernel(dst_ref, nseg_ref, yb_ref, gate_ref, pos_ref, x2_ref, npost_ref, g2_ref, o_ref, ybuf, sem):
    tile = pl.program_id(0) * pl.num_programs(1) + pl.program_id(1)
    n_tiles = pl.num_programs(0) * pl.num_programs(1)
    slot = tile % 2

    def segments(t_idx, s, fn):
        copy = lambda g: _seg_copy(yb_ref, dst_ref[t_idx * SEG_MAX + g], ybuf.at[s], g * SEG_ROWS, sem.at[s])
        _for_segment_pairs(nseg_ref[t_idx], copy, fn)

    @pl.when(tile == 0)
    def _():
        ybuf[...] = jnp.zeros_like(ybuf)
        segments(0, 0, _start)

    @pl.when(tile + 1 < n_tiles)
    def _():
        segments(tile + 1, 1 - slot, _start)

    segments(tile, slot, _wait)
    weights = _sort_onehot(pos_ref[0], ybuf.shape[1], gate_ref[0])
    f = _dot(weights, ybuf[slot].astype(BF16))
    o_ref[0] = x2_ref[0] + g2_ref[0] * _rms(f, npost_ref[...])


def _combine(yb_rows, dst_seg, n_seg, gates, pos, x2, npost, g2):
    b, t, d = x2.shape
    row = lambda bi, i, *_: (bi, i, 0)
    grid_spec = pltpu.PrefetchScalarGridSpec(
        num_scalar_prefetch=2,
        grid=(b, t // MOE_TILE),
        in_specs=[pl.BlockSpec(memory_space=pl.ANY),
                  pl.BlockSpec((1, MOE_TILE, gates.shape[2]), row),
                  pl.BlockSpec((1, MOE_TILE, pos.shape[2]), row),
                  pl.BlockSpec((1, MOE_TILE, d), row),
                  pl.BlockSpec((1, d), lambda bi, i, *_: (0, 0)),
                  pl.BlockSpec((1, 1, d), lambda bi, i, *_: (bi, 0, 0))],
        out_specs=pl.BlockSpec((1, MOE_TILE, d), row),
        scratch_shapes=[pltpu.VMEM((2, SEG_MAX * SEG_ROWS, d), F32), pltpu.SemaphoreType.DMA((2,))],
    )
    return pl.pallas_call(
        _combine_kernel,
        grid_spec=grid_spec,
        out_shape=jax.ShapeDtypeStruct((b, t, d), F32),
        compiler_params=_cparams(("arbitrary", "arbitrary")),
        name="moe_combine",
    )(dst_seg, n_seg, yb_rows, gates, pos, x2, npost, g2)


def _lora_up(w_up):
    _, r, w = w_up.shape
    z = jnp.zeros((r, w), w_up.dtype)
    return jnp.concatenate([jnp.concatenate([w_up[0], z], axis=1),
                            jnp.concatenate([z, w_up[1]], axis=1)], axis=0).astype(BF16)


def _row_tile(t, pref):
    return pref if t % pref == 0 else t


def kernel(x, c, ctx, c_ctx, w_ada, b_ada, norm_pre_mix, norm_post_mix, norm_pre_ffn, norm_post_ffn,
           w_in, mu_shift, w0, w_decay_up, a0, w_iclr_up, k_k, k_a, r_k, w_gate_up, ln_x_w, ln_x_b,
           conv_w, w_out, w_router, b_router, w_exp_in, b_exp_in, w_exp_out, b_exp_out):
    b, t, d = x.shape
    t_ctx = ctx.shape[1]
    depth = w_ada.shape[0]
    width = k_k.shape[1]
    n_groups = width // MXU_WIDTH
    k_off, v_off = 0, width
    decay_off = 2 * width
    iclr_off = decay_off + 2 * DECAY_LORA
    r_off = iclr_off + 2 * ICLR_LORA
    glora_off = r_off + width
    conv_off = glora_off + GATE_LORA
    gate_off = conv_off + 3 * width

    xc = ctx
    for l in range(depth):
        last = l == depth - 1
        rows = jnp.concatenate([c, c_ctx[None, :], jnp.zeros((8 - b - 1, d), F32)], axis=0)
        mod = _ada(rows, w_ada[l], b_ada[l])
        sh1, sc1, g1, sh2, sc2, g2 = [mod[:b, None, j * d:(j + 1) * d] for j in range(6)]
        csh1, csc1, cg1, csh2, csc2, cg2 = [jnp.broadcast_to(mod[b:b + 1, None, j * d:(j + 1) * d], (b, 1, d))
                                            for j in range(6)]

        wl = w_in[l]
        cols = lambda lo, n: wl[:, lo:lo + n]
        w_main = jnp.concatenate([cols(k_off, width), cols(v_off, width), cols(r_off, width)], axis=1).astype(BF16)
        w_lora = jnp.concatenate([cols(decay_off, 2 * DECAY_LORA), cols(iclr_off, 2 * ICLR_LORA),
                                  cols(glora_off, GATE_LORA)], axis=1).astype(BF16)
        w_conv = tuple(cols(conv_off + j * width, width).astype(BF16) for j in range(3))
        w_gate = cols(gate_off, 2 * d).astype(BF16)
        mu = mu_shift[l]
        mu_main = jnp.concatenate([mu[k_off:k_off + width], mu[v_off:v_off + width], mu[r_off:r_off + width]])[None, :]
        mu_lora = jnp.concatenate([mu[decay_off:decay_off + 2 * DECAY_LORA], mu[iclr_off:iclr_off + 2 * ICLR_LORA],
                                   mu[glora_off:glora_off + GATE_LORA]])[None, :]
        sp = {
            "mu_main": mu_main, "mu_lora": mu_lora,
            "w0": w0[l].reshape(1, 2 * width), "w_dec": _lora_up(w_decay_up[l]),
            "a0": a0[l].reshape(1, 2 * width), "w_iclr": _lora_up(w_iclr_up[l]),
            "k_k": k_k[l][None, :], "k_a": k_a[l][None, :],
            "r_k": r_k[l].reshape(1, width), "w_gate": w_gate_up[l].astype(BF16),
        }
        if not last:
            raise NotImplementedError("context stream update for non-final layers")

        n_state_main = 2 * width
        zc_main, zc_lora = _inproj(xc, norm_pre_mix[l], csc1, csh1,
                                   [w_main[:, :n_state_main], w_lora[:, :2 * DECAY_LORA + 2 * ICLR_LORA]],
                                   None, _row_tile(t_ctx, 256))
        sp_ctx = dict(sp, mu_main=mu_main[:, :n_state_main], mu_lora=mu_lora[:, :2 * DECAY_LORA + 2 * ICLR_LORA])
        kap_c, v_c, ld0_c, ld1_c, kd0_c, kd1_c, be0_c, be1_c = _state_terms(
            zc_main, zc_lora, sp_ctx, grid_mode=False, with_read=False, tm=t_ctx)
        s0 = jnp.zeros((b, 2, n_groups, MXU_WIDTH, MXU_WIDTH), F32)
        (s_ctx,) = _wkv_scan([(ld0_c, kd0_c, be0_c), (ld1_c, kd1_c, be1_c)], kap_c, v_c, None, s0)

        zm, zl, zg, zb, p = _inproj(x, norm_pre_mix[l], sc1, sh1, [w_main, w_lora, w_gate], w_conv,
                                    _row_tile(t, 512))
        kap, v, ld0, ld1, kd0, kd1, be0, be1, r, bonus, g = _state_terms(
            zm, zl, sp, grid_mode=True, with_read=True, tm=_row_tile(t, 512))
        yf, yb, _ = _wkv_scan([(ld0, kd0, be0), (ld1, kd1, be1)], kap, v, r, s_ctx)

        lanes = 128
        pad_e = lanes - N_EXPERTS
        pr = {
            "ln_x_w": ln_x_w[l][None, :], "ln_x_b": ln_x_b[l][None, :], "conv_w": conv_w[l],
            "w_out": w_out[l].astype(BF16), "norm_post_mix": norm_post_mix[l][None, :], "g1": g1,
            "norm_pre_ffn": norm_pre_ffn[l][None, :], "sc2": sc2, "sh2": sh2,
            "w_router": jnp.pad(w_router[l], ((0, 0), (0, pad_e))),
            "b_router": jnp.pad(b_router[l], (0, pad_e), constant_values=-jnp.inf)[None, :],
        }
        x2, h2, gates, pos, tile_segs = _merge_route(yf, yb, bonus, g, zb, p, zg, x, pr, MOE_TILE)

        n_tok = b * t
        nk = n_tok * TOP_K
        n_tiles = n_tok // MOE_TILE
        experts = jnp.arange(N_EXPERTS, dtype=jnp.int32)
        run_rows = tile_segs[:, 0, :N_EXPERTS].astype(jnp.int32) * SEG_ROWS
        rows_before = jnp.cumsum(run_rows, axis=0) - run_rows
        counts = jnp.sum(run_rows, axis=0)
        padded = (counts + MOE_BLOCK - 1) // MOE_BLOCK * MOE_BLOCK
        pad_ends = jnp.cumsum(padded)
        pad_starts = pad_ends - padded
        run_off = jnp.cumsum(run_rows, axis=1) - run_rows
        n_seg = (jnp.sum(run_rows, axis=1) // SEG_ROWS).astype(jnp.int32)
        seg_row = jnp.arange(SEG_MAX, dtype=jnp.int32) * SEG_ROWS
        seg_e = jnp.minimum(jnp.sum(((run_off + run_rows)[:, None, :] <= seg_row[None, :, None]).astype(jnp.int32),
                                    axis=2), N_EXPERTS - 1)
        shift = pad_starts[None, :] + rows_before - run_off
        dst_seg = (jnp.sum(jnp.where(seg_e[..., None] == experts, shift[:, None, :], 0), axis=2)
                   + seg_row[None, :]).reshape(n_tiles * SEG_MAX).astype(jnp.int32)
        n_rows_max = nk + n_tiles * N_EXPERTS * (SEG_ROWS - 1) + N_EXPERTS * (MOE_BLOCK - 1)
        n_blocks = -(-n_rows_max // (MOE_BLOCK * EXPERT_SUB)) * EXPERT_SUB
        blk_start = jnp.arange(n_blocks, dtype=jnp.int32) * MOE_BLOCK
        block_e = jnp.minimum(jnp.sum((pad_ends[None, :] <= blk_start[:, None]).astype(jnp.int32), axis=1),
                              N_EXPERTS - 1)
        own = block_e[:, None] == experts[None, :]
        seg_end = jnp.sum(jnp.where(own, (pad_starts + counts)[None, :], 0), axis=1)
        block_rows = jnp.clip(seg_end - blk_start, 0, MOE_BLOCK).astype(jnp.int32)
        xb = _dispatch(h2.reshape(n_tok, d), pos.reshape(n_tok, pos.shape[2]), dst_seg, n_seg, block_rows)
        yb_rows = _experts(xb, block_e, block_rows, counts, w_exp_in[l], b_exp_in[l], w_exp_out[l], b_exp_out[l])
        x = _combine(yb_rows, dst_seg, n_seg, gates, pos, x2, norm_post_ffn[l][None, :], g2)
    return x
```

```python
import functools

import jax
import jax.numpy as jnp
from jax import lax
from jax.experimental import pallas as pl
from jax.experimental.pallas import tpu as pltpu

F32 = jnp.float32
BF16 = jnp.bfloat16

HEAD_DIM = 64
GRID_W = 64
DECAY_LORA = 64
ICLR_LORA = 64
GATE_LORA = 128
GN_EPS = 64e-5
NORM_EPS = 1e-12
CONV_K = 3
N_EXPERTS = 32
TOP_K = 4
SWIGLU_LIMIT = 7.0
SWIGLU_ALPHA = 1.702
MOE_BLOCK = 256
RMS_EPS = 1e-6

MXU_WIDTH = 256
HEADS_PER_GROUP = MXU_WIDTH // HEAD_DIM
CHUNK = 64
SCAN_SUB = 2
DECAY_SCALE = 0.6065306597126334
MOE_TILE = 512
EXPERT_SUB = 4
SEG_ROWS = 16
SEG_MAX = 160
assert SEG_MAX * SEG_ROWS >= MOE_TILE * TOP_K + N_EXPERTS * (SEG_ROWS - 1)
VMEM_LIMIT = 56 * 1024 * 1024


def _cparams(sem):
    return pltpu.CompilerParams(dimension_semantics=sem, vmem_limit_bytes=VMEM_LIMIT)


def _dot(a, b):
    return jnp.dot(a, b, preferred_element_type=F32)


def _dot_nt(a, b):
    return lax.dot_general(a, b, (((1,), (1,)), ((), ())), preferred_element_type=F32)


def _dot_tn(a, b):
    return lax.dot_general(a, b, (((0,), (0,)), ((), ())), preferred_element_type=F32)


def _split3(x):
    h = x.astype(BF16)
    r = x - h.astype(F32)
    m = r.astype(BF16)
    l = (r - m.astype(F32)).astype(BF16)
    return h, m, l


def _block_diag_mask(n, blk):
    r = lax.broadcasted_iota(jnp.int32, (n, n), 0) // blk
    c = lax.broadcasted_iota(jnp.int32, (n, n), 1) // blk
    return r == c


def _head_sum(x, ones_bd):
    h = x.astype(BF16)
    l = (x - h.astype(F32)).astype(BF16)
    return _dot(h, ones_bd) + _dot(l, ones_bd)


def _rms(x, gain):
    return x * lax.rsqrt(jnp.mean(x * x, axis=-1, keepdims=True) + RMS_EPS) * gain


def _sigmoid(x):
    return 1.0 / (1.0 + jnp.exp(-x))


def _ada_kernel(c_ref, w_ref, b_ref, o_ref):
    c = c_ref[...]
    s = c * _sigmoid(c)
    o_ref[...] = _dot(s, w_ref[...]) + b_ref[...]


def _ada(c_rows, w, b):
    rows, d = c_rows.shape
    n = w.shape[1]
    tn = 1024
    return pl.pallas_call(
        _ada_kernel,
        grid=(n // tn,),
        in_specs=[pl.BlockSpec((rows, d), lambda j: (0, 0)),
                  pl.BlockSpec((d, tn), lambda j: (0, j)),
                  pl.BlockSpec((1, tn), lambda j: (0, j))],
        out_specs=pl.BlockSpec((rows, tn), lambda j: (0, j)),
        out_shape=jax.ShapeDtypeStruct((rows, n), F32),
        compiler_params=_cparams(("arbitrary",)),
        name="ada_mod",
    )(c_rows, w, b.reshape(1, n))


def _inproj_kernel(x_ref, g_ref, sc_ref, sh_ref, *refs, n_plain, with_conv):
    n_w = n_plain + (3 if with_conv else 0)
    w_refs, o_refs = refs[:n_w], refs[n_w:]
    x = x_ref[0]
    h = _rms(x, g_ref[...]) * (1.0 + sc_ref[0]) + sh_ref[0]
    hb = h.astype(BF16)
    col = 512
    for w_ref, o_ref in zip(w_refs[:n_plain], o_refs[:n_plain]):
        n = w_ref.shape[1]
        for j in range(0, n, col):
            e = min(j + col, n)
            o_ref[0, :, j:e] = _dot(hb, w_ref[:, j:e]).astype(o_ref.dtype)
    if with_conv:
        wb_ref, wc_ref, wx_ref = w_refs[n_plain:]
        ob_ref, op_ref = o_refs[n_plain:]
        n = wb_ref.shape[1]
        for j in range(0, n, col):
            e = min(j + col, n)
            ob_ref[0, :, j:e] = _dot(hb, wb_ref[:, j:e]).astype(ob_ref.dtype)
            op_ref[0, :, j:e] = (_dot(hb, wc_ref[:, j:e]) * _dot(hb, wx_ref[:, j:e])).astype(op_ref.dtype)


def _inproj(x, gain, sc, sh, plain_ws, conv_ws, tm):
    b, t, d = x.shape
    with_conv = conv_ws is not None
    ws = list(plain_ws) + (list(conv_ws) if with_conv else [])
    out_w = [w.shape[1] for w in plain_ws] + ([conv_ws[0].shape[1]] * 2 if with_conv else [])
    row = lambda bi, i: (bi, i, 0)
    const = lambda bi, i: (0, 0)
    in_specs = [pl.BlockSpec((1, tm, d), row),
                pl.BlockSpec((1, d), const),
                pl.BlockSpec((1, 1, d), lambda bi, i: (bi, 0, 0)),
                pl.BlockSpec((1, 1, d), lambda bi, i: (bi, 0, 0))]
    in_specs += [pl.BlockSpec(w.shape, const, pipeline_mode=pl.Buffered(1)) for w in ws]
    return pl.pallas_call(
        functools.partial(_inproj_kernel, n_plain=len(plain_ws), with_conv=with_conv),
        grid=(b, t // tm),
        in_specs=in_specs,
        out_specs=[pl.BlockSpec((1, tm, n), row) for n in out_w],
        out_shape=[jax.ShapeDtypeStruct((b, t, n), BF16) for n in out_w],
        compiler_params=_cparams(("arbitrary", "arbitrary")),
        name="in_proj",
    )(x, gain.reshape(1, d), sc, sh, *ws)


def _shift_mix(cur, prev, nxt, mu, grid_mode, is_first, is_last):
    tm, w = cur.shape
    row = lax.broadcasted_iota(jnp.int32, (tm, w), 0)
    grp = lax.broadcasted_iota(jnp.int32, (tm, w), 1) % 4
    back = pltpu.roll(cur, 1, axis=0)
    fwd = pltpu.roll(cur, tm - 1, axis=0)
    if grid_mode:
        colpos = row % GRID_W
        left = jnp.where(colpos == 0, 0.0, back)
        right = jnp.where(colpos == GRID_W - 1, 0.0, fwd)
        prev = jnp.where(is_first, 0.0, prev)
        nxt = jnp.where(is_last, 0.0, nxt)
        up = jnp.concatenate([prev, cur[:tm - GRID_W]], axis=0)
        down = jnp.concatenate([cur[GRID_W:], nxt], axis=0)
        shifted = jnp.where(grp == 0, left, jnp.where(grp == 1, right, jnp.where(grp == 2, up, down)))
    else:
        before = jnp.where(row == 0, 0.0, back)
        after = jnp.where(row == tm - 1, 0.0, fwd)
        shifted = jnp.where(grp % 2 == 0, before, after)
    return cur + mu * (shifted - cur)


def _state_kernel(*refs, grid_mode, with_read, width):
    if grid_mode:
        (zm_ref, zmp_ref, zmn_ref, zl_ref, zlp_ref, zln_ref), refs = refs[:6], refs[6:]
    else:
        (zm_ref, zl_ref), refs = refs[:2], refs[2:]
        zmp_ref = zmn_ref = zlp_ref = zln_ref = None
    (mum_ref, mul_ref, w0_ref, wdec_ref, a0_ref, wic_ref, kk_ref, ka_ref), refs = refs[:8], refs[8:]
    if with_read:
        (rk_ref, wg_ref), refs = refs[:2], refs[2:]
    (kap_ref, v_ref, ld0_ref, ld1_ref, kd0_ref, kd1_ref, be0_ref, be1_ref), refs = refs[:8], refs[8:]
    if with_read:
        r_ref, bonus_ref, g_ref = refs

    i = pl.program_id(1)
    is_first = i == 0
    is_last = i == pl.num_programs(1) - 1
    ones_bd = _block_diag_mask(MXU_WIDTH, HEAD_DIM).astype(BF16)

    def shifted(z_ref, zp_ref, zn_ref, mu_ref, lo, hi):
        cur = z_ref[0, :, lo:hi].astype(F32)
        prev = zp_ref[0, :, lo:hi].astype(F32) if grid_mode else None
        nxt = zn_ref[0, :, lo:hi].astype(F32) if grid_mode else None
        return _shift_mix(cur, prev, nxt, mu_ref[:, lo:hi], grid_mode, is_first, is_last)

    lora = shifted(zl_ref, zlp_ref, zln_ref, mul_ref, 0, zl_ref.shape[2])
    lw = jnp.tanh(lora[:, :2 * DECAY_LORA]).astype(BF16)
    la = lora[:, 2 * DECAY_LORA:2 * DECAY_LORA + 2 * ICLR_LORA].astype(BF16)
    if with_read:
        lg = _sigmoid(lora[:, 2 * DECAY_LORA + 2 * ICLR_LORA:]).astype(BF16)

    for lo in range(0, width, MXU_WIDTH):
        hi = lo + MXU_WIDTH
        k = shifted(zm_ref, zmp_ref, zmn_ref, mum_ref, lo, hi)
        v = shifted(zm_ref, zmp_ref, zmn_ref, mum_ref, width + lo, width + hi)
        kk = k * kk_ref[:, lo:hi]
        n2 = _head_sum(kk * kk, ones_bd)
        kk = kk / jnp.maximum(jnp.sqrt(n2), NORM_EPS)
        kap_ref[0, :, lo:hi] = kk.astype(kap_ref.dtype)
        v_ref[0, :, lo:hi] = v.astype(v_ref.dtype)
        a_sum = None
        for d, (ld_ref, kd_ref, be_ref) in enumerate(((ld0_ref, kd0_ref, be0_ref), (ld1_ref, kd1_ref, be1_ref))):
            lo_d, hi_d = d * width + lo, d * width + hi
            pre_w = w0_ref[:, lo_d:hi_d] + _dot(lw, wdec_ref[:, lo_d:hi_d])
            ld_ref[0, :, lo:hi] = -DECAY_SCALE * _sigmoid(pre_w)
            a = _sigmoid(a0_ref[:, lo_d:hi_d] + _dot(la, wic_ref[:, lo_d:hi_d]))
            kd_ref[0, :, lo:hi] = (k * (1.0 + (a - 1.0) * ka_ref[:, lo:hi])).astype(kd_ref.dtype)
            be_ref[0, :, lo:hi] = (kk * a).astype(be_ref.dtype)
            a_sum = a if a_sum is None else a_sum + a
        if with_read:
            r = shifted(zm_ref, zmp_ref, zmn_ref, mum_ref, 2 * width + lo, 2 * width + hi)
            r_ref[0, :, lo:hi] = r.astype(r_ref.dtype)
            k_bonus = k * (1.0 + (0.5 * a_sum - 1.0) * ka_ref[:, lo:hi])
            s = _head_sum(r * k_bonus * rk_ref[:, lo:hi], ones_bd)
            bonus_ref[0, :, lo:hi] = (s * v).astype(bonus_ref.dtype)
            g_ref[0, :, lo:hi] = _dot(lg, wg_ref[:, lo:hi]).astype(g_ref.dtype)


def _state_terms(zm, zl, p, grid_mode, with_read, tm):
    b, t, _ = zm.shape
    width = p["k_k"].shape[1]
    nt = t // tm
    row = lambda bi, i: (bi, i, 0)
    const = lambda bi, i: (0, 0)
    ins, in_specs = [], []
    if grid_mode:
        hb = tm // GRID_W
        last = t // GRID_W - 1
        prev = lambda bi, i: (bi, jnp.maximum(i * hb - 1, 0), 0)
        nxt = lambda bi, i: (bi, jnp.minimum((i + 1) * hb, last), 0)
        for z in (zm, zl):
            ins += [z, z, z]
            in_specs += [pl.BlockSpec((1, tm, z.shape[2]), row),
                         pl.BlockSpec((1, GRID_W, z.shape[2]), prev),
                         pl.BlockSpec((1, GRID_W, z.shape[2]), nxt)]
    else:
        for z in (zm, zl):
            ins.append(z)
            in_specs.append(pl.BlockSpec((1, tm, z.shape[2]), row))
    names = ["mu_main", "mu_lora", "w0", "w_dec", "a0", "w_iclr", "k_k", "k_a"]
    if with_read:
        names += ["r_k", "w_gate"]
    for n in names:
        ins.append(p[n])
        in_specs.append(pl.BlockSpec(p[n].shape, const))
    out_dt = [BF16, BF16, F32, F32, BF16, BF16, BF16, BF16] + ([BF16] * 3 if with_read else [])
    return pl.pallas_call(
        functools.partial(_state_kernel, grid_mode=grid_mode, with_read=with_read, width=width),
        grid=(b, nt),
        in_specs=in_specs,
        out_specs=[pl.BlockSpec((1, tm, width), row) for _ in out_dt],
        out_shape=[jax.ShapeDtypeStruct((b, t, width), dt) for dt in out_dt],
        compiler_params=_cparams(("arbitrary", "arbitrary")),
        name="state_terms_grid" if grid_mode else "state_terms_seq",
    )(*ins)


def _pack_bd(y, bd_mask):
    reps = MXU_WIDTH // y.shape[0]
    return jnp.where(bd_mask, jnp.concatenate([y] * reps, axis=0), jnp.zeros((), y.dtype))


def _scan_chunk(insts, with_y):
    c = CHUNK
    bd_mask = _block_diag_mask(MXU_WIDTH, HEAD_DIM)
    ti = lax.broadcasted_iota(jnp.int32, (c, c), 0)
    tj = lax.broadcasted_iota(jnp.int32, (c, c), 1)
    t4 = lax.broadcasted_iota(jnp.int32, (c, MXU_WIDTH), 0)
    i4 = lax.broadcasted_iota(jnp.int32, (c, MXU_WIDTH), 1) % c
    eye4 = (t4 == i4).astype(F32)
    tri = ((tj <= ti).astype(BF16), (tj >= ti).astype(BF16))
    strict4 = (i4 < t4, i4 > t4)
    incl4 = (i4 <= t4, i4 >= t4)
    pack = lambda y: _pack_bd(y, bd_mask)

    for it in insts:
        rev = it["reverse"]
        ld = it["ld"]
        cl = sum(_dot(tri[rev], t) for t in _split3(ld))
        cl_end = cl[0:1] if rev else cl[c - 1:c]
        rel = cl - cl_end
        e_k = jnp.exp(-rel)
        it["kt"] = (it["kd"] * e_k).astype(BF16)
        it["bt"] = (it["be"] * e_k).astype(BF16)
        it["vb"] = it["v"].astype(BF16)
        kq = (it["kap"] * jnp.exp(rel - ld)).astype(BF16)
        it["lhs"] = jnp.concatenate([kq, (it["r"] * jnp.exp(rel)).astype(BF16)], axis=0) if with_y else kq
        it["sd"] = it["s"] * jnp.exp(cl_end)
    for it in insts:
        it["st"] = _dot_nt(it["lhs"], it["sd"].astype(BF16))
        it["a_b"] = _dot_nt(it["lhs"], pack(it["bt"]))
        it["a_k"] = _dot_nt(it["lhs"], pack(it["kt"]))

    for it in insts:
        a = jnp.where(strict4[it["reverse"]], it["a_b"][:c], 0.0)
        it["t_inv"] = eye4 - a
        ab = a.astype(BF16)
        it["pw"] = _dot(ab, pack(ab)).astype(BF16)
    n = 2
    while 2 * n < c:
        for it in insts:
            both = _dot(jnp.concatenate([it["pw"], it["t_inv"].astype(BF16)], axis=0), pack(it["pw"]))
            it["pw"] = both[:c].astype(BF16)
            it["t_inv"] = it["t_inv"] + both[c:]
        n *= 2
    for it in insts:
        it["t_inv"] = it["t_inv"] + _dot(it["t_inv"].astype(BF16), pack(it["pw"]))

    for it in insts:
        rev = it["reverse"]
        a_kk = jnp.where(strict4[rev], it["a_k"][:c], 0.0)
        if with_y:
            a_rk = jnp.where(incl4[rev], it["a_k"][c:], 0.0)
            it["av"] = _dot(jnp.concatenate([a_kk, a_rk], axis=0).astype(BF16), pack(it["vb"]))
        else:
            it["av"] = _dot(a_kk.astype(BF16), pack(it["vb"]))
    for it in insts:
        rhs = it["st"][:c] + it["av"][:c]
        it["ub"] = _dot(it["t_inv"].astype(BF16), pack(rhs.astype(BF16))).astype(BF16)
    out = []
    for it in insts:
        y = None
        if with_y:
            a_rb = jnp.where(incl4[it["reverse"]], it["a_b"][c:], 0.0)
            y = it["st"][c:] + it["av"][c:] - _dot(a_rb.astype(BF16), pack(it["ub"]))
        delta = _dot_tn(jnp.concatenate([it["vb"], it["ub"]], axis=0),
                        jnp.concatenate([it["kt"], -it["bt"]], axis=0))
        out.append((it["sd"] + jnp.where(bd_mask, delta, 0.0), y))
    return out


def _scan_kernel(*refs, with_y, width):
    n_in = 12 if with_y else 10
    in_refs, refs = refs[:n_in], refs[n_in:]
    s0_ref, refs = refs[0], refs[1:]
    if with_y:
        y_refs, refs = refs[:2], refs[2:]
    sfin_ref, s_scr = refs
    c_idx = pl.program_id(0)
    n_groups = width // MXU_WIDTH

    @pl.when(c_idx == 0)
    def _():
        s_scr[...] = s0_ref[...]

    per_dir = 6 if with_y else 5
    names = ("ld", "kap", "kd", "be", "v", "r")[:per_dir]
    n_batch = in_refs[0].shape[0]
    n_sub = in_refs[0].shape[1] // CHUNK

    def sub_chunk(j, carry):
        insts = []
        for bi in range(n_batch):
            for d in range(2):
                rows = pl.ds(pl.multiple_of((j if d == 0 else n_sub - 1 - j) * CHUNK, CHUNK), CHUNK)
                d_refs = in_refs[d * per_dir:(d + 1) * per_dir]
                for g in range(n_groups):
                    lo, hi = g * MXU_WIDTH, (g + 1) * MXU_WIDTH
                    it = {n: ref[bi, rows, lo:hi].astype(F32) for n, ref in zip(names, d_refs)}
                    it.update(reverse=d, s=s_scr[bi, d, g], at=(bi, d, g), rows=rows)
                    insts.append(it)
        for it, (s_new, y) in zip(insts, _scan_chunk(insts, with_y)):
            bi, d, g = it["at"]
            s_scr[bi, d, g] = s_new
            if with_y:
                y_refs[d][bi, it["rows"], g * MXU_WIDTH:(g + 1) * MXU_WIDTH] = y
        return carry

    lax.fori_loop(0, n_sub, sub_chunk, 0)

    @pl.when(c_idx == pl.num_programs(0) - 1)
    def _():
        sfin_ref[...] = s_scr[...]


def _wkv_scan(dirs, kap, v, r, s0):
    b, t, width = kap.shape
    rows = SCAN_SUB * CHUNK if t % (SCAN_SUB * CHUNK) == 0 else CHUNK
    nc = t // rows
    with_y = r is not None
    fwd = lambda ci: (0, ci, 0)
    bwd = lambda ci: (0, nc - 1 - ci, 0)
    ins, in_specs = [], []
    for d, imap in enumerate((fwd, bwd)):
        ld, kd, be = dirs[d]
        for arr in (ld, kap, kd, be, v) + ((r,) if with_y else ()):
            ins.append(arr)
            in_specs.append(pl.BlockSpec((b, rows, width), imap))
    n_groups = width // MXU_WIDTH
    s_shape = (b, 2, n_groups, MXU_WIDTH, MXU_WIDTH)
    s_spec = pl.BlockSpec(s_shape, lambda ci: (0, 0, 0, 0, 0))
    ins.append(s0)
    in_specs.append(s_spec)
    out_specs, out_shape = [], []
    if with_y:
        out_specs += [pl.BlockSpec((b, rows, width), fwd), pl.BlockSpec((b, rows, width), bwd)]
        out_shape += [jax.ShapeDtypeStruct((b, t, width), F32)] * 2
    out_specs.append(s_spec)
    out_shape.append(jax.ShapeDtypeStruct(s_shape, F32))
    return pl.pallas_call(
        functools.partial(_scan_kernel, with_y=with_y, width=width),
        grid=(nc,),
        in_specs=in_specs,
        out_specs=out_specs,
        out_shape=out_shape,
        scratch_shapes=[pltpu.VMEM(s_shape, F32)],
        compiler_params=_cparams(("arbitrary",)),
        name="wkv_scan" if with_y else "wkv_scan_ctx",
    )(*ins)


def _merge_kernel(yf_ref, yb_ref, bonus_ref, g_ref, zb_ref, p_ref, pp_ref, pn_ref, zg_ref, x_ref,
                  lnw_ref, lnb_ref, cw_ref, wout_ref, npost_ref, g1_ref, npre_ref, sc2_ref, sh2_ref,
                  wr_ref, br_ref,
                  x2_ref, h2_ref, gate_ref, pos_ref, nseg_ref, *, width):
    i = pl.program_id(1)
    tm = x_ref.shape[1]

    ones_bd = _block_diag_mask(MXU_WIDTH, HEAD_DIM).astype(BF16)
    row = lax.broadcasted_iota(jnp.int32, (tm, MXU_WIDTH), 0)
    halo = pp_ref.shape[1]
    parts = []
    for lo in range(0, width, MXU_WIDTH):
        hi = lo + MXU_WIDTH
        y = yf_ref[0, :, lo:hi] + yb_ref[0, :, lo:hi]
        mean = _head_sum(y, ones_bd) * (1.0 / HEAD_DIM)
        yc = y - mean
        var = _head_sum(yc * yc, ones_bd) * (1.0 / HEAD_DIM)
        yn = yc * lax.rsqrt(var + GN_EPS) * lnw_ref[:, lo:hi] + lnb_ref[:, lo:hi]
        y_rwkv = (yn + bonus_ref[0, :, lo:hi].astype(F32)) * g_ref[0, :, lo:hi].astype(F32)
        p = p_ref[0, :, lo:hi].astype(F32)
        p_prev = jnp.where(i == 0, 0.0, pp_ref[0, halo - 1:halo, lo:hi].astype(F32))
        p_next = jnp.where(i == pl.num_programs(1) - 1, 0.0, pn_ref[0, 0:1, lo:hi].astype(F32))
        before = jnp.where(row == 0, p_prev, pltpu.roll(p, 1, axis=0))
        after = jnp.where(row == tm - 1, p_next, pltpu.roll(p, tm - 1, axis=0))
        conv = cw_ref[0:1, lo:hi] * before + cw_ref[1:2, lo:hi] * p + cw_ref[2:3, lo:hi] * after
        y_conv = zb_ref[0, :, lo:hi].astype(F32) * conv
        ga = _sigmoid(zg_ref[0, :, lo:hi].astype(F32))
        gb = _sigmoid(zg_ref[0, :, width + lo:width + hi].astype(F32))
        parts.append((ga * y_rwkv + gb * y_conv).astype(BF16))
    merged = jnp.concatenate(parts, axis=1)
    mix = _dot(merged, wout_ref[...])
    x2 = x_ref[0] + g1_ref[0] * _rms(mix, npost_ref[...])
    x2_ref[0] = x2
    h2 = _rms(x2, npre_ref[...]) * (1.0 + sc2_ref[0]) + sh2_ref[0]
    h2_ref[0] = h2.astype(h2_ref.dtype)

    logits = _dot(h2, wr_ref[...]) + br_ref[...]
    lane = lax.broadcasted_iota(jnp.int32, logits.shape, 1)
    vals, idxs = [], []
    sel = jnp.zeros(logits.shape, F32)
    work = logits
    for _ in range(TOP_K):
        m = jnp.max(work, axis=-1, keepdims=True)
        idx = jnp.min(jnp.where(work == m, lane, logits.shape[1]), axis=-1, keepdims=True)
        hit = lane == idx
        vals.append(m)
        idxs.append(idx)
        sel = jnp.where(hit, 1.0, sel)
        work = jnp.where(hit, -jnp.inf, work)
    exps = [jnp.exp(vk - vals[0]) for vk in vals]
    denom = exps[0] + exps[1] + exps[2] + exps[3]
    r_i = lax.broadcasted_iota(jnp.int32, (tm, tm), 0)
    c_i = lax.broadcasted_iota(jnp.int32, (tm, tm), 1)
    before_cnt = _dot((c_i < r_i).astype(BF16), sel.astype(BF16))
    n_seg = jnp.floor((jnp.sum(sel, axis=0, keepdims=True) + (SEG_ROWS - 1.0)) * (1.0 / SEG_ROWS))
    e_r = lax.broadcasted_iota(jnp.int32, (logits.shape[1], logits.shape[1]), 0)
    e_c = lax.broadcasted_iota(jnp.int32, (logits.shape[1], logits.shape[1]), 1)
    seg_before = _dot(jnp.broadcast_to(n_seg, (8, logits.shape[1])).astype(BF16), (e_r < e_c).astype(BF16))[0:1]
    slot_base = before_cnt + seg_before * float(SEG_ROWS)
    gate_out = jnp.zeros(logits.shape, F32)
    pos_out = jnp.zeros(logits.shape, jnp.int32)
    for k in range(TOP_K):
        pos_k = jnp.sum(jnp.where(lane == idxs[k], slot_base, 0.0), axis=-1, keepdims=True)
        gate_out = jnp.where(lane == k, exps[k] / denom, gate_out)
        pos_out = jnp.where(lane == k, pos_k.astype(jnp.int32), pos_out)
    gate_ref[0] = gate_out
    pos_ref[0] = pos_out
    nseg_ref[0] = n_seg


def _merge_route(yf, yb, bonus, g, zb, p, zg, x, pr, tm):
    b, t, d = x.shape
    width = yf.shape[2]
    halo = 8
    hb = tm // halo
    last = t // halo - 1
    row = lambda bi, i: (bi, i, 0)
    const = lambda bi, i: (0, 0)
    per_b = lambda bi, i: (bi, 0, 0)
    prev = lambda bi, i: (bi, jnp.maximum(i * hb - 1, 0), 0)
    nxt = lambda bi, i: (bi, jnp.minimum((i + 1) * hb, last), 0)
    lanes = 128
    in_specs = [pl.BlockSpec((1, tm, width), row)] * 6
    in_specs += [pl.BlockSpec((1, halo, width), prev), pl.BlockSpec((1, halo, width), nxt),
                 pl.BlockSpec((1, tm, 2 * width), row), pl.BlockSpec((1, tm, d), row)]
    params = [pr["ln_x_w"], pr["ln_x_b"], pr["conv_w"], pr["w_out"], pr["norm_post_mix"]]
    in_specs += [pl.BlockSpec(a.shape, const) for a in params]
    in_specs.append(pl.BlockSpec((1, 1, d), per_b))
    in_specs.append(pl.BlockSpec(pr["norm_pre_ffn"].shape, const))
    in_specs += [pl.BlockSpec((1, 1, d), per_b)] * 2
    in_specs += [pl.BlockSpec(pr["w_router"].shape, const), pl.BlockSpec(pr["b_router"].shape, const)]
    nt = t // tm
    out_specs = [pl.BlockSpec((1, tm, d), row), pl.BlockSpec((1, tm, d), row)]
    out_specs += [pl.BlockSpec((1, tm, lanes), row)] * 2
    out_specs.append(pl.BlockSpec((1, 1, lanes), lambda bi, i: (bi * nt + i, 0, 0)))
    out_shape = [jax.ShapeDtypeStruct((b, t, d), F32), jax.ShapeDtypeStruct((b, t, d), BF16),
                 jax.ShapeDtypeStruct((b, t, lanes), F32), jax.ShapeDtypeStruct((b, t, lanes), jnp.int32),
                 jax.ShapeDtypeStruct((b * nt, 1, lanes), F32)]
    return pl.pallas_call(
        functools.partial(_merge_kernel, width=width),
        grid=(b, nt),
        in_specs=in_specs,
        out_specs=out_specs,
        out_shape=out_shape,
        compiler_params=_cparams(("arbitrary", "arbitrary")),
        name="merge_route",
    )(yf, yb, bonus, g, zb, p, p, p, zg, x, *params, pr["g1"], pr["norm_pre_ffn"], pr["sc2"], pr["sh2"],
      pr["w_router"], pr["b_router"])


def _seg_copy(src_ref, src_row, dst_ref, dst_row, sem):
    al = lambda r: r if isinstance(r, int) else pl.multiple_of(r, SEG_ROWS)
    return pltpu.make_async_copy(src_ref.at[pl.ds(al(src_row), SEG_ROWS)],
                                 dst_ref.at[pl.ds(al(dst_row), SEG_ROWS)], sem)


def _start(cp, priority):
    cp.start(priority=priority)


def _wait(cp, priority):
    del priority
    cp.wait()


def _for_segment_pairs(n, copy, fn):
    def pair(i, carry):
        fn(copy(2 * i), 0)

        @pl.when(2 * i + 1 < n)
        def _():
            fn(copy(2 * i + 1), 1)
        return carry
    lax.fori_loop(0, (n + 1) // 2, pair, 0)


def _sort_onehot(pos, n_sorted, values=None):
    col = lax.broadcasted_iota(jnp.int32, (pos.shape[0], n_sorted), 1)
    out = jnp.zeros((pos.shape[0], n_sorted), F32)
    for k in range(TOP_K):
        out = jnp.where(col == pos[:, k:k + 1], 1.0 if values is None else values[:, k:k + 1], out)
    return out.astype(BF16)


def _dispatch_kernel(dst_ref, nseg_ref, nrows_ref, h2_ref, pos_ref, xb_ref, zseg, sbuf, sem, zsem):
    step = pl.program_id(0)
    slot = step % 2
    n_fill_blocks = nrows_ref.shape[0]

    def fill_segments(fn):
        def per_block(j, carry):
            def one(q, c2):
                fn(_seg_copy(zseg, 0, xb_ref, j * MOE_BLOCK + q * SEG_ROWS, zsem))
                return c2
            return lax.fori_loop(nrows_ref[j] // SEG_ROWS, MOE_BLOCK // SEG_ROWS, one, carry)
        lax.fori_loop(0, n_fill_blocks, per_block, 0)

    @pl.when(step == 0)
    def _():
        zseg[...] = jnp.zeros_like(zseg)
        fill_segments(lambda cp: cp.start())

    onehot = _sort_onehot(pos_ref[...], sbuf.shape[1])
    sbuf[slot] = _dot_tn(onehot, h2_ref[...]).astype(BF16)

    def segments(t_idx, s, fn):
        copy = lambda g: _seg_copy(sbuf.at[s], g * SEG_ROWS, xb_ref, dst_ref[t_idx * SEG_MAX + g], sem.at[s])
        _for_segment_pairs(nseg_ref[t_idx], copy, fn)

    segments(step, slot, _start)

    @pl.when(step > 0)
    def _():
        segments(step - 1, 1 - slot, _wait)

    @pl.when(step == pl.num_programs(0) - 1)
    def _():
        segments(step, slot, _wait)
        fill_segments(lambda cp: cp.wait())


def _dispatch(h2, pos, dst_seg, n_seg, block_rows):
    n_tok, d = h2.shape
    n_rows = block_rows.shape[0] * MOE_BLOCK
    tile = lambda i, *_: (i, 0)
    grid_spec = pltpu.PrefetchScalarGridSpec(
        num_scalar_prefetch=3,
        grid=(n_tok // MOE_TILE,),
        in_specs=[pl.BlockSpec((MOE_TILE, d), tile), pl.BlockSpec((MOE_TILE, pos.shape[1]), tile)],
        out_specs=pl.BlockSpec(memory_space=pl.ANY),
        scratch_shapes=[pltpu.VMEM((SEG_ROWS, d), BF16), pltpu.VMEM((2, SEG_MAX * SEG_ROWS, d), BF16),
                        pltpu.SemaphoreType.DMA((2,)), pltpu.SemaphoreType.DMA],
    )
    return pl.pallas_call(
        _dispatch_kernel,
        grid_spec=grid_spec,
        out_shape=jax.ShapeDtypeStruct((n_rows, d), BF16),
        compiler_params=_cparams(("arbitrary",)),
        name="moe_dispatch",
    )(dst_seg, n_seg, block_rows, h2, pos)


def _expert_kernel(be_ref, nrows_ref, first_ref, slot_ref, next_ref, xb_ref, win_hbm, bin_ref, wout_hbm, bout_ref,
                   o_ref, win_f32, wout_f32, win_scr, wout_scr, sem, *, d_ff):
    step = pl.program_id(0)
    n_sub = xb_ref.shape[0] // MOE_BLOCK

    def weight_copies(e, s):
        return (pltpu.make_async_copy(win_hbm.at[e], win_f32.at[s], sem.at[s, 0]),
                pltpu.make_async_copy(wout_hbm.at[e], wout_f32.at[s], sem.at[s, 1]))

    @pl.when(step == 0)
    def _():
        for cp in weight_copies(be_ref[0], 0):
            cp.start()

    def block(sub, carry):
        i = step * n_sub + sub
        rows = pl.ds(pl.multiple_of(sub * MOE_BLOCK, MOE_BLOCK), MOE_BLOCK)
        n_rows = nrows_ref[i]
        e = be_ref[i]

        @pl.when(first_ref[i] == 1)
        def _():
            s = slot_ref[i]
            for cp in weight_copies(e, s):
                cp.wait()
            win_scr[...] = win_f32[s].astype(BF16)
            wout_scr[...] = wout_f32[s].astype(BF16)

            @pl.when(next_ref[i] >= 0)
            def _():
                for cp in weight_copies(next_ref[i], 1 - s):
                    cp.start()

        @pl.when(n_rows > 0)
        def _():
            gu = _dot(xb_ref[rows, :], win_scr[...]) + bin_ref[e]
            gate = jnp.minimum(gu[:, :d_ff], SWIGLU_LIMIT)
            up = jnp.clip(gu[:, d_ff:], -SWIGLU_LIMIT, SWIGLU_LIMIT)
            act = (up + 1.0) * gate * _sigmoid(SWIGLU_ALPHA * gate)
            o_ref[rows, :] = (_dot(act.astype(BF16), wout_scr[...]) + bout_ref[e]).astype(o_ref.dtype)

        @pl.when(n_rows == 0)
        def _():
            o_ref[rows, :] = jnp.zeros((MOE_BLOCK, o_ref.shape[1]), o_ref.dtype)
        return carry

    lax.fori_loop(0, n_sub, block, 0)


def _experts(xb, block_e, block_rows, counts, w_in, b_in, w_out, b_out):
    n_rows, d = xb.shape
    n_e, _, d_ff2 = w_in.shape
    d_ff = d_ff2 // 2
    n_blocks = n_rows // MOE_BLOCK
    valid = block_rows > 0
    prev_e = jnp.concatenate([jnp.full((1,), -1, jnp.int32), block_e[:-1]])
    first = jnp.logical_and(valid, block_e != prev_e).astype(jnp.int32)
    slot = ((jnp.cumsum(first) - 1) % 2).astype(jnp.int32)
    experts = jnp.arange(n_e, dtype=jnp.int32)
    later = jnp.logical_and(experts[None, :] > experts[:, None], counts[None, :] > 0)
    next_of = jnp.min(jnp.where(later, experts[None, :], n_e), axis=1)
    next_of = jnp.where(next_of == n_e, -1, next_of)
    next_e = jnp.sum(jnp.where(block_e[:, None] == experts[None, :], next_of[None, :], 0), axis=1).astype(jnp.int32)
    blk = lambda i, *_: (i, 0)
    whole = lambda i, *_: (0, 0, 0)
    step_rows = EXPERT_SUB * MOE_BLOCK
    grid_spec = pltpu.PrefetchScalarGridSpec(
        num_scalar_prefetch=5,
        grid=(n_blocks // EXPERT_SUB,),
        in_specs=[pl.BlockSpec((step_rows, d), blk),
                  pl.BlockSpec(memory_space=pl.ANY),
                  pl.BlockSpec((n_e, 1, d_ff2), whole),
                  pl.BlockSpec(memory_space=pl.ANY),
                  pl.BlockSpec((n_e, 1, d), whole)],
        out_specs=pl.BlockSpec((step_rows, d), blk),
        scratch_shapes=[pltpu.VMEM((2, d, d_ff2), F32), pltpu.VMEM((2, d_ff, d), F32),
                        pltpu.VMEM((d, d_ff2), BF16), pltpu.VMEM((d_ff, d), BF16),
                        pltpu.SemaphoreType.DMA((2, 2))],
    )
    return pl.pallas_call(
        functools.partial(_expert_kernel, d_ff=d_ff),
        grid_spec=grid_spec,
        out_shape=jax.ShapeDtypeStruct((n_rows, d), BF16),
        compiler_params=_cparams(("arbitrary",)),
        name="moe_experts",
    )(block_e, block_rows, first, slot, next_e, xb, w_in, b_in.reshape(n_e, 1, d_ff2), w_out,
      b_out.reshape(n_e, 1, d))


def _combine_kernel(dst_ref, nseg_ref, yb_ref, gate_ref, pos_ref, x2_ref, npost_ref, g2_ref, o_ref, ybuf, sem):
    tile = pl.program_id(0) * pl.num_programs(1) + pl.program_id(1)
    n_tiles = pl.num_programs(0) * pl.num_programs(1)
    slot = tile % 2

    def segments(t_idx, s, fn):
        copy = lambda g: _seg_copy(yb_ref, dst_ref[t_idx * SEG_MAX + g], ybuf.at[s], g * SEG_ROWS, sem.at[s])
        _for_segment_pairs(nseg_ref[t_idx], copy, fn)

    @pl.when(tile == 0)
    def _():
        ybuf[...] = jnp.zeros_like(ybuf)
        segments(0, 0, _start)

    @pl.when(tile + 1 < n_tiles)
    def _():
        segments(tile + 1, 1 - slot, _start)

    segments(tile, slot, _wait)
    weights = _sort_onehot(pos_ref[0], ybuf.shape[1], gate_ref[0])
    f = _dot(weights, ybuf[slot])
    o_ref[0] = x2_ref[0] + g2_ref[0] * _rms(f, npost_ref[...])


def _combine(yb_rows, dst_seg, n_seg, gates, pos, x2, npost, g2):
    b, t, d = x2.shape
    row = lambda bi, i, *_: (bi, i, 0)
    grid_spec = pltpu.PrefetchScalarGridSpec(
        num_scalar_prefetch=2,
        grid=(b, t // MOE_TILE),
        in_specs=[pl.BlockSpec(memory_space=pl.ANY),
                  pl.BlockSpec((1, MOE_TILE, gates.shape[2]), row),
                  pl.BlockSpec((1, MOE_TILE, pos.shape[2]), row),
                  pl.BlockSpec((1, MOE_TILE, d), row),
                  pl.BlockSpec((1, d), lambda bi, i, *_: (0, 0)),
                  pl.BlockSpec((1, 1, d), lambda bi, i, *_: (bi, 0, 0))],
        out_specs=pl.BlockSpec((1, MOE_TILE, d), row),
        scratch_shapes=[pltpu.VMEM((2, SEG_MAX * SEG_ROWS, d), BF16), pltpu.SemaphoreType.DMA((2,))],
    )
    return pl.pallas_call(
        _combine_kernel,
        grid_spec=grid_spec,
        out_shape=jax.ShapeDtypeStruct((b, t, d), F32),
        compiler_params=_cparams(("arbitrary", "arbitrary")),
        name="moe_combine",
    )(dst_seg, n_seg, yb_rows, gates, pos, x2, npost, g2)


def _lora_up(w_up):
    _, r, w = w_up.shape
    z = jnp.zeros((r, w), w_up.dtype)
    return jnp.concatenate([jnp.concatenate([w_up[0], z], axis=1),
                            jnp.concatenate([z, w_up[1]], axis=1)], axis=0).astype(BF16)


def _row_tile(t, pref):
    return pref if t % pref == 0 else t


def kernel(x, c, ctx, c_ctx, w_ada, b_ada, norm_pre_mix, norm_post_mix, norm_pre_ffn, norm_post_ffn,
           w_in, mu_shift, w0, w_decay_up, a0, w_iclr_up, k_k, k_a, r_k, w_gate_up, ln_x_w, ln_x_b,
           conv_w, w_out, w_router, b_router, w_exp_in, b_exp_in, w_exp_out, b_exp_out):
    b, t, d = x.shape
    t_ctx = ctx.shape[1]
    depth = w_ada.shape[0]
    width = k_k.shape[1]
    n_groups = width // MXU_WIDTH
    k_off, v_off = 0, width
    decay_off = 2 * width
    iclr_off = decay_off + 2 * DECAY_LORA
    r_off = iclr_off + 2 * ICLR_LORA
    glora_off = r_off + width
    conv_off = glora_off + GATE_LORA
    gate_off = conv_off + 3 * width

    xc = ctx
    for l in range(depth):
        last = l == depth - 1
        rows = jnp.concatenate([c, c_ctx[None, :], jnp.zeros((8 - b - 1, d), F32)], axis=0)
        mod = _ada(rows, w_ada[l], b_ada[l])
        sh1, sc1, g1, sh2, sc2, g2 = [mod[:b, None, j * d:(j + 1) * d] for j in range(6)]
        csh1, csc1, cg1, csh2, csc2, cg2 = [jnp.broadcast_to(mod[b:b + 1, None, j * d:(j + 1) * d], (b, 1, d))
                                            for j in range(6)]

        wl = w_in[l]
        cols = lambda lo, n: wl[:, lo:lo + n]
        w_main = jnp.concatenate([cols(k_off, width), cols(v_off, width), cols(r_off, width)], axis=1).astype(BF16)
        w_lora = jnp.concatenate([cols(decay_off, 2 * DECAY_LORA), cols(iclr_off, 2 * ICLR_LORA),
                                  cols(glora_off, GATE_LORA)], axis=1).astype(BF16)
        w_conv = tuple(cols(conv_off + j * width, width).astype(BF16) for j in range(3))
        w_gate = cols(gate_off, 2 * d).astype(BF16)
        mu = mu_shift[l]
        mu_main = jnp.concatenate([mu[k_off:k_off + width], mu[v_off:v_off + width], mu[r_off:r_off + width]])[None, :]
        mu_lora = jnp.concatenate([mu[decay_off:decay_off + 2 * DECAY_LORA], mu[iclr_off:iclr_off + 2 * ICLR_LORA],
                                   mu[glora_off:glora_off + GATE_LORA]])[None, :]
        sp = {
            "mu_main": mu_main, "mu_lora": mu_lora,
            "w0": w0[l].reshape(1, 2 * width), "w_dec": _lora_up(w_decay_up[l]),
            "a0": a0[l].reshape(1, 2 * width), "w_iclr": _lora_up(w_iclr_up[l]),
            "k_k": k_k[l][None, :], "k_a": k_a[l][None, :],
            "r_k": r_k[l].reshape(1, width), "w_gate": w_gate_up[l].astype(BF16),
        }
        if not last:
            raise NotImplementedError("context stream update for non-final layers")

        n_state_main = 2 * width
        zc_main, zc_lora = _inproj(xc, norm_pre_mix[l], csc1, csh1,
                                   [w_main[:, :n_state_main], w_lora[:, :2 * DECAY_LORA + 2 * ICLR_LORA]],
                                   None, _row_tile(t_ctx, 256))
        sp_ctx = dict(sp, mu_main=mu_main[:, :n_state_main], mu_lora=mu_lora[:, :2 * DECAY_LORA + 2 * ICLR_LORA])
        kap_c, v_c, ld0_c, ld1_c, kd0_c, kd1_c, be0_c, be1_c = _state_terms(
            zc_main, zc_lora, sp_ctx, grid_mode=False, with_read=False, tm=t_ctx)
        s0 = jnp.zeros((b, 2, n_groups, MXU_WIDTH, MXU_WIDTH), F32)
        (s_ctx,) = _wkv_scan([(ld0_c, kd0_c, be0_c), (ld1_c, kd1_c, be1_c)], kap_c, v_c, None, s0)

        zm, zl, zg, zb, p = _inproj(x, norm_pre_mix[l], sc1, sh1, [w_main, w_lora, w_gate], w_conv,
                                    _row_tile(t, 512))
        kap, v, ld0, ld1, kd0, kd1, be0, be1, r, bonus, g = _state_terms(
            zm, zl, sp, grid_mode=True, with_read=True, tm=_row_tile(t, 512))
        yf, yb, _ = _wkv_scan([(ld0, kd0, be0), (ld1, kd1, be1)], kap, v, r, s_ctx)

        lanes = 128
        pad_e = lanes - N_EXPERTS
        pr = {
            "ln_x_w": ln_x_w[l][None, :], "ln_x_b": ln_x_b[l][None, :], "conv_w": conv_w[l],
            "w_out": w_out[l].astype(BF16), "norm_post_mix": norm_post_mix[l][None, :], "g1": g1,
            "norm_pre_ffn": norm_pre_ffn[l][None, :], "sc2": sc2, "sh2": sh2,
            "w_router": jnp.pad(w_router[l], ((0, 0), (0, pad_e))),
            "b_router": jnp.pad(b_router[l], (0, pad_e), constant_values=-jnp.inf)[None, :],
        }
        x2, h2, gates, pos, tile_segs = _merge_route(yf, yb, bonus, g, zb, p, zg, x, pr, MOE_TILE)

        n_tok = b * t
        nk = n_tok * TOP_K
        n_tiles = n_tok // MOE_TILE
        experts = jnp.arange(N_EXPERTS, dtype=jnp.int32)
        run_rows = tile_segs[:, 0, :N_EXPERTS].astype(jnp.int32) * SEG_ROWS
        rows_before = jnp.cumsum(run_rows, axis=0) - run_rows
        counts = jnp.sum(run_rows, axis=0)
        padded = (counts + MOE_BLOCK - 1) // MOE_BLOCK * MOE_BLOCK
        pad_ends = jnp.cumsum(padded)
        pad_starts = pad_ends - padded
        run_off = jnp.cumsum(run_rows, axis=1) - run_rows
        n_seg = (jnp.sum(run_rows, axis=1) // SEG_ROWS).astype(jnp.int32)
        seg_row = jnp.arange(SEG_MAX, dtype=jnp.int32) * SEG_ROWS
        seg_e = jnp.minimum(jnp.sum(((run_off + run_rows)[:, None, :] <= seg_row[None, :, None]).astype(jnp.int32),
                                    axis=2), N_EXPERTS - 1)
        shift = pad_starts[None, :] + rows_before - run_off
        dst_seg = (jnp.sum(jnp.where(seg_e[..., None] == experts, shift[:, None, :], 0), axis=2)
                   + seg_row[None, :]).reshape(n_tiles * SEG_MAX).astype(jnp.int32)
        n_rows_max = nk + n_tiles * N_EXPERTS * (SEG_ROWS - 1) + N_EXPERTS * (MOE_BLOCK - 1)
        n_blocks = -(-n_rows_max // (MOE_BLOCK * EXPERT_SUB)) * EXPERT_SUB
        blk_start = jnp.arange(n_blocks, dtype=jnp.int32) * MOE_BLOCK
        block_e = jnp.minimum(jnp.sum((pad_ends[None, :] <= blk_start[:, None]).astype(jnp.int32), axis=1),
                              N_EXPERTS - 1)
        own = block_e[:, None] == experts[None, :]
        seg_end = jnp.sum(jnp.where(own, (pad_starts + counts)[None, :], 0), axis=1)
        block_rows = jnp.clip(seg_end - blk_start, 0, MOE_BLOCK).astype(jnp.int32)
        xb = _dispatch(h2.reshape(n_tok, d), pos.reshape(n_tok, pos.shape[2]), dst_seg, n_seg, block_rows)
        yb_rows = _experts(xb, block_e, block_rows, counts, w_exp_in[l], b_exp_in[l], w_exp_out[l], b_exp_out[l])
        x = _combine(yb_rows, dst_seg, n_seg, gates, pos, x2, norm_post_ffn[l][None, :], g2)
    return x
```

```python
import functools

import jax
import jax.numpy as jnp
from jax import lax
from jax.experimental import pallas as pl
from jax.experimental.pallas import tpu as pltpu

F32 = jnp.float32
BF16 = jnp.bfloat16

HEAD_DIM = 64
GRID_W = 64
DECAY_LORA = 64
ICLR_LORA = 64
GATE_LORA = 128
GN_EPS = 64e-5
NORM_EPS = 1e-12
CONV_K = 3
N_EXPERTS = 32
TOP_K = 4
SWIGLU_LIMIT = 7.0
SWIGLU_ALPHA = 1.702
MOE_BLOCK = 256
RMS_EPS = 1e-6

MXU_WIDTH = 256
HEADS_PER_GROUP = MXU_WIDTH // HEAD_DIM
CHUNK = 64
SCAN_SUB = 2
DECAY_SCALE = 0.6065306597126334
MOE_TILE = 512
EXPERT_SUB = 4
SEG_ROWS = 16
SEG_MAX = 160
assert SEG_MAX * SEG_ROWS >= MOE_TILE * TOP_K + N_EXPERTS * (SEG_ROWS - 1)
VMEM_LIMIT = 56 * 1024 * 1024


def _cparams(sem):
    return pltpu.CompilerParams(dimension_semantics=sem, vmem_limit_bytes=VMEM_LIMIT)


def _dot(a, b):
    return jnp.dot(a, b, preferred_element_type=F32)


def _dot_nt(a, b):
    return lax.dot_general(a, b, (((1,), (1,)), ((), ())), preferred_element_type=F32)


def _dot_tn(a, b):
    return lax.dot_general(a, b, (((0,), (0,)), ((), ())), preferred_element_type=F32)


def _split3(x):
    h = x.astype(BF16)
    r = x - h.astype(F32)
    m = r.astype(BF16)
    l = (r - m.astype(F32)).astype(BF16)
    return h, m, l


def _block_diag_mask(n, blk):
    r = lax.broadcasted_iota(jnp.int32, (n, n), 0) // blk
    c = lax.broadcasted_iota(jnp.int32, (n, n), 1) // blk
    return r == c


def _head_sum(x, ones_bd):
    h = x.astype(BF16)
    l = (x - h.astype(F32)).astype(BF16)
    return _dot(h, ones_bd) + _dot(l, ones_bd)


def _rms(x, gain):
    return x * lax.rsqrt(jnp.mean(x * x, axis=-1, keepdims=True) + RMS_EPS) * gain


def _sigmoid(x):
    return 1.0 / (1.0 + jnp.exp(-x))


def _ada_kernel(c_ref, w_ref, b_ref, o_ref):
    c = c_ref[...]
    s = c * _sigmoid(c)
    o_ref[...] = _dot(s, w_ref[...]) + b_ref[...]


def _ada(c_rows, w, b):
    rows, d = c_rows.shape
    n = w.shape[1]
    tn = 1024
    return pl.pallas_call(
        _ada_kernel,
        grid=(n // tn,),
        in_specs=[pl.BlockSpec((rows, d), lambda j: (0, 0)),
                  pl.BlockSpec((d, tn), lambda j: (0, j)),
                  pl.BlockSpec((1, tn), lambda j: (0, j))],
        out_specs=pl.BlockSpec((rows, tn), lambda j: (0, j)),
        out_shape=jax.ShapeDtypeStruct((rows, n), F32),
        compiler_params=_cparams(("arbitrary",)),
        name="ada_mod",
    )(c_rows, w, b.reshape(1, n))


def _inproj_kernel(x_ref, g_ref, sc_ref, sh_ref, *refs, n_plain, with_conv):
    n_w = n_plain + (3 if with_conv else 0)
    w_refs, o_refs = refs[:n_w], refs[n_w:]
    x = x_ref[0]
    h = _rms(x, g_ref[...]) * (1.0 + sc_ref[0]) + sh_ref[0]
    hb = h.astype(BF16)
    col = 512
    for w_ref, o_ref in zip(w_refs[:n_plain], o_refs[:n_plain]):
        n = w_ref.shape[1]
        for j in range(0, n, col):
            e = min(j + col, n)
            o_ref[0, :, j:e] = _dot(hb, w_ref[:, j:e]).astype(o_ref.dtype)
    if with_conv:
        wb_ref, wc_ref, wx_ref = w_refs[n_plain:]
        ob_ref, op_ref = o_refs[n_plain:]
        n = wb_ref.shape[1]
        for j in range(0, n, col):
            e = min(j + col, n)
            ob_ref[0, :, j:e] = _dot(hb, wb_ref[:, j:e]).astype(ob_ref.dtype)
            op_ref[0, :, j:e] = (_dot(hb, wc_ref[:, j:e]) * _dot(hb, wx_ref[:, j:e])).astype(op_ref.dtype)


def _inproj(x, gain, sc, sh, plain_ws, conv_ws, tm):
    b, t, d = x.shape
    with_conv = conv_ws is not None
    ws = list(plain_ws) + (list(conv_ws) if with_conv else [])
    out_w = [w.shape[1] for w in plain_ws] + ([conv_ws[0].shape[1]] * 2 if with_conv else [])
    row = lambda bi, i: (bi, i, 0)
    const = lambda bi, i: (0, 0)
    in_specs = [pl.BlockSpec((1, tm, d), row),
                pl.BlockSpec((1, d), const),
                pl.BlockSpec((1, 1, d), lambda bi, i: (bi, 0, 0)),
                pl.BlockSpec((1, 1, d), lambda bi, i: (bi, 0, 0))]
    in_specs += [pl.BlockSpec(w.shape, const, pipeline_mode=pl.Buffered(1)) for w in ws]
    return pl.pallas_call(
        functools.partial(_inproj_kernel, n_plain=len(plain_ws), with_conv=with_conv),
        grid=(b, t // tm),
        in_specs=in_specs,
        out_specs=[pl.BlockSpec((1, tm, n), row) for n in out_w],
        out_shape=[jax.ShapeDtypeStruct((b, t, n), BF16) for n in out_w],
        compiler_params=_cparams(("arbitrary", "arbitrary")),
        name="in_proj",
    )(x, gain.reshape(1, d), sc, sh, *ws)


def _shift_mix(cur, prev, nxt, mu, grid_mode, is_first, is_last):
    tm, w = cur.shape
    row = lax.broadcasted_iota(jnp.int32, (tm, w), 0)
    grp = lax.broadcasted_iota(jnp.int32, (tm, w), 1) % 4
    if grid_mode:
        colpos = row % GRID_W
        left = jnp.where(colpos == 0, 0.0, pltpu.roll(cur, 1, axis=0))
        right = jnp.where(colpos == GRID_W - 1, 0.0, pltpu.roll(cur, tm - 1, axis=0))
        prev = jnp.where(is_first, 0.0, prev)
        nxt = jnp.where(is_last, 0.0, nxt)
        up = jnp.concatenate([prev, cur[:tm - GRID_W]], axis=0)
        down = jnp.concatenate([cur[GRID_W:], nxt], axis=0)
        shifted = jnp.where(grp == 0, left, jnp.where(grp == 1, right, jnp.where(grp == 2, up, down)))
    else:
        before = jnp.where(row == 0, 0.0, pltpu.roll(cur, 1, axis=0))
        after = jnp.where(row == tm - 1, 0.0, pltpu.roll(cur, tm - 1, axis=0))
        shifted = jnp.where(grp % 2 == 0, before, after)
    return cur + mu * (shifted - cur)


def _state_kernel(*refs, grid_mode, with_read, width):
    if grid_mode:
        (zm_ref, zmp_ref, zmn_ref, zl_ref, zlp_ref, zln_ref), refs = refs[:6], refs[6:]
    else:
        (zm_ref, zl_ref), refs = refs[:2], refs[2:]
        zmp_ref = zmn_ref = zlp_ref = zln_ref = None
    (mum_ref, mul_ref, w0_ref, wdec_ref, a0_ref, wic_ref, kk_ref, ka_ref), refs = refs[:8], refs[8:]
    if with_read:
        (rk_ref, wg_ref), refs = refs[:2], refs[2:]
    (kap_ref, v_ref, ld0_ref, ld1_ref, kd0_ref, kd1_ref, be0_ref, be1_ref), refs = refs[:8], refs[8:]
    if with_read:
        r_ref, bonus_ref, g_ref = refs

    i = pl.program_id(1)
    is_first = i == 0
    is_last = i == pl.num_programs(1) - 1
    ones_bd = _block_diag_mask(MXU_WIDTH, HEAD_DIM).astype(BF16)

    def shifted(z_ref, zp_ref, zn_ref, mu_ref, lo, hi):
        cur = z_ref[0, :, lo:hi].astype(F32)
        prev = zp_ref[0, :, lo:hi].astype(F32) if grid_mode else None
        nxt = zn_ref[0, :, lo:hi].astype(F32) if grid_mode else None
        return _shift_mix(cur, prev, nxt, mu_ref[:, lo:hi], grid_mode, is_first, is_last)

    lora = shifted(zl_ref, zlp_ref, zln_ref, mul_ref, 0, zl_ref.shape[2])
    lw = jnp.tanh(lora[:, :2 * DECAY_LORA]).astype(BF16)
    la = lora[:, 2 * DECAY_LORA:2 * DECAY_LORA + 2 * ICLR_LORA].astype(BF16)
    if with_read:
        lg = _sigmoid(lora[:, 2 * DECAY_LORA + 2 * ICLR_LORA:]).astype(BF16)

    for lo in range(0, width, MXU_WIDTH):
        hi = lo + MXU_WIDTH
        k = shifted(zm_ref, zmp_ref, zmn_ref, mum_ref, lo, hi)
        v = shifted(zm_ref, zmp_ref, zmn_ref, mum_ref, width + lo, width + hi)
        kk = k * kk_ref[:, lo:hi]
        n2 = _head_sum(kk * kk, ones_bd)
        kk = kk / jnp.maximum(jnp.sqrt(n2), NORM_EPS)
        kap_ref[0, :, lo:hi] = kk.astype(kap_ref.dtype)
        v_ref[0, :, lo:hi] = v.astype(v_ref.dtype)
        a_sum = None
        for d, (ld_ref, kd_ref, be_ref) in enumerate(((ld0_ref, kd0_ref, be0_ref), (ld1_ref, kd1_ref, be1_ref))):
            lo_d, hi_d = d * width + lo, d * width + hi
            pre_w = w0_ref[:, lo_d:hi_d] + _dot(lw, wdec_ref[:, lo_d:hi_d])
            ld_ref[0, :, lo:hi] = -DECAY_SCALE * _sigmoid(pre_w)
            a = _sigmoid(a0_ref[:, lo_d:hi_d] + _dot(la, wic_ref[:, lo_d:hi_d]))
            kd_ref[0, :, lo:hi] = (k * (1.0 + (a - 1.0) * ka_ref[:, lo:hi])).astype(kd_ref.dtype)
            be_ref[0, :, lo:hi] = (kk * a).astype(be_ref.dtype)
            a_sum = a if a_sum is None else a_sum + a
        if with_read:
            r = shifted(zm_ref, zmp_ref, zmn_ref, mum_ref, 2 * width + lo, 2 * width + hi)
            r_ref[0, :, lo:hi] = r.astype(r_ref.dtype)
            k_bonus = k * (1.0 + (0.5 * a_sum - 1.0) * ka_ref[:, lo:hi])
            s = _head_sum(r * k_bonus * rk_ref[:, lo:hi], ones_bd)
            bonus_ref[0, :, lo:hi] = (s * v).astype(bonus_ref.dtype)
            g_ref[0, :, lo:hi] = _dot(lg, wg_ref[:, lo:hi]).astype(g_ref.dtype)


def _state_terms(zm, zl, p, grid_mode, with_read, tm):
    b, t, _ = zm.shape
    width = p["k_k"].shape[1]
    nt = t // tm
    row = lambda bi, i: (bi, i, 0)
    const = lambda bi, i: (0, 0)
    ins, in_specs = [], []
    if grid_mode:
        hb = tm // GRID_W
        last = t // GRID_W - 1
        prev = lambda bi, i: (bi, jnp.maximum(i * hb - 1, 0), 0)
        nxt = lambda bi, i: (bi, jnp.minimum((i + 1) * hb, last), 0)
        for z in (zm, zl):
            ins += [z, z, z]
            in_specs += [pl.BlockSpec((1, tm, z.shape[2]), row),
                         pl.BlockSpec((1, GRID_W, z.shape[2]), prev),
                         pl.BlockSpec((1, GRID_W, z.shape[2]), nxt)]
    else:
        for z in (zm, zl):
            ins.append(z)
            in_specs.append(pl.BlockSpec((1, tm, z.shape[2]), row))
    names = ["mu_main", "mu_lora", "w0", "w_dec", "a0", "w_iclr", "k_k", "k_a"]
    if with_read:
        names += ["r_k", "w_gate"]
    for n in names:
        ins.append(p[n])
        in_specs.append(pl.BlockSpec(p[n].shape, const))
    out_dt = [BF16, BF16, F32, F32, BF16, BF16, BF16, BF16] + ([BF16] * 3 if with_read else [])
    return pl.pallas_call(
        functools.partial(_state_kernel, grid_mode=grid_mode, with_read=with_read, width=width),
        grid=(b, nt),
        in_specs=in_specs,
        out_specs=[pl.BlockSpec((1, tm, width), row) for _ in out_dt],
        out_shape=[jax.ShapeDtypeStruct((b, t, width), dt) for dt in out_dt],
        compiler_params=_cparams(("arbitrary", "arbitrary")),
        name="state_terms_grid" if grid_mode else "state_terms_seq",
    )(*ins)


def _pack_bd(y, bd_mask):
    reps = MXU_WIDTH // y.shape[0]
    return jnp.where(bd_mask, jnp.concatenate([y] * reps, axis=0), jnp.zeros((), y.dtype))


def _scan_chunk(insts, with_y):
    c = CHUNK
    bd_mask = _block_diag_mask(MXU_WIDTH, HEAD_DIM)
    ti = lax.broadcasted_iota(jnp.int32, (c, c), 0)
    tj = lax.broadcasted_iota(jnp.int32, (c, c), 1)
    t4 = lax.broadcasted_iota(jnp.int32, (c, MXU_WIDTH), 0)
    i4 = lax.broadcasted_iota(jnp.int32, (c, MXU_WIDTH), 1) % c
    eye4 = (t4 == i4).astype(F32)
    tri = ((tj <= ti).astype(BF16), (tj >= ti).astype(BF16))
    strict4 = (i4 < t4, i4 > t4)
    incl4 = (i4 <= t4, i4 >= t4)
    pack = lambda y: _pack_bd(y, bd_mask)

    for it in insts:
        rev = it["reverse"]
        ld = it["ld"]
        cl = sum(_dot(tri[rev], t) for t in _split3(ld))
        cl_end = cl[0:1] if rev else cl[c - 1:c]
        rel = cl - cl_end
        e_k = jnp.exp(-rel)
        it["kt"] = (it["kd"] * e_k).astype(BF16)
        it["bt"] = (it["be"] * e_k).astype(BF16)
        it["vb"] = it["v"].astype(BF16)
        kq = (it["kap"] * jnp.exp(rel - ld)).astype(BF16)
        it["lhs"] = jnp.concatenate([kq, (it["r"] * jnp.exp(rel)).astype(BF16)], axis=0) if with_y else kq
        it["sd"] = it["s"] * jnp.exp(cl_end)
    for it in insts:
        it["st"] = _dot_nt(it["lhs"], it["sd"].astype(BF16))
        it["a_b"] = _dot_nt(it["lhs"], pack(it["bt"]))
        it["a_k"] = _dot_nt(it["lhs"], pack(it["kt"]))

    for it in insts:
        a = jnp.where(strict4[it["reverse"]], it["a_b"][:c], 0.0)
        it["t_inv"] = eye4 - a
        ab = a.astype(BF16)
        it["pw"] = _dot(ab, pack(ab)).astype(BF16)
    n = 2
    while 2 * n < c:
        for it in insts:
            both = _dot(jnp.concatenate([it["pw"], it["t_inv"].astype(BF16)], axis=0), pack(it["pw"]))
            it["pw"] = both[:c].astype(BF16)
            it["t_inv"] = it["t_inv"] + both[c:]
        n *= 2
    for it in insts:
        it["t_inv"] = it["t_inv"] + _dot(it["t_inv"].astype(BF16), pack(it["pw"]))

    for it in insts:
        rev = it["reverse"]
        a_kk = jnp.where(strict4[rev], it["a_k"][:c], 0.0)
        if with_y:
            a_rk = jnp.where(incl4[rev], it["a_k"][c:], 0.0)
            it["av"] = _dot(jnp.concatenate([a_kk, a_rk], axis=0).astype(BF16), pack(it["vb"]))
        else:
            it["av"] = _dot(a_kk.astype(BF16), pack(it["vb"]))
    for it in insts:
        rhs = it["st"][:c] + it["av"][:c]
        it["ub"] = _dot(it["t_inv"].astype(BF16), pack(rhs.astype(BF16))).astype(BF16)
    out = []
    for it in insts:
        y = None
        if with_y:
            a_rb = jnp.where(incl4[it["reverse"]], it["a_b"][c:], 0.0)
            y = it["st"][c:] + it["av"][c:] - _dot(a_rb.astype(BF16), pack(it["ub"]))
        delta = _dot_tn(jnp.concatenate([it["vb"], it["ub"]], axis=0),
                        jnp.concatenate([it["kt"], -it["bt"]], axis=0))
        out.append((it["sd"] + jnp.where(bd_mask, delta, 0.0), y))
    return out


def _scan_kernel(*refs, with_y, width):
    n_in = 12 if with_y else 10
    in_refs, refs = refs[:n_in], refs[n_in:]
    s0_ref, refs = refs[0], refs[1:]
    if with_y:
        y_refs, refs = refs[:2], refs[2:]
    sfin_ref, s_scr = refs
    c_idx = pl.program_id(0)
    n_groups = width // MXU_WIDTH

    @pl.when(c_idx == 0)
    def _():
        s_scr[...] = s0_ref[...]

    per_dir = 6 if with_y else 5
    names = ("ld", "kap", "kd", "be", "v", "r")[:per_dir]
    n_batch = in_refs[0].shape[0]
    n_sub = in_refs[0].shape[1] // CHUNK

    def sub_chunk(j, carry):
        insts = []
        for bi in range(n_batch):
            for d in range(2):
                rows = pl.ds(pl.multiple_of((j if d == 0 else n_sub - 1 - j) * CHUNK, CHUNK), CHUNK)
                d_refs = in_refs[d * per_dir:(d + 1) * per_dir]
                for g in range(n_groups):
                    lo, hi = g * MXU_WIDTH, (g + 1) * MXU_WIDTH
                    it = {n: ref[bi, rows, lo:hi].astype(F32) for n, ref in zip(names, d_refs)}
                    it.update(reverse=d, s=s_scr[bi, d, g], at=(bi, d, g), rows=rows)
                    insts.append(it)
        for it, (s_new, y) in zip(insts, _scan_chunk(insts, with_y)):
            bi, d, g = it["at"]
            s_scr[bi, d, g] = s_new
            if with_y:
                y_refs[d][bi, it["rows"], g * MXU_WIDTH:(g + 1) * MXU_WIDTH] = y
        return carry

    lax.fori_loop(0, n_sub, sub_chunk, 0)

    @pl.when(c_idx == pl.num_programs(0) - 1)
    def _():
        sfin_ref[...] = s_scr[...]


def _wkv_scan(dirs, kap, v, r, s0):
    b, t, width = kap.shape
    rows = SCAN_SUB * CHUNK if t % (SCAN_SUB * CHUNK) == 0 else CHUNK
    nc = t // rows
    with_y = r is not None
    fwd = lambda ci: (0, ci, 0)
    bwd = lambda ci: (0, nc - 1 - ci, 0)
    ins, in_specs = [], []
    for d, imap in enumerate((fwd, bwd)):
        ld, kd, be = dirs[d]
        for arr in (ld, kap, kd, be, v) + ((r,) if with_y else ()):
            ins.append(arr)
            in_specs.append(pl.BlockSpec((b, rows, width), imap))
    n_groups = width // MXU_WIDTH
    s_shape = (b, 2, n_groups, MXU_WIDTH, MXU_WIDTH)
    s_spec = pl.BlockSpec(s_shape, lambda ci: (0, 0, 0, 0, 0))
    ins.append(s0)
    in_specs.append(s_spec)
    out_specs, out_shape = [], []
    if with_y:
        out_specs += [pl.BlockSpec((b, rows, width), fwd), pl.BlockSpec((b, rows, width), bwd)]
        out_shape += [jax.ShapeDtypeStruct((b, t, width), F32)] * 2
    out_specs.append(s_spec)
    out_shape.append(jax.ShapeDtypeStruct(s_shape, F32))
    return pl.pallas_call(
        functools.partial(_scan_kernel, with_y=with_y, width=width),
        grid=(nc,),
        in_specs=in_specs,
        out_specs=out_specs,
        out_shape=out_shape,
        scratch_shapes=[pltpu.VMEM(s_shape, F32)],
        compiler_params=_cparams(("arbitrary",)),
        name="wkv_scan" if with_y else "wkv_scan_ctx",
    )(*ins)


def _merge_kernel(yf_ref, yb_ref, bonus_ref, g_ref, zb_ref, p_ref, pp_ref, pn_ref, zg_ref, x_ref,
                  lnw_ref, lnb_ref, cw_ref, wout_ref, npost_ref, g1_ref, npre_ref, sc2_ref, sh2_ref,
                  wr_ref, br_ref,
                  x2_ref, h2_ref, gate_ref, pos_ref, nseg_ref, *, width):
    i = pl.program_id(1)
    tm = x_ref.shape[1]

    ones_bd = _block_diag_mask(MXU_WIDTH, HEAD_DIM).astype(BF16)
    row = lax.broadcasted_iota(jnp.int32, (tm, MXU_WIDTH), 0)
    halo = pp_ref.shape[1]
    parts = []
    for lo in range(0, width, MXU_WIDTH):
        hi = lo + MXU_WIDTH
        y = yf_ref[0, :, lo:hi] + yb_ref[0, :, lo:hi]
        mean = _head_sum(y, ones_bd) * (1.0 / HEAD_DIM)
        yc = y - mean
        var = _head_sum(yc * yc, ones_bd) * (1.0 / HEAD_DIM)
        yn = yc * lax.rsqrt(var + GN_EPS) * lnw_ref[:, lo:hi] + lnb_ref[:, lo:hi]
        y_rwkv = (yn + bonus_ref[0, :, lo:hi].astype(F32)) * g_ref[0, :, lo:hi].astype(F32)
        p = p_ref[0, :, lo:hi].astype(F32)
        p_prev = jnp.where(i == 0, 0.0, pp_ref[0, halo - 1:halo, lo:hi].astype(F32))
        p_next = jnp.where(i == pl.num_programs(1) - 1, 0.0, pn_ref[0, 0:1, lo:hi].astype(F32))
        before = jnp.where(row == 0, p_prev, pltpu.roll(p, 1, axis=0))
        after = jnp.where(row == tm - 1, p_next, pltpu.roll(p, tm - 1, axis=0))
        conv = cw_ref[0:1, lo:hi] * before + cw_ref[1:2, lo:hi] * p + cw_ref[2:3, lo:hi] * after
        y_conv = zb_ref[0, :, lo:hi].astype(F32) * conv
        ga = _sigmoid(zg_ref[0, :, lo:hi].astype(F32))
        gb = _sigmoid(zg_ref[0, :, width + lo:width + hi].astype(F32))
        parts.append((ga * y_rwkv + gb * y_conv).astype(BF16))
    merged = jnp.concatenate(parts, axis=1)
    mix = _dot(merged, wout_ref[...])
    x2 = x_ref[0] + g1_ref[0] * _rms(mix, npost_ref[...])
    x2_ref[0] = x2
    h2 = _rms(x2, npre_ref[...]) * (1.0 + sc2_ref[0]) + sh2_ref[0]
    h2_ref[0] = h2.astype(h2_ref.dtype)

    logits = _dot(h2, wr_ref[...]) + br_ref[...]
    lane = lax.broadcasted_iota(jnp.int32, logits.shape, 1)
    vals, idxs = [], []
    sel = jnp.zeros(logits.shape, F32)
    work = logits
    for _ in range(TOP_K):
        m = jnp.max(work, axis=-1, keepdims=True)
        idx = jnp.min(jnp.where(work == m, lane, logits.shape[1]), axis=-1, keepdims=True)
        hit = lane == idx
        vals.append(m)
        idxs.append(idx)
        sel = jnp.where(hit, 1.0, sel)
        work = jnp.where(hit, -jnp.inf, work)
    exps = [jnp.exp(vk - vals[0]) for vk in vals]
    denom = exps[0] + exps[1] + exps[2] + exps[3]
    r_i = lax.broadcasted_iota(jnp.int32, (tm, tm), 0)
    c_i = lax.broadcasted_iota(jnp.int32, (tm, tm), 1)
    before_cnt = _dot((c_i < r_i).astype(BF16), sel.astype(BF16))
    n_seg = jnp.floor((jnp.sum(sel, axis=0, keepdims=True) + (SEG_ROWS - 1.0)) * (1.0 / SEG_ROWS))
    e_r = lax.broadcasted_iota(jnp.int32, (logits.shape[1], logits.shape[1]), 0)
    e_c = lax.broadcasted_iota(jnp.int32, (logits.shape[1], logits.shape[1]), 1)
    seg_before = _dot(jnp.broadcast_to(n_seg, (8, logits.shape[1])).astype(BF16), (e_r < e_c).astype(BF16))[0:1]
    slot_base = before_cnt + seg_before * float(SEG_ROWS)
    gate_out = jnp.zeros(logits.shape, F32)
    pos_out = jnp.zeros(logits.shape, jnp.int32)
    for k in range(TOP_K):
        pos_k = jnp.sum(jnp.where(lane == idxs[k], slot_base, 0.0), axis=-1, keepdims=True)
        gate_out = jnp.where(lane == k, exps[k] / denom, gate_out)
        pos_out = jnp.where(lane == k, pos_k.astype(jnp.int32), pos_out)
    gate_ref[0] = gate_out
    pos_ref[0] = pos_out
    nseg_ref[0] = n_seg


def _merge_route(yf, yb, bonus, g, zb, p, zg, x, pr, tm):
    b, t, d = x.shape
    width = yf.shape[2]
    halo = 8
    hb = tm // halo
    last = t // halo - 1
    row = lambda bi, i: (bi, i, 0)
    const = lambda bi, i: (0, 0)
    per_b = lambda bi, i: (bi, 0, 0)
    prev = lambda bi, i: (bi, jnp.maximum(i * hb - 1, 0), 0)
    nxt = lambda bi, i: (bi, jnp.minimum((i + 1) * hb, last), 0)
    lanes = 128
    in_specs = [pl.BlockSpec((1, tm, width), row)] * 6
    in_specs += [pl.BlockSpec((1, halo, width), prev), pl.BlockSpec((1, halo, width), nxt),
                 pl.BlockSpec((1, tm, 2 * width), row), pl.BlockSpec((1, tm, d), row)]
    params = [pr["ln_x_w"], pr["ln_x_b"], pr["conv_w"], pr["w_out"], pr["norm_post_mix"]]
    in_specs += [pl.BlockSpec(a.shape, const) for a in params]
    in_specs.append(pl.BlockSpec((1, 1, d), per_b))
    in_specs.append(pl.BlockSpec(pr["norm_pre_ffn"].shape, const))
    in_specs += [pl.BlockSpec((1, 1, d), per_b)] * 2
    in_specs += [pl.BlockSpec(pr["w_router"].shape, const), pl.BlockSpec(pr["b_router"].shape, const)]
    nt = t // tm
    out_specs = [pl.BlockSpec((1, tm, d), row), pl.BlockSpec((1, tm, d), row)]
    out_specs += [pl.BlockSpec((1, tm, lanes), row)] * 2
    out_specs.append(pl.BlockSpec((1, 1, lanes), lambda bi, i: (bi * nt + i, 0, 0)))
    out_shape = [jax.ShapeDtypeStruct((b, t, d), F32), jax.ShapeDtypeStruct((b, t, d), BF16),
                 jax.ShapeDtypeStruct((b, t, lanes), F32), jax.ShapeDtypeStruct((b, t, lanes), jnp.int32),
                 jax.ShapeDtypeStruct((b * nt, 1, lanes), F32)]
    return pl.pallas_call(
        functools.partial(_merge_kernel, width=width),
        grid=(b, nt),
        in_specs=in_specs,
        out_specs=out_specs,
        out_shape=out_shape,
        compiler_params=_cparams(("arbitrary", "arbitrary")),
        name="merge_route",
    )(yf, yb, bonus, g, zb, p, p, p, zg, x, *params, pr["g1"], pr["norm_pre_ffn"], pr["sc2"], pr["sh2"],
      pr["w_router"], pr["b_router"])


def _seg_copy(src_ref, src_row, dst_ref, dst_row, sem):
    al = lambda r: r if isinstance(r, int) else pl.multiple_of(r, SEG_ROWS)
    return pltpu.make_async_copy(src_ref.at[pl.ds(al(src_row), SEG_ROWS)],
                                 dst_ref.at[pl.ds(al(dst_row), SEG_ROWS)], sem)


def _start(cp, priority):
    cp.start(priority=priority)


def _wait_segments(n, src_ref, dst_ref, sem):
    for bit in (128, 64, 32, 16, 8, 4, 2, 1):
        @pl.when((n & bit) != 0)
        def _():
            rows = pl.ds(0, bit * SEG_ROWS)
            pltpu.make_async_copy(src_ref.at[rows], dst_ref.at[rows], sem).wait()


def _for_segment_pairs(n, copy, fn):
    def pair(i, carry):
        fn(copy(2 * i), 0)

        @pl.when(2 * i + 1 < n)
        def _():
            fn(copy(2 * i + 1), 1)
        return carry
    lax.fori_loop(0, (n + 1) // 2, pair, 0)


def _sort_onehot(pos, n_sorted, values=None):
    col = lax.broadcasted_iota(jnp.int32, (pos.shape[0], n_sorted), 1)
    out = jnp.zeros((pos.shape[0], n_sorted), F32)
    for k in range(TOP_K):
        out = jnp.where(col == pos[:, k:k + 1], 1.0 if values is None else values[:, k:k + 1], out)
    return out.astype(BF16)


def _dispatch_kernel(dst_ref, nseg_ref, nrows_ref, h2_ref, pos_ref, xb_ref, zseg, sbuf, sem, zsem):
    step = pl.program_id(0)
    slot = step % 2
    n_fill_blocks = nrows_ref.shape[0]

    def fill_segments(fn):
        def per_block(j, carry):
            def one(q, c2):
                fn(_seg_copy(zseg, 0, xb_ref, j * MOE_BLOCK + q * SEG_ROWS, zsem))
                return c2
            return lax.fori_loop(nrows_ref[j] // SEG_ROWS, MOE_BLOCK // SEG_ROWS, one, carry)
        lax.fori_loop(0, n_fill_blocks, per_block, 0)

    @pl.when(step == 0)
    def _():
        zseg[...] = jnp.zeros_like(zseg)
        fill_segments(lambda cp: cp.start())

    onehot = _sort_onehot(pos_ref[...], sbuf.shape[1])
    sbuf[slot] = _dot_tn(onehot, h2_ref[...]).astype(BF16)

    copy = lambda g: _seg_copy(sbuf.at[slot], g * SEG_ROWS, xb_ref, dst_ref[step * SEG_MAX + g], sem.at[slot])
    _for_segment_pairs(nseg_ref[step], copy, _start)

    @pl.when(step > 0)
    def _():
        _wait_segments(nseg_ref[step - 1], sbuf.at[1 - slot], xb_ref, sem.at[1 - slot])

    @pl.when(step == pl.num_programs(0) - 1)
    def _():
        _wait_segments(nseg_ref[step], sbuf.at[slot], xb_ref, sem.at[slot])
        fill_segments(lambda cp: cp.wait())


def _dispatch(h2, pos, dst_seg, n_seg, block_rows):
    n_tok, d = h2.shape
    n_rows = block_rows.shape[0] * MOE_BLOCK
    tile = lambda i, *_: (i, 0)
    grid_spec = pltpu.PrefetchScalarGridSpec(
        num_scalar_prefetch=3,
        grid=(n_tok // MOE_TILE,),
        in_specs=[pl.BlockSpec((MOE_TILE, d), tile), pl.BlockSpec((MOE_TILE, pos.shape[1]), tile)],
        out_specs=pl.BlockSpec(memory_space=pl.ANY),
        scratch_shapes=[pltpu.VMEM((SEG_ROWS, d), BF16), pltpu.VMEM((2, SEG_MAX * SEG_ROWS, d), BF16),
                        pltpu.SemaphoreType.DMA((2,)), pltpu.SemaphoreType.DMA],
    )
    return pl.pallas_call(
        _dispatch_kernel,
        grid_spec=grid_spec,
        out_shape=jax.ShapeDtypeStruct((n_rows, d), BF16),
        compiler_params=_cparams(("arbitrary",)),
        name="moe_dispatch",
    )(dst_seg, n_seg, block_rows, h2, pos)


def _expert_kernel(be_ref, nrows_ref, first_ref, slot_ref, next_ref, xb_ref, win_hbm, bin_ref, wout_hbm, bout_ref,
                   o_ref, win_f32, wout_f32, win_scr, wout_scr, sem, *, d_ff):
    step = pl.program_id(0)
    n_sub = xb_ref.shape[0] // MOE_BLOCK

    def weight_copies(e, s):
        return (pltpu.make_async_copy(win_hbm.at[e], win_f32.at[s], sem.at[s, 0]),
                pltpu.make_async_copy(wout_hbm.at[e], wout_f32.at[s], sem.at[s, 1]))

    @pl.when(step == 0)
    def _():
        for cp in weight_copies(be_ref[0], 0):
            cp.start()

    def block(sub, carry):
        i = step * n_sub + sub
        rows = pl.ds(pl.multiple_of(sub * MOE_BLOCK, MOE_BLOCK), MOE_BLOCK)
        n_rows = nrows_ref[i]
        e = be_ref[i]

        @pl.when(first_ref[i] == 1)
        def _():
            s = slot_ref[i]
            for cp in weight_copies(e, s):
                cp.wait()
            win_scr[...] = win_f32[s].astype(BF16)
            wout_scr[...] = wout_f32[s].astype(BF16)

            @pl.when(next_ref[i] >= 0)
            def _():
                for cp in weight_copies(next_ref[i], 1 - s):
                    cp.start()

        @pl.when(n_rows > 0)
        def _():
            gu = _dot(xb_ref[rows, :], win_scr[...]) + bin_ref[e]
            gate = jnp.minimum(gu[:, :d_ff], SWIGLU_LIMIT)
            up = jnp.clip(gu[:, d_ff:], -SWIGLU_LIMIT, SWIGLU_LIMIT)
            act = (up + 1.0) * gate * _sigmoid(SWIGLU_ALPHA * gate)
            o_ref[rows, :] = (_dot(act.astype(BF16), wout_scr[...]) + bout_ref[e]).astype(o_ref.dtype)

        @pl.when(n_rows == 0)
        def _():
            o_ref[rows, :] = jnp.zeros((MOE_BLOCK, o_ref.shape[1]), o_ref.dtype)
        return carry

    lax.fori_loop(0, n_sub, block, 0)


def _experts(xb, block_e, block_rows, counts, w_in, b_in, w_out, b_out):
    n_rows, d = xb.shape
    n_e, _, d_ff2 = w_in.shape
    d_ff = d_ff2 // 2
    n_blocks = n_rows // MOE_BLOCK
    valid = block_rows > 0
    prev_e = jnp.concatenate([jnp.full((1,), -1, jnp.int32), block_e[:-1]])
    first = jnp.logical_and(valid, block_e != prev_e).astype(jnp.int32)
    slot = ((jnp.cumsum(first) - 1) % 2).astype(jnp.int32)
    experts = jnp.arange(n_e, dtype=jnp.int32)
    later = jnp.logical_and(experts[None, :] > experts[:, None], counts[None, :] > 0)
    next_of = jnp.min(jnp.where(later, experts[None, :], n_e), axis=1)
    next_of = jnp.where(next_of == n_e, -1, next_of)
    next_e = jnp.sum(jnp.where(block_e[:, None] == experts[None, :], next_of[None, :], 0), axis=1).astype(jnp.int32)
    blk = lambda i, *_: (i, 0)
    whole = lambda i, *_: (0, 0, 0)
    step_rows = EXPERT_SUB * MOE_BLOCK
    grid_spec = pltpu.PrefetchScalarGridSpec(
        num_scalar_prefetch=5,
        grid=(n_blocks // EXPERT_SUB,),
        in_specs=[pl.BlockSpec((step_rows, d), blk),
                  pl.BlockSpec(memory_space=pl.ANY),
                  pl.BlockSpec((n_e, 1, d_ff2), whole),
                  pl.BlockSpec(memory_space=pl.ANY),
                  pl.BlockSpec((n_e, 1, d), whole)],
        out_specs=pl.BlockSpec((step_rows, d), blk),
        scratch_shapes=[pltpu.VMEM((2, d, d_ff2), F32), pltpu.VMEM((2, d_ff, d), F32),
                        pltpu.VMEM((d, d_ff2), BF16), pltpu.VMEM((d_ff, d), BF16),
                        pltpu.SemaphoreType.DMA((2, 2))],
    )
    return pl.pallas_call(
        functools.partial(_expert_kernel, d_ff=d_ff),
        grid_spec=grid_spec,
        out_shape=jax.ShapeDtypeStruct((n_rows, d), BF16),
        compiler_params=_cparams(("arbitrary",)),
        name="moe_experts",
    )(block_e, block_rows, first, slot, next_e, xb, w_in, b_in.reshape(n_e, 1, d_ff2), w_out,
      b_out.reshape(n_e, 1, d))


def _combine_kernel(dst_ref, nseg_ref, yb_ref, gate_ref, pos_ref, x2_ref, npost_ref, g2_ref, o_ref, ybuf, sem):
    tile = pl.program_id(0) * pl.num_programs(1) + pl.program_id(1)
    n_tiles = pl.num_programs(0) * pl.num_programs(1)
    slot = tile % 2

    def fetch(t_idx, s):
        copy = lambda g: _seg_copy(yb_ref, dst_ref[t_idx * SEG_MAX + g], ybuf.at[s], g * SEG_ROWS, sem.at[s])
        _for_segment_pairs(nseg_ref[t_idx], copy, _start)

    @pl.when(tile == 0)
    def _():
        ybuf[...] = jnp.zeros_like(ybuf)
        fetch(0, 0)

    @pl.when(tile + 1 < n_tiles)
    def _():
        fetch(tile + 1, 1 - slot)

    _wait_segments(nseg_ref[tile], yb_ref, ybuf.at[slot], sem.at[slot])
    weights = _sort_onehot(pos_ref[0], ybuf.shape[1], gate_ref[0])
    f = _dot(weights, ybuf[slot])
    o_ref[0] = x2_ref[0] + g2_ref[0] * _rms(f, npost_ref[...])


def _combine(yb_rows, dst_seg, n_seg, gates, pos, x2, npost, g2):
    b, t, d = x2.shape
    row = lambda bi, i, *_: (bi, i, 0)
    grid_spec = pltpu.PrefetchScalarGridSpec(
        num_scalar_prefetch=2,
        grid=(b, t // MOE_TILE),
        in_specs=[pl.BlockSpec(memory_space=pl.ANY),
                  pl.BlockSpec((1, MOE_TILE, gates.shape[2]), row),
                  pl.BlockSpec((1, MOE_TILE, pos.shape[2]), row),
                  pl.BlockSpec((1, MOE_TILE, d), row),
                  pl.BlockSpec((1, d), lambda bi, i, *_: (0, 0)),
                  pl.BlockSpec((1, 1, d), lambda bi, i, *_: (bi, 0, 0))],
        out_specs=pl.BlockSpec((1, MOE_TILE, d), row),
        scratch_shapes=[pltpu.VMEM((2, SEG_MAX * SEG_ROWS, d), BF16), pltpu.SemaphoreType.DMA((2,))],
    )
    return pl.pallas_call(
        _combine_kernel,
        grid_spec=grid_spec,
        out_shape=jax.ShapeDtypeStruct((b, t, d), F32),
        compiler_params=_cparams(("arbitrary", "arbitrary")),
        name="moe_combine",
    )(dst_seg, n_seg, yb_rows, gates, pos, x2, npost, g2)


def _lora_up(w_up):
    _, r, w = w_up.shape
    z = jnp.zeros((r, w), w_up.dtype)
    return jnp.concatenate([jnp.concatenate([w_up[0], z], axis=1),
                            jnp.concatenate([z, w_up[1]], axis=1)], axis=0).astype(BF16)


def _row_tile(t, pref):
    return pref if t % pref == 0 else t


def kernel(x, c, ctx, c_ctx, w_ada, b_ada, norm_pre_mix, norm_post_mix, norm_pre_ffn, norm_post_ffn,
           w_in, mu_shift, w0, w_decay_up, a0, w_iclr_up, k_k, k_a, r_k, w_gate_up, ln_x_w, ln_x_b,
           conv_w, w_out, w_router, b_router, w_exp_in, b_exp_in, w_exp_out, b_exp_out):
    b, t, d = x.shape
    t_ctx = ctx.shape[1]
    depth = w_ada.shape[0]
    width = k_k.shape[1]
    n_groups = width // MXU_WIDTH
    k_off, v_off = 0, width
    decay_off = 2 * width
    iclr_off = decay_off + 2 * DECAY_LORA
    r_off = iclr_off + 2 * ICLR_LORA
    glora_off = r_off + width
    conv_off = glora_off + GATE_LORA
    gate_off = conv_off + 3 * width

    xc = ctx
    for l in range(depth):
        last = l == depth - 1
        rows = jnp.concatenate([c, c_ctx[None, :], jnp.zeros((8 - b - 1, d), F32)], axis=0)
        mod = _ada(rows, w_ada[l], b_ada[l])
        sh1, sc1, g1, sh2, sc2, g2 = [mod[:b, None, j * d:(j + 1) * d] for j in range(6)]
        csh1, csc1, cg1, csh2, csc2, cg2 = [jnp.broadcast_to(mod[b:b + 1, None, j * d:(j + 1) * d], (b, 1, d))
                                            for j in range(6)]

        wl = w_in[l]
        cols = lambda lo, n: wl[:, lo:lo + n]
        w_main = jnp.concatenate([cols(k_off, width), cols(v_off, width), cols(r_off, width)], axis=1).astype(BF16)
        w_lora = jnp.concatenate([cols(decay_off, 2 * DECAY_LORA), cols(iclr_off, 2 * ICLR_LORA),
                                  cols(glora_off, GATE_LORA)], axis=1).astype(BF16)
        w_conv = tuple(cols(conv_off + j * width, width).astype(BF16) for j in range(3))
        w_gate = cols(gate_off, 2 * d).astype(BF16)
        mu = mu_shift[l]
        mu_main = jnp.concatenate([mu[k_off:k_off + width], mu[v_off:v_off + width], mu[r_off:r_off + width]])[None, :]
        mu_lora = jnp.concatenate([mu[decay_off:decay_off + 2 * DECAY_LORA], mu[iclr_off:iclr_off + 2 * ICLR_LORA],
                                   mu[glora_off:glora_off + GATE_LORA]])[None, :]
        sp = {
            "mu_main": mu_main, "mu_lora": mu_lora,
            "w0": w0[l].reshape(1, 2 * width), "w_dec": _lora_up(w_decay_up[l]),
            "a0": a0[l].reshape(1, 2 * width), "w_iclr": _lora_up(w_iclr_up[l]),
            "k_k": k_k[l][None, :], "k_a": k_a[l][None, :],
            "r_k": r_k[l].reshape(1, width), "w_gate": w_gate_up[l].astype(BF16),
        }
        if not last:
            raise NotImplementedError("context stream update for non-final layers")

        n_state_main = 2 * width
        zc_main, zc_lora = _inproj(xc, norm_pre_mix[l], csc1, csh1,
                                   [w_main[:, :n_state_main], w_lora[:, :2 * DECAY_LORA + 2 * ICLR_LORA]],
                                   None, _row_tile(t_ctx, 256))
        sp_ctx = dict(sp, mu_main=mu_main[:, :n_state_main], mu_lora=mu_lora[:, :2 * DECAY_LORA + 2 * ICLR_LORA])
        kap_c, v_c, ld0_c, ld1_c, kd0_c, kd1_c, be0_c, be1_c = _state_terms(
            zc_main, zc_lora, sp_ctx, grid_mode=False, with_read=False, tm=t_ctx)
        s0 = jnp.zeros((b, 2, n_groups, MXU_WIDTH, MXU_WIDTH), F32)
        (s_ctx,) = _wkv_scan([(ld0_c, kd0_c, be0_c), (ld1_c, kd1_c, be1_c)], kap_c, v_c, None, s0)

        zm, zl, zg, zb, p = _inproj(x, norm_pre_mix[l], sc1, sh1, [w_main, w_lora, w_gate], w_conv,
                                    _row_tile(t, 512))
        kap, v, ld0, ld1, kd0, kd1, be0, be1, r, bonus, g = _state_terms(
            zm, zl, sp, grid_mode=True, with_read=True, tm=_row_tile(t, 512))
        yf, yb, _ = _wkv_scan([(ld0, kd0, be0), (ld1, kd1, be1)], kap, v, r, s_ctx)

        lanes = 128
        pad_e = lanes - N_EXPERTS
        pr = {
            "ln_x_w": ln_x_w[l][None, :], "ln_x_b": ln_x_b[l][None, :], "conv_w": conv_w[l],
            "w_out": w_out[l].astype(BF16), "norm_post_mix": norm_post_mix[l][None, :], "g1": g1,
            "norm_pre_ffn": norm_pre_ffn[l][None, :], "sc2": sc2, "sh2": sh2,
            "w_router": jnp.pad(w_router[l], ((0, 0), (0, pad_e))),
            "b_router": jnp.pad(b_router[l], (0, pad_e), constant_values=-jnp.inf)[None, :],
        }
        x2, h2, gates, pos, tile_segs = _merge_route(yf, yb, bonus, g, zb, p, zg, x, pr, MOE_TILE)

        n_tok = b * t
        nk = n_tok * TOP_K
        n_tiles = n_tok // MOE_TILE
        experts = jnp.arange(N_EXPERTS, dtype=jnp.int32)
        run_rows = tile_segs[:, 0, :N_EXPERTS].astype(jnp.int32) * SEG_ROWS
        rows_before = jnp.cumsum(run_rows, axis=0) - run_rows
        counts = jnp.sum(run_rows, axis=0)
        padded = (counts + MOE_BLOCK - 1) // MOE_BLOCK * MOE_BLOCK
        pad_ends = jnp.cumsum(padded)
        pad_starts = pad_ends - padded
        run_off = jnp.cumsum(run_rows, axis=1) - run_rows
        n_seg = (jnp.sum(run_rows, axis=1) // SEG_ROWS).astype(jnp.int32)
        seg_row = jnp.arange(SEG_MAX, dtype=jnp.int32) * SEG_ROWS
        seg_e = jnp.minimum(jnp.sum(((run_off + run_rows)[:, None, :] <= seg_row[None, :, None]).astype(jnp.int32),
                                    axis=2), N_EXPERTS - 1)
        shift = pad_starts[None, :] + rows_before - run_off
        dst_seg = (jnp.sum(jnp.where(seg_e[..., None] == experts, shift[:, None, :], 0), axis=2)
                   + seg_row[None, :]).reshape(n_tiles * SEG_MAX).astype(jnp.int32)
        n_rows_max = nk + n_tiles * N_EXPERTS * (SEG_ROWS - 1) + N_EXPERTS * (MOE_BLOCK - 1)
        n_blocks = -(-n_rows_max // (MOE_BLOCK * EXPERT_SUB)) * EXPERT_SUB
        blk_start = jnp.arange(n_blocks, dtype=jnp.int32) * MOE_BLOCK
        block_e = jnp.minimum(jnp.sum((pad_ends[None, :] <= blk_start[:, None]).astype(jnp.int32), axis=1),
                              N_EXPERTS - 1)
        own = block_e[:, None] == experts[None, :]
        seg_end = jnp.sum(jnp.where(own, (pad_starts + counts)[None, :], 0), axis=1)
        block_rows = jnp.clip(seg_end - blk_start, 0, MOE_BLOCK).astype(jnp.int32)
        xb = _dispatch(h2.reshape(n_tok, d), pos.reshape(n_tok, pos.shape[2]), dst_seg, n_seg, block_rows)
        yb_rows = _experts(xb, block_e, block_rows, counts, w_exp_in[l], b_exp_in[l], w_exp_out[l], b_exp_out[l])
        x = _combine(yb_rows, dst_seg, n_seg, gates, pos, x2, norm_post_ffn[l][None, :], g2)
    return x
```

```python
import functools

import jax
import jax.numpy as jnp
from jax import lax
from jax.experimental import pallas as pl
from jax.experimental.pallas import tpu as pltpu

F32 = jnp.float32
BF16 = jnp.bfloat16

HEAD_DIM = 64
GRID_W = 64
DECAY_LORA = 64
ICLR_LORA = 64
GATE_LORA = 128
GN_EPS = 64e-5
NORM_EPS = 1e-12
CONV_K = 3
N_EXPERTS = 32
TOP_K = 4
SWIGLU_LIMIT = 7.0
SWIGLU_ALPHA = 1.702
MOE_BLOCK = 256
RMS_EPS = 1e-6

MXU_WIDTH = 256
HEADS_PER_GROUP = MXU_WIDTH // HEAD_DIM
CHUNK = 64
SCAN_SUB = 2
DECAY_SCALE = 0.6065306597126334
MOE_TILE = 512
EXPERT_SUB = 8
SEG_ROWS = 16
SEG_MAX = 160
assert SEG_MAX * SEG_ROWS >= MOE_TILE * TOP_K + N_EXPERTS * (SEG_ROWS - 1)
ROW_TILE = 512
CTX_ROW_TILE = 256
VMEM_LIMIT = 56 * 1024 * 1024


def _cparams(sem):
    return pltpu.CompilerParams(dimension_semantics=sem, vmem_limit_bytes=VMEM_LIMIT)


def _dot(a, b):
    return jnp.dot(a, b, preferred_element_type=F32)


def _dot_nt(a, b):
    return lax.dot_general(a, b, (((1,), (1,)), ((), ())), preferred_element_type=F32)


def _dot_tn(a, b):
    return lax.dot_general(a, b, (((0,), (0,)), ((), ())), preferred_element_type=F32)


def _split3(x):
    h = x.astype(BF16)
    r = x - h.astype(F32)
    m = r.astype(BF16)
    l = (r - m.astype(F32)).astype(BF16)
    return h, m, l


def _block_diag_mask(n, blk):
    r = lax.broadcasted_iota(jnp.int32, (n, n), 0) // blk
    c = lax.broadcasted_iota(jnp.int32, (n, n), 1) // blk
    return r == c


def _head_sum(x, ones_bd):
    h = x.astype(BF16)
    l = (x - h.astype(F32)).astype(BF16)
    return _dot(h, ones_bd) + _dot(l, ones_bd)


def _rms(x, gain):
    return x * lax.rsqrt(jnp.mean(x * x, axis=-1, keepdims=True) + RMS_EPS) * gain


def _sigmoid(x):
    return 1.0 / (1.0 + jnp.exp(-x))


def _ada_kernel(c_ref, w_ref, b_ref, o_ref):
    c = c_ref[...]
    s = c * _sigmoid(c)
    o_ref[...] = _dot(s, w_ref[...]) + b_ref[...]


def _ada(c_rows, w, b):
    rows, d = c_rows.shape
    n = w.shape[1]
    tn = 1024
    return pl.pallas_call(
        _ada_kernel,
        grid=(n // tn,),
        in_specs=[pl.BlockSpec((rows, d), lambda j: (0, 0)),
                  pl.BlockSpec((d, tn), lambda j: (0, j)),
                  pl.BlockSpec((1, tn), lambda j: (0, j))],
        out_specs=pl.BlockSpec((rows, tn), lambda j: (0, j)),
        out_shape=jax.ShapeDtypeStruct((rows, n), F32),
        compiler_params=_cparams(("arbitrary",)),
        name="ada_mod",
    )(c_rows, w, b.reshape(1, n))


def _inproj_kernel(x_ref, g_ref, sc_ref, sh_ref, *refs, n_plain, with_conv):
    n_w = n_plain + (3 if with_conv else 0)
    w_refs, o_refs = refs[:n_w], refs[n_w:]
    x = x_ref[0]
    h = _rms(x, g_ref[...]) * (1.0 + sc_ref[0]) + sh_ref[0]
    hb = h.astype(BF16)
    col = 512
    for w_ref, o_ref in zip(w_refs[:n_plain], o_refs[:n_plain]):
        n = w_ref.shape[1]
        for j in range(0, n, col):
            e = min(j + col, n)
            o_ref[0, :, j:e] = _dot(hb, w_ref[:, j:e]).astype(o_ref.dtype)
    if with_conv:
        wb_ref, wc_ref, wx_ref = w_refs[n_plain:]
        ob_ref, op_ref = o_refs[n_plain:]
        n = wb_ref.shape[1]
        for j in range(0, n, col):
            e = min(j + col, n)
            ob_ref[0, :, j:e] = _dot(hb, wb_ref[:, j:e]).astype(ob_ref.dtype)
            op_ref[0, :, j:e] = (_dot(hb, wc_ref[:, j:e]) * _dot(hb, wx_ref[:, j:e])).astype(op_ref.dtype)


def _inproj(x, gain, sc, sh, plain_ws, conv_ws, tm):
    b, t, d = x.shape
    with_conv = conv_ws is not None
    ws = list(plain_ws) + (list(conv_ws) if with_conv else [])
    out_w = [w.shape[1] for w in plain_ws] + ([conv_ws[0].shape[1]] * 2 if with_conv else [])
    row = lambda bi, i: (bi, i, 0)
    const = lambda bi, i: (0, 0)
    in_specs = [pl.BlockSpec((1, tm, d), row),
                pl.BlockSpec((1, d), const),
                pl.BlockSpec((1, 1, d), lambda bi, i: (bi, 0, 0)),
                pl.BlockSpec((1, 1, d), lambda bi, i: (bi, 0, 0))]
    in_specs += [pl.BlockSpec(w.shape, const, pipeline_mode=pl.Buffered(1)) for w in ws]
    return pl.pallas_call(
        functools.partial(_inproj_kernel, n_plain=len(plain_ws), with_conv=with_conv),
        grid=(b, t // tm),
        in_specs=in_specs,
        out_specs=[pl.BlockSpec((1, tm, n), row) for n in out_w],
        out_shape=[jax.ShapeDtypeStruct((b, t, n), BF16) for n in out_w],
        compiler_params=_cparams(("arbitrary", "arbitrary")),
        name="in_proj",
    )(x, gain.reshape(1, d), sc, sh, *ws)


def _shift_mix(cur, prev, nxt, mu, grid_mode, is_first, is_last):
    tm, w = cur.shape
    row = lax.broadcasted_iota(jnp.int32, (tm, w), 0)
    grp = lax.broadcasted_iota(jnp.int32, (tm, w), 1) % 4
    if grid_mode:
        colpos = row % GRID_W
        left = jnp.where(colpos == 0, 0.0, pltpu.roll(cur, 1, axis=0))
        right = jnp.where(colpos == GRID_W - 1, 0.0, pltpu.roll(cur, tm - 1, axis=0))
        prev = jnp.where(is_first, 0.0, prev)
        nxt = jnp.where(is_last, 0.0, nxt)
        up = jnp.concatenate([prev, cur[:tm - GRID_W]], axis=0)
        down = jnp.concatenate([cur[GRID_W:], nxt], axis=0)
        shifted = jnp.where(grp == 0, left, jnp.where(grp == 1, right, jnp.where(grp == 2, up, down)))
    else:
        before = jnp.where(row == 0, 0.0, pltpu.roll(cur, 1, axis=0))
        after = jnp.where(row == tm - 1, 0.0, pltpu.roll(cur, tm - 1, axis=0))
        shifted = jnp.where(grp % 2 == 0, before, after)
    return cur + mu * (shifted - cur)


def _state_kernel(*refs, grid_mode, with_read, width):
    if grid_mode:
        (zm_ref, zmp_ref, zmn_ref, zl_ref, zlp_ref, zln_ref), refs = refs[:6], refs[6:]
    else:
        (zm_ref, zl_ref), refs = refs[:2], refs[2:]
        zmp_ref = zmn_ref = zlp_ref = zln_ref = None
    (mum_ref, mul_ref, w0_ref, wdec_ref, a0_ref, wic_ref, kk_ref, ka_ref), refs = refs[:8], refs[8:]
    if with_read:
        (rk_ref, wg_ref), refs = refs[:2], refs[2:]
    (kap_ref, v_ref, ld0_ref, ld1_ref, kd0_ref, kd1_ref, be0_ref, be1_ref), refs = refs[:8], refs[8:]
    if with_read:
        r_ref, bonus_ref, g_ref = refs

    i = pl.program_id(1)
    is_first = i == 0
    is_last = i == pl.num_programs(1) - 1
    ones_bd = _block_diag_mask(MXU_WIDTH, HEAD_DIM).astype(BF16)

    def shifted(z_ref, zp_ref, zn_ref, mu_ref, lo, hi):
        cur = z_ref[0, :, lo:hi].astype(F32)
        prev = zp_ref[0, :, lo:hi].astype(F32) if grid_mode else None
        nxt = zn_ref[0, :, lo:hi].astype(F32) if grid_mode else None
        return _shift_mix(cur, prev, nxt, mu_ref[:, lo:hi], grid_mode, is_first, is_last)

    lora = shifted(zl_ref, zlp_ref, zln_ref, mul_ref, 0, zl_ref.shape[2])
    lw = jnp.tanh(lora[:, :2 * DECAY_LORA]).astype(BF16)
    la = lora[:, 2 * DECAY_LORA:2 * DECAY_LORA + 2 * ICLR_LORA].astype(BF16)
    if with_read:
        lg = _sigmoid(lora[:, 2 * DECAY_LORA + 2 * ICLR_LORA:]).astype(BF16)

    for lo in range(0, width, MXU_WIDTH):
        hi = lo + MXU_WIDTH
        k = shifted(zm_ref, zmp_ref, zmn_ref, mum_ref, lo, hi)
        v = shifted(zm_ref, zmp_ref, zmn_ref, mum_ref, width + lo, width + hi)
        kk = k * kk_ref[:, lo:hi]
        n2 = _head_sum(kk * kk, ones_bd)
        kk = kk / jnp.maximum(jnp.sqrt(n2), NORM_EPS)
        kap_ref[0, :, lo:hi] = kk.astype(kap_ref.dtype)
        v_ref[0, :, lo:hi] = v.astype(v_ref.dtype)
        a_sum = None
        for d, (ld_ref, kd_ref, be_ref) in enumerate(((ld0_ref, kd0_ref, be0_ref), (ld1_ref, kd1_ref, be1_ref))):
            lo_d, hi_d = d * width + lo, d * width + hi
            pre_w = w0_ref[:, lo_d:hi_d] + _dot(lw, wdec_ref[:, lo_d:hi_d])
            ld_ref[0, :, lo:hi] = -DECAY_SCALE * _sigmoid(pre_w)
            a = _sigmoid(a0_ref[:, lo_d:hi_d] + _dot(la, wic_ref[:, lo_d:hi_d]))
            kd_ref[0, :, lo:hi] = (k * (1.0 + (a - 1.0) * ka_ref[:, lo:hi])).astype(kd_ref.dtype)
            be_ref[0, :, lo:hi] = (kk * a).astype(be_ref.dtype)
            a_sum = a if a_sum is None else a_sum + a
        if with_read:
            r = shifted(zm_ref, zmp_ref, zmn_ref, mum_ref, 2 * width + lo, 2 * width + hi)
            r_ref[0, :, lo:hi] = r.astype(r_ref.dtype)
            k_bonus = k * (1.0 + (0.5 * a_sum - 1.0) * ka_ref[:, lo:hi])
            s = _head_sum(r * k_bonus * rk_ref[:, lo:hi], ones_bd)
            bonus_ref[0, :, lo:hi] = (s * v).astype(bonus_ref.dtype)
            g_ref[0, :, lo:hi] = _dot(lg, wg_ref[:, lo:hi]).astype(g_ref.dtype)


def _state_terms(zm, zl, p, grid_mode, with_read, tm):
    b, t, _ = zm.shape
    width = p["k_k"].shape[1]
    nt = t // tm
    row = lambda bi, i: (bi, i, 0)
    const = lambda bi, i: (0, 0)
    ins, in_specs = [], []
    if grid_mode:
        hb = tm // GRID_W
        last = t // GRID_W - 1
        prev = lambda bi, i: (bi, jnp.maximum(i * hb - 1, 0), 0)
        nxt = lambda bi, i: (bi, jnp.minimum((i + 1) * hb, last), 0)
        for z in (zm, zl):
            ins += [z, z, z]
            in_specs += [pl.BlockSpec((1, tm, z.shape[2]), row),
                         pl.BlockSpec((1, GRID_W, z.shape[2]), prev),
                         pl.BlockSpec((1, GRID_W, z.shape[2]), nxt)]
    else:
        for z in (zm, zl):
            ins.append(z)
            in_specs.append(pl.BlockSpec((1, tm, z.shape[2]), row))
    names = ["mu_main", "mu_lora", "w0", "w_dec", "a0", "w_iclr", "k_k", "k_a"]
    if with_read:
        names += ["r_k", "w_gate"]
    for n in names:
        ins.append(p[n])
        in_specs.append(pl.BlockSpec(p[n].shape, const))
    out_dt = [BF16, BF16, F32, F32, BF16, BF16, BF16, BF16] + ([BF16] * 3 if with_read else [])
    return pl.pallas_call(
        functools.partial(_state_kernel, grid_mode=grid_mode, with_read=with_read, width=width),
        grid=(b, nt),
        in_specs=in_specs,
        out_specs=[pl.BlockSpec((1, tm, width), row) for _ in out_dt],
        out_shape=[jax.ShapeDtypeStruct((b, t, width), dt) for dt in out_dt],
        compiler_params=_cparams(("arbitrary", "arbitrary")),
        name="state_terms_grid" if grid_mode else "state_terms_seq",
    )(*ins)


def _pack_bd(y, bd_mask):
    reps = MXU_WIDTH // y.shape[0]
    return jnp.where(bd_mask, jnp.concatenate([y] * reps, axis=0), jnp.zeros((), y.dtype))


def _scan_chunk(insts, with_y):
    c = CHUNK
    bd_mask = _block_diag_mask(MXU_WIDTH, HEAD_DIM)
    ti = lax.broadcasted_iota(jnp.int32, (c, c), 0)
    tj = lax.broadcasted_iota(jnp.int32, (c, c), 1)
    t4 = lax.broadcasted_iota(jnp.int32, (c, MXU_WIDTH), 0)
    i4 = lax.broadcasted_iota(jnp.int32, (c, MXU_WIDTH), 1) % c
    eye4 = (t4 == i4).astype(F32)
    tri = ((tj <= ti).astype(BF16), (tj >= ti).astype(BF16))
    strict4 = (i4 < t4, i4 > t4)
    incl4 = (i4 <= t4, i4 >= t4)
    pack = lambda y: _pack_bd(y, bd_mask)

    for it in insts:
        rev = it["reverse"]
        ld = it["ld"]
        cl = sum(_dot(tri[rev], t) for t in _split3(ld))
        cl_end = cl[0:1] if rev else cl[c - 1:c]
        rel = cl - cl_end
        e_k = jnp.exp(-rel)
        it["kt"] = (it["kd"] * e_k).astype(BF16)
        it["bt"] = (it["be"] * e_k).astype(BF16)
        it["vb"] = it["v"].astype(BF16)
        kq = (it["kap"] * jnp.exp(rel - ld)).astype(BF16)
        it["lhs"] = jnp.concatenate([kq, (it["r"] * jnp.exp(rel)).astype(BF16)], axis=0) if with_y else kq
        it["sd"] = it["s"] * jnp.exp(cl_end)
    for it in insts:
        it["st"] = _dot_nt(it["lhs"], it["sd"].astype(BF16))
        it["a_b"] = _dot_nt(it["lhs"], pack(it["bt"]))
        it["a_k"] = _dot_nt(it["lhs"], pack(it["kt"]))

    for it in insts:
        a = jnp.where(strict4[it["reverse"]], it["a_b"][:c], 0.0)
        it["t_inv"] = eye4 - a
        ab = a.astype(BF16)
        it["pw"] = _dot(ab, pack(ab)).astype(BF16)
    n = 2
    while 2 * n < c:
        for it in insts:
            both = _dot(jnp.concatenate([it["pw"], it["t_inv"].astype(BF16)], axis=0), pack(it["pw"]))
            it["pw"] = both[:c].astype(BF16)
            it["t_inv"] = it["t_inv"] + both[c:]
        n *= 2
    for it in insts:
        it["t_inv"] = it["t_inv"] + _dot(it["t_inv"].astype(BF16), pack(it["pw"]))

    for it in insts:
        rev = it["reverse"]
        a_kk = jnp.where(strict4[rev], it["a_k"][:c], 0.0)
        if with_y:
            a_rk = jnp.where(incl4[rev], it["a_k"][c:], 0.0)
            it["av"] = _dot(jnp.concatenate([a_kk, a_rk], axis=0).astype(BF16), pack(it["vb"]))
        else:
            it["av"] = _dot(a_kk.astype(BF16), pack(it["vb"]))
    for it in insts:
        rhs = it["st"][:c] + it["av"][:c]
        it["ub"] = _dot(it["t_inv"].astype(BF16), pack(rhs.astype(BF16))).astype(BF16)
    out = []
    for it in insts:
        y = None
        if with_y:
            a_rb = jnp.where(incl4[it["reverse"]], it["a_b"][c:], 0.0)
            y = it["st"][c:] + it["av"][c:] - _dot(a_rb.astype(BF16), pack(it["ub"]))
        delta = _dot_tn(jnp.concatenate([it["vb"], it["ub"]], axis=0),
                        jnp.concatenate([it["kt"], -it["bt"]], axis=0))
        out.append((it["sd"] + jnp.where(bd_mask, delta, 0.0), y))
    return out


def _scan_kernel(*refs, with_y, width):
    n_in = 12 if with_y else 10
    in_refs, refs = refs[:n_in], refs[n_in:]
    s0_ref, refs = refs[0], refs[1:]
    if with_y:
        y_refs, refs = refs[:2], refs[2:]
    sfin_ref, s_scr = refs
    c_idx = pl.program_id(0)
    n_groups = width // MXU_WIDTH

    @pl.when(c_idx == 0)
    def _():
        s_scr[...] = s0_ref[...]

    per_dir = 6 if with_y else 5
    names = ("ld", "kap", "kd", "be", "v", "r")[:per_dir]
    n_batch = in_refs[0].shape[0]
    n_sub = in_refs[0].shape[1] // CHUNK

    def sub_chunk(j, carry):
        insts = []
        for bi in range(n_batch):
            for d in range(2):
                rows = pl.ds(pl.multiple_of((j if d == 0 else n_sub - 1 - j) * CHUNK, CHUNK), CHUNK)
                d_refs = in_refs[d * per_dir:(d + 1) * per_dir]
                for g in range(n_groups):
                    lo, hi = g * MXU_WIDTH, (g + 1) * MXU_WIDTH
                    it = {n: ref[bi, rows, lo:hi].astype(F32) for n, ref in zip(names, d_refs)}
                    it.update(reverse=d, s=s_scr[bi, d, g], at=(bi, d, g), rows=rows)
                    insts.append(it)
        for it, (s_new, y) in zip(insts, _scan_chunk(insts, with_y)):
            bi, d, g = it["at"]
            s_scr[bi, d, g] = s_new
            if with_y:
                y_refs[d][bi, it["rows"], g * MXU_WIDTH:(g + 1) * MXU_WIDTH] = y
        return carry

    lax.fori_loop(0, n_sub, sub_chunk, 0)

    @pl.when(c_idx == pl.num_programs(0) - 1)
    def _():
        sfin_ref[...] = s_scr[...]


def _wkv_scan(dirs, kap, v, r, s0):
    b, t, width = kap.shape
    rows = SCAN_SUB * CHUNK if t % (SCAN_SUB * CHUNK) == 0 else CHUNK
    nc = t // rows
    with_y = r is not None
    fwd = lambda ci: (0, ci, 0)
    bwd = lambda ci: (0, nc - 1 - ci, 0)
    ins, in_specs = [], []
    for d, imap in enumerate((fwd, bwd)):
        ld, kd, be = dirs[d]
        for arr in (ld, kap, kd, be, v) + ((r,) if with_y else ()):
            ins.append(arr)
            in_specs.append(pl.BlockSpec((b, rows, width), imap))
    n_groups = width // MXU_WIDTH
    s_shape = (b, 2, n_groups, MXU_WIDTH, MXU_WIDTH)
    s_spec = pl.BlockSpec(s_shape, lambda ci: (0, 0, 0, 0, 0))
    ins.append(s0)
    in_specs.append(s_spec)
    out_specs, out_shape = [], []
    if with_y:
        out_specs += [pl.BlockSpec((b, rows, width), fwd), pl.BlockSpec((b, rows, width), bwd)]
        out_shape += [jax.ShapeDtypeStruct((b, t, width), F32)] * 2
    out_specs.append(s_spec)
    out_shape.append(jax.ShapeDtypeStruct(s_shape, F32))
    return pl.pallas_call(
        functools.partial(_scan_kernel, with_y=with_y, width=width),
        grid=(nc,),
        in_specs=in_specs,
        out_specs=out_specs,
        out_shape=out_shape,
        scratch_shapes=[pltpu.VMEM(s_shape, F32)],
        compiler_params=_cparams(("arbitrary",)),
        name="wkv_scan" if with_y else "wkv_scan_ctx",
    )(*ins)


def _merge_kernel(yf_ref, yb_ref, bonus_ref, g_ref, zb_ref, p_ref, pp_ref, pn_ref, zg_ref, x_ref,
                  lnw_ref, lnb_ref, cw_ref, wout_ref, npost_ref, g1_ref, npre_ref, sc2_ref, sh2_ref,
                  wr_ref, br_ref,
                  x2_ref, h2_ref, gate_ref, pos_ref, nseg_ref, *, width):
    i = pl.program_id(1)
    tm = x_ref.shape[1]

    ones_bd = _block_diag_mask(MXU_WIDTH, HEAD_DIM).astype(BF16)
    row = lax.broadcasted_iota(jnp.int32, (tm, MXU_WIDTH), 0)
    halo = pp_ref.shape[1]
    parts = []
    for lo in range(0, width, MXU_WIDTH):
        hi = lo + MXU_WIDTH
        y = yf_ref[0, :, lo:hi] + yb_ref[0, :, lo:hi]
        mean = _head_sum(y, ones_bd) * (1.0 / HEAD_DIM)
        yc = y - mean
        var = _head_sum(yc * yc, ones_bd) * (1.0 / HEAD_DIM)
        yn = yc * lax.rsqrt(var + GN_EPS) * lnw_ref[:, lo:hi] + lnb_ref[:, lo:hi]
        y_rwkv = (yn + bonus_ref[0, :, lo:hi].astype(F32)) * g_ref[0, :, lo:hi].astype(F32)
        p = p_ref[0, :, lo:hi].astype(F32)
        p_prev = jnp.where(i == 0, 0.0, pp_ref[0, halo - 1:halo, lo:hi].astype(F32))
        p_next = jnp.where(i == pl.num_programs(1) - 1, 0.0, pn_ref[0, 0:1, lo:hi].astype(F32))
        before = jnp.where(row == 0, p_prev, pltpu.roll(p, 1, axis=0))
        after = jnp.where(row == tm - 1, p_next, pltpu.roll(p, tm - 1, axis=0))
        conv = cw_ref[0:1, lo:hi] * before + cw_ref[1:2, lo:hi] * p + cw_ref[2:3, lo:hi] * after
        y_conv = zb_ref[0, :, lo:hi].astype(F32) * conv
        ga = _sigmoid(zg_ref[0, :, lo:hi].astype(F32))
        gb = _sigmoid(zg_ref[0, :, width + lo:width + hi].astype(F32))
        parts.append((ga * y_rwkv + gb * y_conv).astype(BF16))
    merged = jnp.concatenate(parts, axis=1)
    mix = _dot(merged, wout_ref[...])
    x2 = x_ref[0] + g1_ref[0] * _rms(mix, npost_ref[...])
    x2_ref[0] = x2
    h2 = _rms(x2, npre_ref[...]) * (1.0 + sc2_ref[0]) + sh2_ref[0]
    h2_ref[0] = h2.astype(h2_ref.dtype)

    logits = _dot(h2, wr_ref[...]) + br_ref[...]
    lane = lax.broadcasted_iota(jnp.int32, logits.shape, 1)
    vals, idxs = [], []
    sel = jnp.zeros(logits.shape, F32)
    work = logits
    for _ in range(TOP_K):
        m = jnp.max(work, axis=-1, keepdims=True)
        idx = jnp.min(jnp.where(work == m, lane, logits.shape[1]), axis=-1, keepdims=True)
        hit = lane == idx
        vals.append(m)
        idxs.append(idx)
        sel = jnp.where(hit, 1.0, sel)
        work = jnp.where(hit, -jnp.inf, work)
    exps = [jnp.exp(vk - vals[0]) for vk in vals]
    denom = exps[0] + exps[1] + exps[2] + exps[3]
    r_i = lax.broadcasted_iota(jnp.int32, (tm, tm), 0)
    c_i = lax.broadcasted_iota(jnp.int32, (tm, tm), 1)
    before_cnt = _dot((c_i < r_i).astype(BF16), sel.astype(BF16))
    n_seg = jnp.floor((jnp.sum(sel, axis=0, keepdims=True) + (SEG_ROWS - 1.0)) * (1.0 / SEG_ROWS))
    e_r = lax.broadcasted_iota(jnp.int32, (logits.shape[1], logits.shape[1]), 0)
    e_c = lax.broadcasted_iota(jnp.int32, (logits.shape[1], logits.shape[1]), 1)
    seg_before = _dot(jnp.broadcast_to(n_seg, (8, logits.shape[1])).astype(BF16), (e_r < e_c).astype(BF16))[0:1]
    slot_base = before_cnt + seg_before * float(SEG_ROWS)
    gate_out = jnp.zeros(logits.shape, F32)
    pos_out = jnp.zeros(logits.shape, jnp.int32)
    for k in range(TOP_K):
        pos_k = jnp.sum(jnp.where(lane == idxs[k], slot_base, 0.0), axis=-1, keepdims=True)
        gate_out = jnp.where(lane == k, exps[k] / denom, gate_out)
        pos_out = jnp.where(lane == k, pos_k.astype(jnp.int32), pos_out)
    gate_ref[0] = gate_out
    pos_ref[0] = pos_out
    nseg_ref[0] = n_seg


def _merge_route(yf, yb, bonus, g, zb, p, zg, x, pr, tm):
    b, t, d = x.shape
    width = yf.shape[2]
    halo = 8
    hb = tm // halo
    last = t // halo - 1
    row = lambda bi, i: (bi, i, 0)
    const = lambda bi, i: (0, 0)
    per_b = lambda bi, i: (bi, 0, 0)
    prev = lambda bi, i: (bi, jnp.maximum(i * hb - 1, 0), 0)
    nxt = lambda bi, i: (bi, jnp.minimum((i + 1) * hb, last), 0)
    lanes = 128
    in_specs = [pl.BlockSpec((1, tm, width), row)] * 6
    in_specs += [pl.BlockSpec((1, halo, width), prev), pl.BlockSpec((1, halo, width), nxt),
                 pl.BlockSpec((1, tm, 2 * width), row), pl.BlockSpec((1, tm, d), row)]
    params = [pr["ln_x_w"], pr["ln_x_b"], pr["conv_w"], pr["w_out"], pr["norm_post_mix"]]
    in_specs += [pl.BlockSpec(a.shape, const) for a in params]
    in_specs.append(pl.BlockSpec((1, 1, d), per_b))
    in_specs.append(pl.BlockSpec(pr["norm_pre_ffn"].shape, const))
    in_specs += [pl.BlockSpec((1, 1, d), per_b)] * 2
    in_specs += [pl.BlockSpec(pr["w_router"].shape, const), pl.BlockSpec(pr["b_router"].shape, const)]
    nt = t // tm
    out_specs = [pl.BlockSpec((1, tm, d), row), pl.BlockSpec((1, tm, d), row)]
    out_specs += [pl.BlockSpec((1, tm, lanes), row)] * 2
    out_specs.append(pl.BlockSpec((1, 1, lanes), lambda bi, i: (bi * nt + i, 0, 0)))
    out_shape = [jax.ShapeDtypeStruct((b, t, d), F32), jax.ShapeDtypeStruct((b, t, d), BF16),
                 jax.ShapeDtypeStruct((b, t, lanes), F32), jax.ShapeDtypeStruct((b, t, lanes), jnp.int32),
                 jax.ShapeDtypeStruct((b * nt, 1, lanes), F32)]
    return pl.pallas_call(
        functools.partial(_merge_kernel, width=width),
        grid=(b, nt),
        in_specs=in_specs,
        out_specs=out_specs,
        out_shape=out_shape,
        compiler_params=_cparams(("arbitrary", "arbitrary")),
        name="merge_route",
    )(yf, yb, bonus, g, zb, p, p, p, zg, x, *params, pr["g1"], pr["norm_pre_ffn"], pr["sc2"], pr["sh2"],
      pr["w_router"], pr["b_router"])


def _seg_copy(src_ref, src_row, dst_ref, dst_row, sem):
    al = lambda r: r if isinstance(r, int) else pl.multiple_of(r, SEG_ROWS)
    return pltpu.make_async_copy(src_ref.at[pl.ds(al(src_row), SEG_ROWS)],
                                 dst_ref.at[pl.ds(al(dst_row), SEG_ROWS)], sem)


def _start(cp, priority):
    cp.start(priority=priority)


def _wait_segments(n, src_ref, dst_ref, sem):
    for bit in (128, 64, 32, 16, 8, 4, 2, 1):
        @pl.when((n & bit) != 0)
        def _():
            rows = pl.ds(0, bit * SEG_ROWS)
            pltpu.make_async_copy(src_ref.at[rows], dst_ref.at[rows], sem).wait()


def _for_segment_pairs(n, copy, fn):
    def pair(i, carry):
        fn(copy(2 * i), 0)
        fn(copy(2 * i + 1), 1)
        return carry
    lax.fori_loop(0, n // 2, pair, 0)

    @pl.when(n % 2 == 1)
    def _():
        fn(copy(n - 1), 0)


def _sort_onehot(pos, n_sorted, values=None):
    col = lax.broadcasted_iota(jnp.int32, (pos.shape[0], n_sorted), 1)
    out = jnp.zeros((pos.shape[0], n_sorted), F32)
    for k in range(TOP_K):
        out = jnp.where(col == pos[:, k:k + 1], 1.0 if values is None else values[:, k:k + 1], out)
    return out.astype(BF16)


def _dispatch_kernel(dst_ref, nseg_ref, nrows_ref, h2_ref, pos_ref, xb_ref, zseg, sbuf, sem, zsem):
    step = pl.program_id(0)
    slot = step % 2
    n_fill_blocks = nrows_ref.shape[0]

    def fill_segments(fn):
        def per_block(j, carry):
            def one(q, c2):
                fn(_seg_copy(zseg, 0, xb_ref, j * MOE_BLOCK + q * SEG_ROWS, zsem))
                return c2
            return lax.fori_loop(nrows_ref[j] // SEG_ROWS, MOE_BLOCK // SEG_ROWS, one, carry)
        lax.fori_loop(0, n_fill_blocks, per_block, 0)

    @pl.when(step == 0)
    def _():
        zseg[...] = jnp.zeros_like(zseg)
        fill_segments(lambda cp: cp.start())

    pos_t = pos_ref[...].T
    srow = lax.broadcasted_iota(jnp.int32, (sbuf.shape[1], pos_t.shape[1]), 0)
    onehot = jnp.zeros(srow.shape, F32)
    for k in range(TOP_K):
        onehot = jnp.where(srow == pos_t[k:k + 1, :], 1.0, onehot)
    sbuf[slot] = _dot(onehot.astype(BF16), h2_ref[...]).astype(BF16)

    copy = lambda g: _seg_copy(sbuf.at[slot], g * SEG_ROWS, xb_ref, dst_ref[step * SEG_MAX + g], sem.at[slot])
    _for_segment_pairs(nseg_ref[step], copy, _start)

    @pl.when(step > 0)
    def _():
        _wait_segments(nseg_ref[step - 1], sbuf.at[1 - slot], xb_ref, sem.at[1 - slot])

    @pl.when(step == pl.num_programs(0) - 1)
    def _():
        _wait_segments(nseg_ref[step], sbuf.at[slot], xb_ref, sem.at[slot])
        fill_segments(lambda cp: cp.wait())


def _dispatch(h2, pos, dst_seg, n_seg, block_rows):
    n_tok, d = h2.shape
    n_rows = block_rows.shape[0] * MOE_BLOCK
    tile = lambda i, *_: (i, 0)
    grid_spec = pltpu.PrefetchScalarGridSpec(
        num_scalar_prefetch=3,
        grid=(n_tok // MOE_TILE,),
        in_specs=[pl.BlockSpec((MOE_TILE, d), tile), pl.BlockSpec((MOE_TILE, pos.shape[1]), tile)],
        out_specs=pl.BlockSpec(memory_space=pl.ANY),
        scratch_shapes=[pltpu.VMEM((SEG_ROWS, d), BF16), pltpu.VMEM((2, SEG_MAX * SEG_ROWS, d), BF16),
                        pltpu.SemaphoreType.DMA((2,)), pltpu.SemaphoreType.DMA],
    )
    return pl.pallas_call(
        _dispatch_kernel,
        grid_spec=grid_spec,
        out_shape=jax.ShapeDtypeStruct((n_rows, d), BF16),
        compiler_params=_cparams(("arbitrary",)),
        name="moe_dispatch",
    )(dst_seg, n_seg, block_rows, h2, pos)


def _expert_kernel(be_ref, nrows_ref, first_ref, slot_ref, next_ref, xb_ref, win_hbm, bin_ref, wout_hbm, bout_ref,
                   o_ref, win_f32, wout_f32, win_scr, wout_scr, sem, *, d_ff):
    step = pl.program_id(0)
    n_sub = xb_ref.shape[0] // MOE_BLOCK

    def weight_copies(e, s):
        return (pltpu.make_async_copy(win_hbm.at[e], win_f32.at[s], sem.at[s, 0]),
                pltpu.make_async_copy(wout_hbm.at[e], wout_f32.at[s], sem.at[s, 1]))

    @pl.when(step == 0)
    def _():
        for cp in weight_copies(be_ref[0], 0):
            cp.start()

    def block(sub, carry):
        i = step * n_sub + sub
        rows = pl.ds(pl.multiple_of(sub * MOE_BLOCK, MOE_BLOCK), MOE_BLOCK)
        n_rows = nrows_ref[i]
        e = be_ref[i]

        @pl.when(first_ref[i] == 1)
        def _():
            s = slot_ref[i]
            for cp in weight_copies(e, s):
                cp.wait()
            win_scr[...] = win_f32[s].astype(BF16)
            wout_scr[...] = wout_f32[s].astype(BF16)

            @pl.when(next_ref[i] >= 0)
            def _():
                for cp in weight_copies(next_ref[i], 1 - s):
                    cp.start()

        @pl.when(n_rows > 0)
        def _():
            gu = _dot(xb_ref[rows, :], win_scr[...]) + bin_ref[e]
            gate = jnp.minimum(gu[:, :d_ff], SWIGLU_LIMIT)
            up = jnp.clip(gu[:, d_ff:], -SWIGLU_LIMIT, SWIGLU_LIMIT)
            act = (up + 1.0) * gate * _sigmoid(SWIGLU_ALPHA * gate)
            o_ref[rows, :] = (_dot(act.astype(BF16), wout_scr[...]) + bout_ref[e]).astype(o_ref.dtype)

        @pl.when(n_rows == 0)
        def _():
            o_ref[rows, :] = jnp.zeros((MOE_BLOCK, o_ref.shape[1]), o_ref.dtype)
        return carry

    lax.fori_loop(0, n_sub, block, 0)


def _experts(xb, block_e, block_rows, counts, w_in, b_in, w_out, b_out):
    n_rows, d = xb.shape
    n_e, _, d_ff2 = w_in.shape
    d_ff = d_ff2 // 2
    n_blocks = n_rows // MOE_BLOCK
    valid = block_rows > 0
    prev_e = jnp.concatenate([jnp.full((1,), -1, jnp.int32), block_e[:-1]])
    first = jnp.logical_and(valid, block_e != prev_e).astype(jnp.int32)
    slot = ((jnp.cumsum(first) - 1) % 2).astype(jnp.int32)
    experts = jnp.arange(n_e, dtype=jnp.int32)
    later = jnp.logical_and(experts[None, :] > experts[:, None], counts[None, :] > 0)
    next_of = jnp.min(jnp.where(later, experts[None, :], n_e), axis=1)
    next_of = jnp.where(next_of == n_e, -1, next_of)
    next_e = jnp.sum(jnp.where(block_e[:, None] == experts[None, :], next_of[None, :], 0), axis=1).astype(jnp.int32)
    blk = lambda i, *_: (i, 0)
    whole = lambda i, *_: (0, 0, 0)
    step_rows = EXPERT_SUB * MOE_BLOCK
    grid_spec = pltpu.PrefetchScalarGridSpec(
        num_scalar_prefetch=5,
        grid=(n_blocks // EXPERT_SUB,),
        in_specs=[pl.BlockSpec((step_rows, d), blk),
                  pl.BlockSpec(memory_space=pl.ANY),
                  pl.BlockSpec((n_e, 1, d_ff2), whole),
                  pl.BlockSpec(memory_space=pl.ANY),
                  pl.BlockSpec((n_e, 1, d), whole)],
        out_specs=pl.BlockSpec((step_rows, d), blk),
        scratch_shapes=[pltpu.VMEM((2, d, d_ff2), F32), pltpu.VMEM((2, d_ff, d), F32),
                        pltpu.VMEM((d, d_ff2), BF16), pltpu.VMEM((d_ff, d), BF16),
                        pltpu.SemaphoreType.DMA((2, 2))],
    )
    return pl.pallas_call(
        functools.partial(_expert_kernel, d_ff=d_ff),
        grid_spec=grid_spec,
        out_shape=jax.ShapeDtypeStruct((n_rows, d), BF16),
        compiler_params=_cparams(("arbitrary",)),
        name="moe_experts",
    )(block_e, block_rows, first, slot, next_e, xb, w_in, b_in.reshape(n_e, 1, d_ff2), w_out,
      b_out.reshape(n_e, 1, d))


def _combine_kernel(dst_ref, nseg_ref, yb_ref, gate_ref, pos_ref, x2_ref, npost_ref, g2_ref, o_ref, ybuf, sem):
    tile = pl.program_id(0) * pl.num_programs(1) + pl.program_id(1)
    n_tiles = pl.num_programs(0) * pl.num_programs(1)
    slot = tile % 2

    def fetch(t_idx, s):
        copy = lambda g: _seg_copy(yb_ref, dst_ref[t_idx * SEG_MAX + g], ybuf.at[s], g * SEG_ROWS, sem.at[s])
        _for_segment_pairs(nseg_ref[t_idx], copy, _start)

    @pl.when(tile == 0)
    def _():
        ybuf[...] = jnp.zeros_like(ybuf)
        fetch(0, 0)

    @pl.when(tile + 1 < n_tiles)
    def _():
        fetch(tile + 1, 1 - slot)

    _wait_segments(nseg_ref[tile], yb_ref, ybuf.at[slot], sem.at[slot])
    weights = _sort_onehot(pos_ref[0], ybuf.shape[1], gate_ref[0])
    f = _dot(weights, ybuf[slot])
    o_ref[0] = x2_ref[0] + g2_ref[0] * _rms(f, npost_ref[...])


def _combine(yb_rows, dst_seg, n_seg, gates, pos, x2, npost, g2):
    b, t, d = x2.shape
    row = lambda bi, i, *_: (bi, i, 0)
    grid_spec = pltpu.PrefetchScalarGridSpec(
        num_scalar_prefetch=2,
        grid=(b, t // MOE_TILE),
        in_specs=[pl.BlockSpec(memory_space=pl.ANY),
                  pl.BlockSpec((1, MOE_TILE, gates.shape[2]), row),
                  pl.BlockSpec((1, MOE_TILE, pos.shape[2]), row),
                  pl.BlockSpec((1, MOE_TILE, d), row),
                  pl.BlockSpec((1, d), lambda bi, i, *_: (0, 0)),
                  pl.BlockSpec((1, 1, d), lambda bi, i, *_: (bi, 0, 0))],
        out_specs=pl.BlockSpec((1, MOE_TILE, d), row),
        scratch_shapes=[pltpu.VMEM((2, SEG_MAX * SEG_ROWS, d), BF16), pltpu.SemaphoreType.DMA((2,))],
    )
    return pl.pallas_call(
        _combine_kernel,
        grid_spec=grid_spec,
        out_shape=jax.ShapeDtypeStruct((b, t, d), F32),
        compiler_params=_cparams(("arbitrary", "arbitrary")),
        name="moe_combine",
    )(dst_seg, n_seg, yb_rows, gates, pos, x2, npost, g2)


def _lora_up(w_up):
    _, r, w = w_up.shape
    z = jnp.zeros((r, w), w_up.dtype)
    return jnp.concatenate([jnp.concatenate([w_up[0], z], axis=1),
                            jnp.concatenate([z, w_up[1]], axis=1)], axis=0).astype(BF16)


def _row_tile(t, pref):
    return pref if t % pref == 0 else t


def kernel(x, c, ctx, c_ctx, w_ada, b_ada, norm_pre_mix, norm_post_mix, norm_pre_ffn, norm_post_ffn,
           w_in, mu_shift, w0, w_decay_up, a0, w_iclr_up, k_k, k_a, r_k, w_gate_up, ln_x_w, ln_x_b,
           conv_w, w_out, w_router, b_router, w_exp_in, b_exp_in, w_exp_out, b_exp_out):
    b, t, d = x.shape
    t_ctx = ctx.shape[1]
    depth = w_ada.shape[0]
    width = k_k.shape[1]
    n_groups = width // MXU_WIDTH
    k_off, v_off = 0, width
    decay_off = 2 * width
    iclr_off = decay_off + 2 * DECAY_LORA
    r_off = iclr_off + 2 * ICLR_LORA
    glora_off = r_off + width
    conv_off = glora_off + GATE_LORA
    gate_off = conv_off + 3 * width

    xc = ctx
    for l in range(depth):
        last = l == depth - 1
        rows = jnp.concatenate([c, c_ctx[None, :], jnp.zeros((8 - b - 1, d), F32)], axis=0)
        mod = _ada(rows, w_ada[l], b_ada[l])
        sh1, sc1, g1, sh2, sc2, g2 = [mod[:b, None, j * d:(j + 1) * d] for j in range(6)]
        csh1, csc1, cg1, csh2, csc2, cg2 = [jnp.broadcast_to(mod[b:b + 1, None, j * d:(j + 1) * d], (b, 1, d))
                                            for j in range(6)]

        wl = w_in[l]
        cols = lambda lo, n: wl[:, lo:lo + n]
        w_main = jnp.concatenate([cols(k_off, width), cols(v_off, width), cols(r_off, width)], axis=1).astype(BF16)
        w_lora = jnp.concatenate([cols(decay_off, 2 * DECAY_LORA), cols(iclr_off, 2 * ICLR_LORA),
                                  cols(glora_off, GATE_LORA)], axis=1).astype(BF16)
        w_conv = tuple(cols(conv_off + j * width, width).astype(BF16) for j in range(3))
        w_gate = cols(gate_off, 2 * d).astype(BF16)
        mu = mu_shift[l]
        mu_main = jnp.concatenate([mu[k_off:k_off + width], mu[v_off:v_off + width], mu[r_off:r_off + width]])[None, :]
        mu_lora = jnp.concatenate([mu[decay_off:decay_off + 2 * DECAY_LORA], mu[iclr_off:iclr_off + 2 * ICLR_LORA],
                                   mu[glora_off:glora_off + GATE_LORA]])[None, :]
        sp = {
            "mu_main": mu_main, "mu_lora": mu_lora,
            "w0": w0[l].reshape(1, 2 * width), "w_dec": _lora_up(w_decay_up[l]),
            "a0": a0[l].reshape(1, 2 * width), "w_iclr": _lora_up(w_iclr_up[l]),
            "k_k": k_k[l][None, :], "k_a": k_a[l][None, :],
            "r_k": r_k[l].reshape(1, width), "w_gate": w_gate_up[l].astype(BF16),
        }
        if not last:
            raise NotImplementedError("context stream update for non-final layers")

        n_state_main = 2 * width
        zc_main, zc_lora = _inproj(xc, norm_pre_mix[l], csc1, csh1,
                                   [w_main[:, :n_state_main], w_lora[:, :2 * DECAY_LORA + 2 * ICLR_LORA]],
                                   None, _row_tile(t_ctx, CTX_ROW_TILE))
        sp_ctx = dict(sp, mu_main=mu_main[:, :n_state_main], mu_lora=mu_lora[:, :2 * DECAY_LORA + 2 * ICLR_LORA])
        kap_c, v_c, ld0_c, ld1_c, kd0_c, kd1_c, be0_c, be1_c = _state_terms(
            zc_main, zc_lora, sp_ctx, grid_mode=False, with_read=False, tm=t_ctx)
        s0 = jnp.zeros((b, 2, n_groups, MXU_WIDTH, MXU_WIDTH), F32)
        (s_ctx,) = _wkv_scan([(ld0_c, kd0_c, be0_c), (ld1_c, kd1_c, be1_c)], kap_c, v_c, None, s0)

        zm, zl, zg, zb, p = _inproj(x, norm_pre_mix[l], sc1, sh1, [w_main, w_lora, w_gate], w_conv,
                                    _row_tile(t, ROW_TILE))
        kap, v, ld0, ld1, kd0, kd1, be0, be1, r, bonus, g = _state_terms(
            zm, zl, sp, grid_mode=True, with_read=True, tm=_row_tile(t, ROW_TILE))
        yf, yb, _ = _wkv_scan([(ld0, kd0, be0), (ld1, kd1, be1)], kap, v, r, s_ctx)

        lanes = 128
        pad_e = lanes - N_EXPERTS
        pr = {
            "ln_x_w": ln_x_w[l][None, :], "ln_x_b": ln_x_b[l][None, :], "conv_w": conv_w[l],
            "w_out": w_out[l].astype(BF16), "norm_post_mix": norm_post_mix[l][None, :], "g1": g1,
            "norm_pre_ffn": norm_pre_ffn[l][None, :], "sc2": sc2, "sh2": sh2,
            "w_router": jnp.pad(w_router[l], ((0, 0), (0, pad_e))),
            "b_router": jnp.pad(b_router[l], (0, pad_e), constant_values=-jnp.inf)[None, :],
        }
        x2, h2, gates, pos, tile_segs = _merge_route(yf, yb, bonus, g, zb, p, zg, x, pr, MOE_TILE)

        n_tok = b * t
        nk = n_tok * TOP_K
        n_tiles = n_tok // MOE_TILE
        experts = jnp.arange(N_EXPERTS, dtype=jnp.int32)
        run_rows = tile_segs[:, 0, :N_EXPERTS].astype(jnp.int32) * SEG_ROWS
        rows_before = jnp.cumsum(run_rows, axis=0) - run_rows
        counts = jnp.sum(run_rows, axis=0)
        padded = (counts + MOE_BLOCK - 1) // MOE_BLOCK * MOE_BLOCK
        pad_ends = jnp.cumsum(padded)
        pad_starts = pad_ends - padded
        run_off = jnp.cumsum(run_rows, axis=1) - run_rows
        n_seg = (jnp.sum(run_rows, axis=1) // SEG_ROWS).astype(jnp.int32)
        seg_row = jnp.arange(SEG_MAX, dtype=jnp.int32) * SEG_ROWS
        seg_e = jnp.minimum(jnp.sum(((run_off + run_rows)[:, None, :] <= seg_row[None, :, None]).astype(jnp.int32),
                                    axis=2), N_EXPERTS - 1)
        shift = pad_starts[None, :] + rows_before - run_off
        dst_seg = (jnp.sum(jnp.where(seg_e[..., None] == experts, shift[:, None, :], 0), axis=2)
                   + seg_row[None, :]).reshape(n_tiles * SEG_MAX).astype(jnp.int32)
        n_rows_max = nk + n_tiles * N_EXPERTS * (SEG_ROWS - 1) + N_EXPERTS * (MOE_BLOCK - 1)
        n_blocks = -(-n_rows_max // (MOE_BLOCK * EXPERT_SUB)) * EXPERT_SUB
        blk_start = jnp.arange(n_blocks, dtype=jnp.int32) * MOE_BLOCK
        block_e = jnp.minimum(jnp.sum((pad_ends[None, :] <= blk_start[:, None]).astype(jnp.int32), axis=1),
                              N_EXPERTS - 1)
        own = block_e[:, None] == experts[None, :]
        seg_end = jnp.sum(jnp.where(own, (pad_starts + counts)[None, :], 0), axis=1)
        block_rows = jnp.clip(seg_end - blk_start, 0, MOE_BLOCK).astype(jnp.int32)
        xb = _dispatch(h2.reshape(n_tok, d), pos.reshape(n_tok, pos.shape[2]), dst_seg, n_seg, block_rows)
        yb_rows = _experts(xb, block_e, block_rows, counts, w_exp_in[l], b_exp_in[l], w_exp_out[l], b_exp_out[l])
        x = _combine(yb_rows, dst_seg, n_seg, gates, pos, x2, norm_post_ffn[l][None, :], g2)
    return x
```

```python
import functools

import jax
import jax.numpy as jnp
from jax import lax
from jax.experimental import pallas as pl
from jax.experimental.pallas import tpu as pltpu

F32 = jnp.float32
BF16 = jnp.bfloat16

HEAD_DIM = 64
GRID_W = 64
DECAY_LORA = 64
ICLR_LORA = 64
GATE_LORA = 128
GN_EPS = 64e-5
NORM_EPS = 1e-12
CONV_K = 3
N_EXPERTS = 32
TOP_K = 4
SWIGLU_LIMIT = 7.0
SWIGLU_ALPHA = 1.702
MOE_BLOCK = 256
RMS_EPS = 1e-6

MXU_WIDTH = 256
HEADS_PER_GROUP = MXU_WIDTH // HEAD_DIM
CHUNK = 64
SCAN_SUB = 4
DECAY_SCALE = 0.6065306597126334
MOE_TILE = 512
EXPERT_SUB = 8
SEG_ROWS = 16
SEG_MAX = 160
assert SEG_MAX * SEG_ROWS >= MOE_TILE * TOP_K + N_EXPERTS * (SEG_ROWS - 1)
ROW_TILE = 512
CTX_ROW_TILE = 256
VMEM_LIMIT = 56 * 1024 * 1024


def _cparams(sem):
    return pltpu.CompilerParams(dimension_semantics=sem, vmem_limit_bytes=VMEM_LIMIT)


def _dot(a, b):
    return jnp.dot(a, b, preferred_element_type=F32)


def _dot_nt(a, b):
    return lax.dot_general(a, b, (((1,), (1,)), ((), ())), preferred_element_type=F32)


def _dot_tn(a, b):
    return lax.dot_general(a, b, (((0,), (0,)), ((), ())), preferred_element_type=F32)


def _split3(x):
    h = x.astype(BF16)
    r = x - h.astype(F32)
    m = r.astype(BF16)
    l = (r - m.astype(F32)).astype(BF16)
    return h, m, l


def _block_diag_mask(n, blk):
    r = lax.broadcasted_iota(jnp.int32, (n, n), 0) // blk
    c = lax.broadcasted_iota(jnp.int32, (n, n), 1) // blk
    return r == c


def _head_sum(x, ones_bd):
    h = x.astype(BF16)
    l = (x - h.astype(F32)).astype(BF16)
    return _dot(h, ones_bd) + _dot(l, ones_bd)


def _rms(x, gain):
    return x * lax.rsqrt(jnp.mean(x * x, axis=-1, keepdims=True) + RMS_EPS) * gain


def _sigmoid(x):
    return 1.0 / (1.0 + jnp.exp(-x))


def _ada_kernel(c_ref, w_ref, b_ref, o_ref):
    c = c_ref[...]
    s = c * _sigmoid(c)
    o_ref[...] = _dot(s, w_ref[...]) + b_ref[...]


def _ada(c_rows, w, b):
    rows, d = c_rows.shape
    n = w.shape[1]
    tn = 1024
    return pl.pallas_call(
        _ada_kernel,
        grid=(n // tn,),
        in_specs=[pl.BlockSpec((rows, d), lambda j: (0, 0)),
                  pl.BlockSpec((d, tn), lambda j: (0, j)),
                  pl.BlockSpec((1, tn), lambda j: (0, j))],
        out_specs=pl.BlockSpec((rows, tn), lambda j: (0, j)),
        out_shape=jax.ShapeDtypeStruct((rows, n), F32),
        compiler_params=_cparams(("arbitrary",)),
        name="ada_mod",
    )(c_rows, w, b.reshape(1, n))


def _inproj_kernel(x_ref, g_ref, sc_ref, sh_ref, *refs, n_plain, with_conv):
    n_w = n_plain + (3 if with_conv else 0)
    w_refs, o_refs = refs[:n_w], refs[n_w:]
    x = x_ref[0]
    h = _rms(x, g_ref[...]) * (1.0 + sc_ref[0]) + sh_ref[0]
    hb = h.astype(BF16)
    col = 512
    for w_ref, o_ref in zip(w_refs[:n_plain], o_refs[:n_plain]):
        n = w_ref.shape[1]
        for j in range(0, n, col):
            e = min(j + col, n)
            o_ref[0, :, j:e] = _dot(hb, w_ref[:, j:e]).astype(o_ref.dtype)
    if with_conv:
        wb_ref, wc_ref, wx_ref = w_refs[n_plain:]
        ob_ref, op_ref = o_refs[n_plain:]
        n = wb_ref.shape[1]
        for j in range(0, n, col):
            e = min(j + col, n)
            ob_ref[0, :, j:e] = _dot(hb, wb_ref[:, j:e]).astype(ob_ref.dtype)
            op_ref[0, :, j:e] = (_dot(hb, wc_ref[:, j:e]) * _dot(hb, wx_ref[:, j:e])).astype(op_ref.dtype)


def _inproj(x, gain, sc, sh, plain_ws, conv_ws, tm):
    b, t, d = x.shape
    with_conv = conv_ws is not None
    ws = list(plain_ws) + (list(conv_ws) if with_conv else [])
    out_w = [w.shape[1] for w in plain_ws] + ([conv_ws[0].shape[1]] * 2 if with_conv else [])
    row = lambda bi, i: (bi, i, 0)
    const = lambda bi, i: (0, 0)
    in_specs = [pl.BlockSpec((1, tm, d), row),
                pl.BlockSpec((1, d), const),
                pl.BlockSpec((1, 1, d), lambda bi, i: (bi, 0, 0)),
                pl.BlockSpec((1, 1, d), lambda bi, i: (bi, 0, 0))]
    in_specs += [pl.BlockSpec(w.shape, const, pipeline_mode=pl.Buffered(1)) for w in ws]
    return pl.pallas_call(
        functools.partial(_inproj_kernel, n_plain=len(plain_ws), with_conv=with_conv),
        grid=(b, t // tm),
        in_specs=in_specs,
        out_specs=[pl.BlockSpec((1, tm, n), row) for n in out_w],
        out_shape=[jax.ShapeDtypeStruct((b, t, n), BF16) for n in out_w],
        compiler_params=_cparams(("arbitrary", "arbitrary")),
        name="in_proj",
    )(x, gain.reshape(1, d), sc, sh, *ws)


def _shift_mix(cur, prev, nxt, mu, grid_mode, is_first, is_last):
    tm, w = cur.shape
    row = lax.broadcasted_iota(jnp.int32, (tm, w), 0)
    grp = lax.broadcasted_iota(jnp.int32, (tm, w), 1) % 4
    if grid_mode:
        colpos = row % GRID_W
        left = jnp.where(colpos == 0, 0.0, pltpu.roll(cur, 1, axis=0))
        right = jnp.where(colpos == GRID_W - 1, 0.0, pltpu.roll(cur, tm - 1, axis=0))
        prev = jnp.where(is_first, 0.0, prev)
        nxt = jnp.where(is_last, 0.0, nxt)
        up = jnp.concatenate([prev, cur[:tm - GRID_W]], axis=0)
        down = jnp.concatenate([cur[GRID_W:], nxt], axis=0)
        shifted = jnp.where(grp == 0, left, jnp.where(grp == 1, right, jnp.where(grp == 2, up, down)))
    else:
        before = jnp.where(row == 0, 0.0, pltpu.roll(cur, 1, axis=0))
        after = jnp.where(row == tm - 1, 0.0, pltpu.roll(cur, tm - 1, axis=0))
        shifted = jnp.where(grp % 2 == 0, before, after)
    return cur + mu * (shifted - cur)


def _state_kernel(*refs, grid_mode, with_read, width):
    if grid_mode:
        (zm_ref, zmp_ref, zmn_ref, zl_ref, zlp_ref, zln_ref), refs = refs[:6], refs[6:]
    else:
        (zm_ref, zl_ref), refs = refs[:2], refs[2:]
        zmp_ref = zmn_ref = zlp_ref = zln_ref = None
    (mum_ref, mul_ref, w0_ref, wdec_ref, a0_ref, wic_ref, kk_ref, ka_ref), refs = refs[:8], refs[8:]
    if with_read:
        (rk_ref, wg_ref), refs = refs[:2], refs[2:]
    (kap_ref, v_ref, ld0_ref, ld1_ref, kd0_ref, kd1_ref, be0_ref, be1_ref), refs = refs[:8], refs[8:]
    if with_read:
        r_ref, bonus_ref, g_ref = refs

    i = pl.program_id(1)
    is_first = i == 0
    is_last = i == pl.num_programs(1) - 1
    ones_bd = _block_diag_mask(MXU_WIDTH, HEAD_DIM).astype(BF16)

    def shifted(z_ref, zp_ref, zn_ref, mu_ref, lo, hi):
        cur = z_ref[0, :, lo:hi].astype(F32)
        prev = zp_ref[0, :, lo:hi].astype(F32) if grid_mode else None
        nxt = zn_ref[0, :, lo:hi].astype(F32) if grid_mode else None
        return _shift_mix(cur, prev, nxt, mu_ref[:, lo:hi], grid_mode, is_first, is_last)

    lora = shifted(zl_ref, zlp_ref, zln_ref, mul_ref, 0, zl_ref.shape[2])
    lw = jnp.tanh(lora[:, :2 * DECAY_LORA]).astype(BF16)
    la = lora[:, 2 * DECAY_LORA:2 * DECAY_LORA + 2 * ICLR_LORA].astype(BF16)
    if with_read:
        lg = _sigmoid(lora[:, 2 * DECAY_LORA + 2 * ICLR_LORA:]).astype(BF16)

    for lo in range(0, width, MXU_WIDTH):
        hi = lo + MXU_WIDTH
        k = shifted(zm_ref, zmp_ref, zmn_ref, mum_ref, lo, hi)
        v = shifted(zm_ref, zmp_ref, zmn_ref, mum_ref, width + lo, width + hi)
        kk = k * kk_ref[:, lo:hi]
        n2 = _head_sum(kk * kk, ones_bd)
        kk = kk / jnp.maximum(jnp.sqrt(n2), NORM_EPS)
        kap_ref[0, :, lo:hi] = kk.astype(kap_ref.dtype)
        v_ref[0, :, lo:hi] = v.astype(v_ref.dtype)
        a_sum = None
        for d, (ld_ref, kd_ref, be_ref) in enumerate(((ld0_ref, kd0_ref, be0_ref), (ld1_ref, kd1_ref, be1_ref))):
            lo_d, hi_d = d * width + lo, d * width + hi
            pre_w = w0_ref[:, lo_d:hi_d] + _dot(lw, wdec_ref[:, lo_d:hi_d])
            ld_ref[0, :, lo:hi] = -DECAY_SCALE * _sigmoid(pre_w)
            a = _sigmoid(a0_ref[:, lo_d:hi_d] + _dot(la, wic_ref[:, lo_d:hi_d]))
            kd_ref[0, :, lo:hi] = (k * (1.0 + (a - 1.0) * ka_ref[:, lo:hi])).astype(kd_ref.dtype)
            be_ref[0, :, lo:hi] = (kk * a).astype(be_ref.dtype)
            a_sum = a if a_sum is None else a_sum + a
        if with_read:
            r = shifted(zm_ref, zmp_ref, zmn_ref, mum_ref, 2 * width + lo, 2 * width + hi)
            r_ref[0, :, lo:hi] = r.astype(r_ref.dtype)
            k_bonus = k * (1.0 + (0.5 * a_sum - 1.0) * ka_ref[:, lo:hi])
            s = _head_sum(r * k_bonus * rk_ref[:, lo:hi], ones_bd)
            bonus_ref[0, :, lo:hi] = (s * v).astype(bonus_ref.dtype)
            g_ref[0, :, lo:hi] = _dot(lg, wg_ref[:, lo:hi]).astype(g_ref.dtype)


def _state_terms(zm, zl, p, grid_mode, with_read, tm):
    b, t, _ = zm.shape
    width = p["k_k"].shape[1]
    nt = t // tm
    row = lambda bi, i: (bi, i, 0)
    const = lambda bi, i: (0, 0)
    ins, in_specs = [], []
    if grid_mode:
        hb = tm // GRID_W
        last = t // GRID_W - 1
        prev = lambda bi, i: (bi, jnp.maximum(i * hb - 1, 0), 0)
        nxt = lambda bi, i: (bi, jnp.minimum((i + 1) * hb, last), 0)
        for z in (zm, zl):
            ins += [z, z, z]
            in_specs += [pl.BlockSpec((1, tm, z.shape[2]), row),
                         pl.BlockSpec((1, GRID_W, z.shape[2]), prev),
                         pl.BlockSpec((1, GRID_W, z.shape[2]), nxt)]
    else:
        for z in (zm, zl):
            ins.append(z)
            in_specs.append(pl.BlockSpec((1, tm, z.shape[2]), row))
    names = ["mu_main", "mu_lora", "w0", "w_dec", "a0", "w_iclr", "k_k", "k_a"]
    if with_read:
        names += ["r_k", "w_gate"]
    for n in names:
        ins.append(p[n])
        in_specs.append(pl.BlockSpec(p[n].shape, const))
    out_dt = [BF16, BF16, F32, F32, BF16, BF16, BF16, BF16] + ([BF16] * 3 if with_read else [])
    return pl.pallas_call(
        functools.partial(_state_kernel, grid_mode=grid_mode, with_read=with_read, width=width),
        grid=(b, nt),
        in_specs=in_specs,
        out_specs=[pl.BlockSpec((1, tm, width), row) for _ in out_dt],
        out_shape=[jax.ShapeDtypeStruct((b, t, width), dt) for dt in out_dt],
        compiler_params=_cparams(("arbitrary", "arbitrary")),
        name="state_terms_grid" if grid_mode else "state_terms_seq",
    )(*ins)


def _pack_bd(y, bd_mask):
    reps = MXU_WIDTH // y.shape[0]
    return jnp.where(bd_mask, jnp.concatenate([y] * reps, axis=0), jnp.zeros((), y.dtype))


def _scan_chunk(insts, with_y):
    c = CHUNK
    bd_mask = _block_diag_mask(MXU_WIDTH, HEAD_DIM)
    ti = lax.broadcasted_iota(jnp.int32, (c, c), 0)
    tj = lax.broadcasted_iota(jnp.int32, (c, c), 1)
    t4 = lax.broadcasted_iota(jnp.int32, (c, MXU_WIDTH), 0)
    i4 = lax.broadcasted_iota(jnp.int32, (c, MXU_WIDTH), 1) % c
    eye4 = (t4 == i4).astype(F32)
    tri = ((tj <= ti).astype(BF16), (tj >= ti).astype(BF16))
    strict4 = (i4 < t4, i4 > t4)
    incl4 = (i4 <= t4, i4 >= t4)
    pack = lambda y: _pack_bd(y, bd_mask)

    for it in insts:
        rev = it["reverse"]
        ld = it["ld"]
        cl = sum(_dot(tri[rev], t) for t in _split3(ld))
        cl_end = cl[0:1] if rev else cl[c - 1:c]
        rel = cl - cl_end
        e_k = jnp.exp(-rel)
        it["kt"] = (it["kd"] * e_k).astype(BF16)
        it["bt"] = (it["be"] * e_k).astype(BF16)
        it["vb"] = it["v"].astype(BF16)
        kq = (it["kap"] * jnp.exp(rel - ld)).astype(BF16)
        it["lhs"] = jnp.concatenate([kq, (it["r"] * jnp.exp(rel)).astype(BF16)], axis=0) if with_y else kq
        it["sd"] = it["s"] * jnp.exp(cl_end)
    for it in insts:
        it["st"] = _dot_nt(it["lhs"], it["sd"].astype(BF16))
        it["a_b"] = _dot_nt(it["lhs"], pack(it["bt"]))
        it["a_k"] = _dot_nt(it["lhs"], pack(it["kt"]))

    for it in insts:
        a = jnp.where(strict4[it["reverse"]], it["a_b"][:c], 0.0)
        it["t_inv"] = eye4 - a
        ab = a.astype(BF16)
        it["pw"] = _dot(ab, pack(ab)).astype(BF16)
    n = 2
    while 2 * n < c:
        for it in insts:
            both = _dot(jnp.concatenate([it["pw"], it["t_inv"].astype(BF16)], axis=0), pack(it["pw"]))
            it["pw"] = both[:c].astype(BF16)
            it["t_inv"] = it["t_inv"] + both[c:]
        n *= 2
    for it in insts:
        it["t_inv"] = it["t_inv"] + _dot(it["t_inv"].astype(BF16), pack(it["pw"]))

    for it in insts:
        rev = it["reverse"]
        a_kk = jnp.where(strict4[rev], it["a_k"][:c], 0.0)
        if with_y:
            a_rk = jnp.where(incl4[rev], it["a_k"][c:], 0.0)
            it["av"] = _dot(jnp.concatenate([a_kk, a_rk], axis=0).astype(BF16), pack(it["vb"]))
        else:
            it["av"] = _dot(a_kk.astype(BF16), pack(it["vb"]))
    for it in insts:
        rhs = it["st"][:c] + it["av"][:c]
        it["ub"] = _dot(it["t_inv"].astype(BF16), pack(rhs.astype(BF16))).astype(BF16)
    out = []
    for it in insts:
        y = None
        if with_y:
            a_rb = jnp.where(incl4[it["reverse"]], it["a_b"][c:], 0.0)
            y = it["st"][c:] + it["av"][c:] - _dot(a_rb.astype(BF16), pack(it["ub"]))
        delta = _dot_tn(jnp.concatenate([it["vb"], it["ub"]], axis=0),
                        jnp.concatenate([it["kt"], -it["bt"]], axis=0))
        out.append((it["sd"] + jnp.where(bd_mask, delta, 0.0), y))
    return out


def _scan_kernel(*refs, with_y, width):
    n_in = 12 if with_y else 10
    in_refs, refs = refs[:n_in], refs[n_in:]
    s0_ref, refs = refs[0], refs[1:]
    if with_y:
        y_refs, s_scr = refs[:2], refs[2]
        sfin_ref = None
    else:
        sfin_ref, s_scr = refs
    c_idx = pl.program_id(0)
    n_groups = width // MXU_WIDTH

    @pl.when(c_idx == 0)
    def _():
        s_scr[...] = s0_ref[...]

    per_dir = 6 if with_y else 5
    names = ("ld", "kap", "kd", "be", "v", "r")[:per_dir]
    n_batch = in_refs[0].shape[0]
    n_sub = in_refs[0].shape[1] // CHUNK

    def sub_chunk(j, carry):
        insts = []
        for bi in range(n_batch):
            for d in range(2):
                rows = pl.ds(pl.multiple_of((j if d == 0 else n_sub - 1 - j) * CHUNK, CHUNK), CHUNK)
                d_refs = in_refs[d * per_dir:(d + 1) * per_dir]
                for g in range(n_groups):
                    lo, hi = g * MXU_WIDTH, (g + 1) * MXU_WIDTH
                    it = {n: ref[bi, rows, lo:hi].astype(F32) for n, ref in zip(names, d_refs)}
                    it.update(reverse=d, s=s_scr[bi, d, g], at=(bi, d, g), rows=rows)
                    insts.append(it)
        for it, (s_new, y) in zip(insts, _scan_chunk(insts, with_y)):
            bi, d, g = it["at"]
            s_scr[bi, d, g] = s_new
            if with_y:
                y_refs[d][bi, it["rows"], g * MXU_WIDTH:(g + 1) * MXU_WIDTH] = y
        return carry

    lax.fori_loop(0, n_sub, sub_chunk, 0)

    if sfin_ref is not None:
        @pl.when(c_idx == pl.num_programs(0) - 1)
        def _():
            sfin_ref[...] = s_scr[...]


def _wkv_scan(dirs, kap, v, r, s0):
    b, t, width = kap.shape
    rows = SCAN_SUB * CHUNK if t % (SCAN_SUB * CHUNK) == 0 else CHUNK
    nc = t // rows
    with_y = r is not None
    fwd = lambda ci: (0, ci, 0)
    bwd = lambda ci: (0, nc - 1 - ci, 0)
    ins, in_specs = [], []
    for d, imap in enumerate((fwd, bwd)):
        ld, kd, be = dirs[d]
        for arr in (ld, kap, kd, be, v) + ((r,) if with_y else ()):
            ins.append(arr)
            in_specs.append(pl.BlockSpec((b, rows, width), imap))
    n_groups = width // MXU_WIDTH
    s_shape = (b, 2, n_groups, MXU_WIDTH, MXU_WIDTH)
    s_spec = pl.BlockSpec(s_shape, lambda ci: (0, 0, 0, 0, 0))
    ins.append(s0)
    in_specs.append(s_spec)
    out_specs, out_shape = [], []
    if with_y:
        out_specs += [pl.BlockSpec((b, rows, width), fwd), pl.BlockSpec((b, rows, width), bwd)]
        out_shape += [jax.ShapeDtypeStruct((b, t, width), F32)] * 2
    else:
        out_specs.append(s_spec)
        out_shape.append(jax.ShapeDtypeStruct(s_shape, F32))
    return pl.pallas_call(
        functools.partial(_scan_kernel, with_y=with_y, width=width),
        grid=(nc,),
        in_specs=in_specs,
        out_specs=out_specs,
        out_shape=out_shape,
        scratch_shapes=[pltpu.VMEM(s_shape, F32)],
        compiler_params=_cparams(("arbitrary",)),
        name="wkv_scan" if with_y else "wkv_scan_ctx",
    )(*ins)


def _merge_kernel(yf_ref, yb_ref, bonus_ref, g_ref, zb_ref, p_ref, pp_ref, pn_ref, zg_ref, x_ref,
                  lnw_ref, lnb_ref, cw_ref, wout_ref, npost_ref, g1_ref, npre_ref, sc2_ref, sh2_ref,
                  wr_ref, br_ref,
                  x2_ref, h2_ref, gate_ref, pos_ref, nseg_ref, *, width):
    i = pl.program_id(1)
    tm = x_ref.shape[1]

    ones_bd = _block_diag_mask(MXU_WIDTH, HEAD_DIM).astype(BF16)
    row = lax.broadcasted_iota(jnp.int32, (tm, MXU_WIDTH), 0)
    halo = pp_ref.shape[1]
    parts = []
    for lo in range(0, width, MXU_WIDTH):
        hi = lo + MXU_WIDTH
        y = yf_ref[0, :, lo:hi] + yb_ref[0, :, lo:hi]
        mean = _head_sum(y, ones_bd) * (1.0 / HEAD_DIM)
        yc = y - mean
        var = _head_sum(yc * yc, ones_bd) * (1.0 / HEAD_DIM)
        yn = yc * lax.rsqrt(var + GN_EPS) * lnw_ref[:, lo:hi] + lnb_ref[:, lo:hi]
        y_rwkv = (yn + bonus_ref[0, :, lo:hi].astype(F32)) * g_ref[0, :, lo:hi].astype(F32)
        p = p_ref[0, :, lo:hi].astype(F32)
        p_prev = jnp.where(i == 0, 0.0, pp_ref[0, halo - 1:halo, lo:hi].astype(F32))
        p_next = jnp.where(i == pl.num_programs(1) - 1, 0.0, pn_ref[0, 0:1, lo:hi].astype(F32))
        before = jnp.where(row == 0, p_prev, pltpu.roll(p, 1, axis=0))
        after = jnp.where(row == tm - 1, p_next, pltpu.roll(p, tm - 1, axis=0))
        conv = cw_ref[0:1, lo:hi] * before + cw_ref[1:2, lo:hi] * p + cw_ref[2:3, lo:hi] * after
        y_conv = zb_ref[0, :, lo:hi].astype(F32) * conv
        ga = _sigmoid(zg_ref[0, :, lo:hi].astype(F32))
        gb = _sigmoid(zg_ref[0, :, width + lo:width + hi].astype(F32))
        parts.append((ga * y_rwkv + gb * y_conv).astype(BF16))
    merged = jnp.concatenate(parts, axis=1)
    mix = _dot(merged, wout_ref[...])
    x2 = x_ref[0] + g1_ref[0] * _rms(mix, npost_ref[...])
    x2_ref[0] = x2
    h2 = _rms(x2, npre_ref[...]) * (1.0 + sc2_ref[0]) + sh2_ref[0]
    h2_ref[0] = h2.astype(h2_ref.dtype)

    logits = _dot(h2, wr_ref[...]) + br_ref[...]
    lane = lax.broadcasted_iota(jnp.int32, logits.shape, 1)
    vals, idxs = [], []
    sel = jnp.zeros(logits.shape, F32)
    work = logits
    for _ in range(TOP_K):
        m = jnp.max(work, axis=-1, keepdims=True)
        idx = jnp.min(jnp.where(work == m, lane, logits.shape[1]), axis=-1, keepdims=True)
        hit = lane == idx
        vals.append(m)
        idxs.append(idx)
        sel = jnp.where(hit, 1.0, sel)
        work = jnp.where(hit, -jnp.inf, work)
    exps = [jnp.exp(vk - vals[0]) for vk in vals]
    denom = exps[0] + exps[1] + exps[2] + exps[3]
    r_i = lax.broadcasted_iota(jnp.int32, (tm, tm), 0)
    c_i = lax.broadcasted_iota(jnp.int32, (tm, tm), 1)
    before_cnt = _dot((c_i < r_i).astype(BF16), sel.astype(BF16))
    n_seg = jnp.floor((jnp.sum(sel, axis=0, keepdims=True) + (SEG_ROWS - 1.0)) * (1.0 / SEG_ROWS))
    e_r = lax.broadcasted_iota(jnp.int32, (logits.shape[1], logits.shape[1]), 0)
    e_c = lax.broadcasted_iota(jnp.int32, (logits.shape[1], logits.shape[1]), 1)
    seg_before = _dot(jnp.broadcast_to(n_seg, (8, logits.shape[1])).astype(BF16), (e_r < e_c).astype(BF16))[0:1]
    slot_base = before_cnt + seg_before * float(SEG_ROWS)
    gate_out = jnp.zeros(logits.shape, F32)
    pos_out = jnp.zeros(logits.shape, jnp.int32)
    for k in range(TOP_K):
        pos_k = jnp.sum(jnp.where(lane == idxs[k], slot_base, 0.0), axis=-1, keepdims=True)
        gate_out = jnp.where(lane == k, exps[k] / denom, gate_out)
        pos_out = jnp.where(lane == k, pos_k.astype(jnp.int32), pos_out)
    gate_ref[0] = gate_out
    pos_ref[0] = pos_out
    nseg_ref[0] = n_seg


def _merge_route(yf, yb, bonus, g, zb, p, zg, x, pr, tm):
    b, t, d = x.shape
    width = yf.shape[2]
    halo = 8
    hb = tm // halo
    last = t // halo - 1
    row = lambda bi, i: (bi, i, 0)
    const = lambda bi, i: (0, 0)
    per_b = lambda bi, i: (bi, 0, 0)
    prev = lambda bi, i: (bi, jnp.maximum(i * hb - 1, 0), 0)
    nxt = lambda bi, i: (bi, jnp.minimum((i + 1) * hb, last), 0)
    lanes = 128
    in_specs = [pl.BlockSpec((1, tm, width), row)] * 6
    in_specs += [pl.BlockSpec((1, halo, width), prev), pl.BlockSpec((1, halo, width), nxt),
                 pl.BlockSpec((1, tm, 2 * width), row), pl.BlockSpec((1, tm, d), row)]
    params = [pr["ln_x_w"], pr["ln_x_b"], pr["conv_w"], pr["w_out"], pr["norm_post_mix"]]
    in_specs += [pl.BlockSpec(a.shape, const) for a in params]
    in_specs.append(pl.BlockSpec((1, 1, d), per_b))
    in_specs.append(pl.BlockSpec(pr["norm_pre_ffn"].shape, const))
    in_specs += [pl.BlockSpec((1, 1, d), per_b)] * 2
    in_specs += [pl.BlockSpec(pr["w_router"].shape, const), pl.BlockSpec(pr["b_router"].shape, const)]
    nt = t // tm
    out_specs = [pl.BlockSpec((1, tm, d), row), pl.BlockSpec((1, tm, d), row)]
    out_specs += [pl.BlockSpec((1, tm, lanes), row)] * 2
    out_specs.append(pl.BlockSpec((1, 1, lanes), lambda bi, i: (bi * nt + i, 0, 0)))
    out_shape = [jax.ShapeDtypeStruct((b, t, d), F32), jax.ShapeDtypeStruct((b, t, d), BF16),
                 jax.ShapeDtypeStruct((b, t, lanes), F32), jax.ShapeDtypeStruct((b, t, lanes), jnp.int32),
                 jax.ShapeDtypeStruct((b * nt, 1, lanes), F32)]
    return pl.pallas_call(
        functools.partial(_merge_kernel, width=width),
        grid=(b, nt),
        in_specs=in_specs,
        out_specs=out_specs,
        out_shape=out_shape,
        compiler_params=_cparams(("arbitrary", "arbitrary")),
        name="merge_route",
    )(yf, yb, bonus, g, zb, p, p, p, zg, x, *params, pr["g1"], pr["norm_pre_ffn"], pr["sc2"], pr["sh2"],
      pr["w_router"], pr["b_router"])


def _seg_copy(src_ref, src_row, dst_ref, dst_row, sem):
    al = lambda r: r if isinstance(r, int) else pl.multiple_of(r, SEG_ROWS)
    return pltpu.make_async_copy(src_ref.at[pl.ds(al(src_row), SEG_ROWS)],
                                 dst_ref.at[pl.ds(al(dst_row), SEG_ROWS)], sem)


def _start(cp, priority):
    cp.start(priority=priority)


def _wait_segments(n, src_ref, dst_ref, sem):
    for bit in (128, 64, 32, 16, 8, 4, 2, 1):
        @pl.when((n & bit) != 0)
        def _():
            rows = pl.ds(0, bit * SEG_ROWS)
            pltpu.make_async_copy(src_ref.at[rows], dst_ref.at[rows], sem).wait()


def _for_segment_pairs(n, copy, fn):
    def pair(i, carry):
        fn(copy(2 * i), 0)
        fn(copy(2 * i + 1), 1)
        return carry
    lax.fori_loop(0, n // 2, pair, 0)

    @pl.when(n % 2 == 1)
    def _():
        fn(copy(n - 1), 0)


def _sort_onehot(pos, n_sorted, values=None):
    col = lax.broadcasted_iota(jnp.int32, (pos.shape[0], n_sorted), 1)
    out = jnp.zeros((pos.shape[0], n_sorted), F32)
    for k in range(TOP_K):
        out = jnp.where(col == pos[:, k:k + 1], 1.0 if values is None else values[:, k:k + 1], out)
    return out.astype(BF16)


def _dispatch_kernel(dst_ref, nseg_ref, nrows_ref, h2_ref, pos_ref, xb_ref, zseg, sbuf, sem, zsem):
    step = pl.program_id(0)
    slot = step % 2
    n_fill_blocks = nrows_ref.shape[0]

    def fill_segments(fn):
        def per_block(j, carry):
            def one(q, c2):
                fn(_seg_copy(zseg, 0, xb_ref, j * MOE_BLOCK + q * SEG_ROWS, zsem))
                return c2
            return lax.fori_loop(nrows_ref[j] // SEG_ROWS, MOE_BLOCK // SEG_ROWS, one, carry)
        lax.fori_loop(0, n_fill_blocks, per_block, 0)

    @pl.when(step == 0)
    def _():
        zseg[...] = jnp.zeros_like(zseg)
        fill_segments(lambda cp: cp.start())

    pos_t = pos_ref[...].T
    srow = lax.broadcasted_iota(jnp.int32, (sbuf.shape[1], pos_t.shape[1]), 0)
    onehot = jnp.zeros(srow.shape, F32)
    for k in range(TOP_K):
        onehot = jnp.where(srow == pos_t[k:k + 1, :], 1.0, onehot)
    sbuf[slot] = _dot(onehot.astype(BF16), h2_ref[...]).astype(BF16)

    copy = lambda g: _seg_copy(sbuf.at[slot], g * SEG_ROWS, xb_ref, dst_ref[step * SEG_MAX + g], sem.at[slot])
    _for_segment_pairs(nseg_ref[step], copy, _start)

    @pl.when(step > 0)
    def _():
        _wait_segments(nseg_ref[step - 1], sbuf.at[1 - slot], xb_ref, sem.at[1 - slot])

    @pl.when(step == pl.num_programs(0) - 1)
    def _():
        _wait_segments(nseg_ref[step], sbuf.at[slot], xb_ref, sem.at[slot])
        fill_segments(lambda cp: cp.wait())


def _dispatch(h2, pos, dst_seg, n_seg, block_rows):
    n_tok, d = h2.shape
    n_rows = block_rows.shape[0] * MOE_BLOCK
    tile = lambda i, *_: (i, 0)
    grid_spec = pltpu.PrefetchScalarGridSpec(
        num_scalar_prefetch=3,
        grid=(n_tok // MOE_TILE,),
        in_specs=[pl.BlockSpec((MOE_TILE, d), tile), pl.BlockSpec((MOE_TILE, pos.shape[1]), tile)],
        out_specs=pl.BlockSpec(memory_space=pl.ANY),
        scratch_shapes=[pltpu.VMEM((SEG_ROWS, d), BF16), pltpu.VMEM((2, SEG_MAX * SEG_ROWS, d), BF16),
                        pltpu.SemaphoreType.DMA((2,)), pltpu.SemaphoreType.DMA],
    )
    return pl.pallas_call(
        _dispatch_kernel,
        grid_spec=grid_spec,
        out_shape=jax.ShapeDtypeStruct((n_rows, d), BF16),
        compiler_params=_cparams(("arbitrary",)),
        name="moe_dispatch",
    )(dst_seg, n_seg, block_rows, h2, pos)


def _expert_kernel(be_ref, nrows_ref, first_ref, slot_ref, next_ref, xb_ref, win_hbm, bin_ref, wout_hbm, bout_ref,
                   o_ref, win_f32, wout_f32, win_scr, wout_scr, sem, *, d_ff):
    step = pl.program_id(0)
    n_sub = xb_ref.shape[0] // MOE_BLOCK

    def weight_copies(e, s):
        return (pltpu.make_async_copy(win_hbm.at[e], win_f32.at[s], sem.at[s, 0]),
                pltpu.make_async_copy(wout_hbm.at[e], wout_f32.at[s], sem.at[s, 1]))

    @pl.when(step == 0)
    def _():
        for cp in weight_copies(be_ref[0], 0):
            cp.start()

    def block(sub, carry):
        i = step * n_sub + sub
        rows = pl.ds(pl.multiple_of(sub * MOE_BLOCK, MOE_BLOCK), MOE_BLOCK)
        n_rows = nrows_ref[i]
        e = be_ref[i]

        @pl.when(first_ref[i] == 1)
        def _():
            s = slot_ref[i]
            for cp in weight_copies(e, s):
                cp.wait()
            win_scr[...] = win_f32[s].astype(BF16)
            wout_scr[...] = wout_f32[s].astype(BF16)

            @pl.when(next_ref[i] >= 0)
            def _():
                for cp in weight_copies(next_ref[i], 1 - s):
                    cp.start()

        @pl.when(n_rows > 0)
        def _():
            gu = _dot(xb_ref[rows, :], win_scr[...]) + bin_ref[e]
            gate = jnp.minimum(gu[:, :d_ff], SWIGLU_LIMIT)
            up = jnp.clip(gu[:, d_ff:], -SWIGLU_LIMIT, SWIGLU_LIMIT)
            act = (up + 1.0) * gate * _sigmoid(SWIGLU_ALPHA * gate)
            o_ref[rows, :] = (_dot(act.astype(BF16), wout_scr[...]) + bout_ref[e]).astype(o_ref.dtype)

        @pl.when(n_rows == 0)
        def _():
            o_ref[rows, :] = jnp.zeros((MOE_BLOCK, o_ref.shape[1]), o_ref.dtype)
        return carry

    lax.fori_loop(0, n_sub, block, 0)


def _experts(xb, block_e, block_rows, counts, w_in, b_in, w_out, b_out):
    n_rows, d = xb.shape
    n_e, _, d_ff2 = w_in.shape
    d_ff = d_ff2 // 2
    n_blocks = n_rows // MOE_BLOCK
    valid = block_rows > 0
    prev_e = jnp.concatenate([jnp.full((1,), -1, jnp.int32), block_e[:-1]])
    first = jnp.logical_and(valid, block_e != prev_e).astype(jnp.int32)
    slot = ((jnp.cumsum(first) - 1) % 2).astype(jnp.int32)
    experts = jnp.arange(n_e, dtype=jnp.int32)
    later = jnp.logical_and(experts[None, :] > experts[:, None], counts[None, :] > 0)
    next_of = jnp.min(jnp.where(later, experts[None, :], n_e), axis=1)
    next_of = jnp.where(next_of == n_e, -1, next_of)
    next_e = jnp.sum(jnp.where(block_e[:, None] == experts[None, :], next_of[None, :], 0), axis=1).astype(jnp.int32)
    blk = lambda i, *_: (i, 0)
    whole = lambda i, *_: (0, 0, 0)
    step_rows = EXPERT_SUB * MOE_BLOCK
    grid_spec = pltpu.PrefetchScalarGridSpec(
        num_scalar_prefetch=5,
        grid=(n_blocks // EXPERT_SUB,),
        in_specs=[pl.BlockSpec((step_rows, d), blk),
                  pl.BlockSpec(memory_space=pl.ANY),
                  pl.BlockSpec((n_e, 1, d_ff2), whole),
                  pl.BlockSpec(memory_space=pl.ANY),
                  pl.BlockSpec((n_e, 1, d), whole)],
        out_specs=pl.BlockSpec((step_rows, d), blk),
        scratch_shapes=[pltpu.VMEM((2, d, d_ff2), F32), pltpu.VMEM((2, d_ff, d), F32),
                        pltpu.VMEM((d, d_ff2), BF16), pltpu.VMEM((d_ff, d), BF16),
                        pltpu.SemaphoreType.DMA((2, 2))],
    )
    return pl.pallas_call(
        functools.partial(_expert_kernel, d_ff=d_ff),
        grid_spec=grid_spec,
        out_shape=jax.ShapeDtypeStruct((n_rows, d), BF16),
        compiler_params=_cparams(("arbitrary",)),
        name="moe_experts",
    )(block_e, block_rows, first, slot, next_e, xb, w_in, b_in.reshape(n_e, 1, d_ff2), w_out,
      b_out.reshape(n_e, 1, d))


def _combine_kernel(dst_ref, nseg_ref, yb_ref, gate_ref, pos_ref, x2_ref, npost_ref, g2_ref, o_ref, ybuf, sem):
    tile = pl.program_id(0) * pl.num_programs(1) + pl.program_id(1)
    n_tiles = pl.num_programs(0) * pl.num_programs(1)
    slot = tile % 2

    def fetch(t_idx, s):
        copy = lambda g: _seg_copy(yb_ref, dst_ref[t_idx * SEG_MAX + g], ybuf.at[s], g * SEG_ROWS, sem.at[s])
        _for_segment_pairs(nseg_ref[t_idx], copy, _start)

    @pl.when(tile == 0)
    def _():
        ybuf[...] = jnp.zeros_like(ybuf)
        fetch(0, 0)

    @pl.when(tile + 1 < n_tiles)
    def _():
        fetch(tile + 1, 1 - slot)

    _wait_segments(nseg_ref[tile], yb_ref, ybuf.at[slot], sem.at[slot])
    weights = _sort_onehot(pos_ref[0], ybuf.shape[1], gate_ref[0])
    f = _dot(weights, ybuf[slot])
    o_ref[0] = x2_ref[0] + g2_ref[0] * _rms(f, npost_ref[...])


def _combine(yb_rows, dst_seg, n_seg, gates, pos, x2, npost, g2):
    b, t, d = x2.shape
    row = lambda bi, i, *_: (bi, i, 0)
    grid_spec = pltpu.PrefetchScalarGridSpec(
        num_scalar_prefetch=2,
        grid=(b, t // MOE_TILE),
        in_specs=[pl.BlockSpec(memory_space=pl.ANY),
                  pl.BlockSpec((1, MOE_TILE, gates.shape[2]), row),
                  pl.BlockSpec((1, MOE_TILE, pos.shape[2]), row),
                  pl.BlockSpec((1, MOE_TILE, d), row),
                  pl.BlockSpec((1, d), lambda bi, i, *_: (0, 0)),
                  pl.BlockSpec((1, 1, d), lambda bi, i, *_: (bi, 0, 0))],
        out_specs=pl.BlockSpec((1, MOE_TILE, d), row),
        scratch_shapes=[pltpu.VMEM((2, SEG_MAX * SEG_ROWS, d), BF16), pltpu.SemaphoreType.DMA((2,))],
    )
    return pl.pallas_call(
        _combine_kernel,
        grid_spec=grid_spec,
        out_shape=jax.ShapeDtypeStruct((b, t, d), F32),
        compiler_params=_cparams(("arbitrary", "arbitrary")),
        name="moe_combine",
    )(dst_seg, n_seg, yb_rows, gates, pos, x2, npost, g2)


def _lora_up(w_up):
    _, r, w = w_up.shape
    z = jnp.zeros((r, w), w_up.dtype)
    return jnp.concatenate([jnp.concatenate([w_up[0], z], axis=1),
                            jnp.concatenate([z, w_up[1]], axis=1)], axis=0).astype(BF16)


def _row_tile(t, pref):
    return pref if t % pref == 0 else t


def kernel(x, c, ctx, c_ctx, w_ada, b_ada, norm_pre_mix, norm_post_mix, norm_pre_ffn, norm_post_ffn,
           w_in, mu_shift, w0, w_decay_up, a0, w_iclr_up, k_k, k_a, r_k, w_gate_up, ln_x_w, ln_x_b,
           conv_w, w_out, w_router, b_router, w_exp_in, b_exp_in, w_exp_out, b_exp_out):
    b, t, d = x.shape
    t_ctx = ctx.shape[1]
    depth = w_ada.shape[0]
    width = k_k.shape[1]
    n_groups = width // MXU_WIDTH
    k_off, v_off = 0, width
    decay_off = 2 * width
    iclr_off = decay_off + 2 * DECAY_LORA
    r_off = iclr_off + 2 * ICLR_LORA
    glora_off = r_off + width
    conv_off = glora_off + GATE_LORA
    gate_off = conv_off + 3 * width

    xc = ctx
    for l in range(depth):
        last = l == depth - 1
        rows = jnp.concatenate([c, c_ctx[None, :], jnp.zeros((8 - b - 1, d), F32)], axis=0)
        mod = _ada(rows, w_ada[l], b_ada[l])
        sh1, sc1, g1, sh2, sc2, g2 = [mod[:b, None, j * d:(j + 1) * d] for j in range(6)]
        csh1, csc1, cg1, csh2, csc2, cg2 = [jnp.broadcast_to(mod[b:b + 1, None, j * d:(j + 1) * d], (b, 1, d))
                                            for j in range(6)]

        wl = w_in[l]
        cols = lambda lo, n: wl[:, lo:lo + n]
        w_main = jnp.concatenate([cols(k_off, width), cols(v_off, width), cols(r_off, width)], axis=1).astype(BF16)
        w_lora = jnp.concatenate([cols(decay_off, 2 * DECAY_LORA), cols(iclr_off, 2 * ICLR_LORA),
                                  cols(glora_off, GATE_LORA)], axis=1).astype(BF16)
        w_conv = tuple(cols(conv_off + j * width, width).astype(BF16) for j in range(3))
        w_gate = cols(gate_off, 2 * d).astype(BF16)
        mu = mu_shift[l]
        mu_main = jnp.concatenate([mu[k_off:k_off + width], mu[v_off:v_off + width], mu[r_off:r_off + width]])[None, :]
        mu_lora = jnp.concatenate([mu[decay_off:decay_off + 2 * DECAY_LORA], mu[iclr_off:iclr_off + 2 * ICLR_LORA],
                                   mu[glora_off:glora_off + GATE_LORA]])[None, :]
        sp = {
            "mu_main": mu_main, "mu_lora": mu_lora,
            "w0": w0[l].reshape(1, 2 * width), "w_dec": _lora_up(w_decay_up[l]),
            "a0": a0[l].reshape(1, 2 * width), "w_iclr": _lora_up(w_iclr_up[l]),
            "k_k": k_k[l][None, :], "k_a": k_a[l][None, :],
            "r_k": r_k[l].reshape(1, width), "w_gate": w_gate_up[l].astype(BF16),
        }
        if not last:
            raise NotImplementedError("context stream update for non-final layers")

        n_state_main = 2 * width
        zc_main, zc_lora = _inproj(xc, norm_pre_mix[l], csc1, csh1,
                                   [w_main[:, :n_state_main], w_lora[:, :2 * DECAY_LORA + 2 * ICLR_LORA]],
                                   None, _row_tile(t_ctx, CTX_ROW_TILE))
        sp_ctx = dict(sp, mu_main=mu_main[:, :n_state_main], mu_lora=mu_lora[:, :2 * DECAY_LORA + 2 * ICLR_LORA])
        kap_c, v_c, ld0_c, ld1_c, kd0_c, kd1_c, be0_c, be1_c = _state_terms(
            zc_main, zc_lora, sp_ctx, grid_mode=False, with_read=False, tm=t_ctx)
        s0 = jnp.zeros((b, 2, n_groups, MXU_WIDTH, MXU_WIDTH), F32)
        (s_ctx,) = _wkv_scan([(ld0_c, kd0_c, be0_c), (ld1_c, kd1_c, be1_c)], kap_c, v_c, None, s0)

        zm, zl, zg, zb, p = _inproj(x, norm_pre_mix[l], sc1, sh1, [w_main, w_lora, w_gate], w_conv,
                                    _row_tile(t, ROW_TILE))
        kap, v, ld0, ld1, kd0, kd1, be0, be1, r, bonus, g = _state_terms(
            zm, zl, sp, grid_mode=True, with_read=True, tm=_row_tile(t, ROW_TILE))
        yf, yb = _wkv_scan([(ld0, kd0, be0), (ld1, kd1, be1)], kap, v, r, s_ctx)

        lanes = 128
        pad_e = lanes - N_EXPERTS
        pr = {
            "ln_x_w": ln_x_w[l][None, :], "ln_x_b": ln_x_b[l][None, :], "conv_w": conv_w[l],
            "w_out": w_out[l].astype(BF16), "norm_post_mix": norm_post_mix[l][None, :], "g1": g1,
            "norm_pre_ffn": norm_pre_ffn[l][None, :], "sc2": sc2, "sh2": sh2,
            "w_router": jnp.pad(w_router[l], ((0, 0), (0, pad_e))),
            "b_router": jnp.pad(b_router[l], (0, pad_e), constant_values=-jnp.inf)[None, :],
        }
        x2, h2, gates, pos, tile_segs = _merge_route(yf, yb, bonus, g, zb, p, zg, x, pr, MOE_TILE)

        n_tok = b * t
        nk = n_tok * TOP_K
        n_tiles = n_tok // MOE_TILE
        experts = jnp.arange(N_EXPERTS, dtype=jnp.int32)
        run_rows = tile_segs[:, 0, :N_EXPERTS].astype(jnp.int32) * SEG_ROWS
        rows_before = jnp.cumsum(run_rows, axis=0) - run_rows
        counts = jnp.sum(run_rows, axis=0)
        padded = (counts + MOE_BLOCK - 1) // MOE_BLOCK * MOE_BLOCK
        pad_ends = jnp.cumsum(padded)
        pad_starts = pad_ends - padded
        run_off = jnp.cumsum(run_rows, axis=1) - run_rows
        n_seg = (jnp.sum(run_rows, axis=1) // SEG_ROWS).astype(jnp.int32)
        seg_row = jnp.arange(SEG_MAX, dtype=jnp.int32) * SEG_ROWS
        seg_e = jnp.minimum(jnp.sum(((run_off + run_rows)[:, None, :] <= seg_row[None, :, None]).astype(jnp.int32),
                                    axis=2), N_EXPERTS - 1)
        shift = pad_starts[None, :] + rows_before - run_off
        dst_seg = (jnp.sum(jnp.where(seg_e[..., None] == experts, shift[:, None, :], 0), axis=2)
                   + seg_row[None, :]).reshape(n_tiles * SEG_MAX).astype(jnp.int32)
        n_rows_max = nk + n_tiles * N_EXPERTS * (SEG_ROWS - 1) + N_EXPERTS * (MOE_BLOCK - 1)
        n_blocks = -(-n_rows_max // (MOE_BLOCK * EXPERT_SUB)) * EXPERT_SUB
        blk_start = jnp.arange(n_blocks, dtype=jnp.int32) * MOE_BLOCK
        block_e = jnp.minimum(jnp.sum((pad_ends[None, :] <= blk_start[:, None]).astype(jnp.int32), axis=1),
                              N_EXPERTS - 1)
        own = block_e[:, None] == experts[None, :]
        seg_end = jnp.sum(jnp.where(own, (pad_starts + counts)[None, :], 0), axis=1)
        block_rows = jnp.clip(seg_end - blk_start, 0, MOE_BLOCK).astype(jnp.int32)
        xb = _dispatch(h2.reshape(n_tok, d), pos.reshape(n_tok, pos.shape[2]), dst_seg, n_seg, block_rows)
        yb_rows = _experts(xb, block_e, block_rows, counts, w_exp_in[l], b_exp_in[l], w_exp_out[l], b_exp_out[l])
        x = _combine(yb_rows, dst_seg, n_seg, gates, pos, x2, norm_post_ffn[l][None, :], g2)
    return x
```
